```python
import math
import jax
import jax.numpy as jnp
from jax import lax
import numpy as np

D_MODEL = 2048
BATCH = 8
SEQ = 4096
DEPTH = 2

CTX_LEN = 256
GRID_W = 64
N_BRANCH = 4
D_BRANCH = D_MODEL // 4
FNET_GROUPS = 4
FNET_GROUP_DIM = D_BRANCH // FNET_GROUPS
HY_ORDER = 2
HY_SHORT = 3
HY_EMB = 33
HY_BANDS = (HY_EMB - 1) // 2
HY_FFN = 64
HY_DECAY_TARGET = 1e-2
HY_FAST_PCT = 0.3
HY_SLOW_PCT = 1.5
SC_WIDTH = 3
LRU_CONV = 4
LRU_HEADS = 8
LRU_HEAD_DIM = D_BRANCH // LRU_HEADS
LRU_C = 8.0
N_GROUPS = 4
EXPERTS_PER_GROUP = 8
N_EXPERTS = N_GROUPS * EXPERTS_PER_GROUP
TOP_K = 2
D_EXPERT = D_MODEL // 2
NORM_EPS = 1e-6

COL_FNET = 0
COL_HYENA = COL_FNET + D_BRANCH
COL_SC = COL_HYENA + 3 * D_BRANCH
COL_LRU = COL_SC + 3 * D_BRANCH
D_IN = COL_LRU + 2 * D_BRANCH

kernel_name = 'hybrid_fnet_hyena_conv_rglru_hmoe_dit'


def rmsnorm(x, g):
    xf = x.astype(jnp.float32)
    y = xf * lax.rsqrt(jnp.mean(xf * xf, axis=-1, keepdims=True) + NORM_EPS)
    return (y * g.astype(jnp.float32)).astype(x.dtype)


def adaln(cvec, w, b, n_chunks):
    m = jax.nn.silu(cvec) @ w + b
    return jnp.split(m[:, None, :], n_chunks, axis=-1)


def modulate(h, shift, scale):
    return h * (1.0 + scale) + shift


def dwconv(u, w, left, row_width):
    b, l, ch = u.shape
    k = w.shape[0]
    rows = l // row_width
    ur = u.reshape(b, rows, row_width, ch)
    up = jnp.pad(ur, ((0, 0), (0, 0), (left, k - 1 - left), (0, 0)))
    y = sum(up[:, :, j:j + row_width] * w[j] for j in range(k))
    return y.reshape(b, l, ch)


def fnet_mix(u):
    b, l, _ = u.shape
    ug = u.astype(jnp.float32).reshape(b, l, FNET_GROUPS, FNET_GROUP_DIM)
    y = jnp.fft.fft2(ug, axes=(1, 3), norm='ortho').real
    return y.reshape(b, l, D_BRANCH).astype(u.dtype)


def hyena_filters(length, p):
    f32 = jnp.float32
    t = jnp.linspace(0.0, 1.0, length, dtype=f32)[:, None]
    w = (2.0 * math.pi / length) * jnp.arange(length, dtype=f32)[:, None]
    bands = jnp.linspace(1e-4, HY_BANDS - 1, HY_BANDS, dtype=f32)[None, :]
    feats = jnp.concatenate([t, jnp.cos(bands * w), -jnp.sin(bands * w)], axis=-1)
    freq = p['hf_freq'].astype(f32)
    z = jnp.sin(freq[0] * (feats @ p['hf_w1'].astype(f32) + p['hf_b1'].astype(f32)))
    z = jnp.sin(freq[1] * (z @ p['hf_w2'].astype(f32) + p['hf_b2'].astype(f32)))
    filt = (z @ p['hf_w3'].astype(f32)).reshape(length, HY_ORDER, 2, D_BRANCH)
    min_decay = math.log(HY_DECAY_TARGET) / HY_SLOW_PCT
    max_decay = math.log(HY_DECAY_TARGET) / HY_FAST_PCT
    deltas = jnp.abs(jnp.linspace(min_decay, max_decay, D_BRANCH, dtype=f32))
    filt = filt * jnp.exp(-t * deltas)[:, None, None, :]
    return filt * lax.rsqrt(jnp.sum(filt * filt, axis=(0, 2), keepdims=True) + NORM_EPS)


def two_sided_fft_conv(z, h_fwd, h_bwd):
    l = z.shape[1]
    ker = jnp.concatenate([h_fwd, jnp.zeros_like(h_fwd[:1]), h_bwd[1:][::-1]], axis=0)
    zf = jnp.fft.rfft(z, n=2 * l, axis=1)
    kf = jnp.fft.rfft(ker, n=2 * l, axis=0)
    return jnp.fft.irfft(zf * kf[None], n=2 * l, axis=1)[:, :l]


def hyena_mix(u, p, row_width):
    uc = dwconv(u, p['w_hconv'], HY_SHORT // 2, row_width)
    v, g1, g2 = jnp.split(uc.astype(jnp.float32), 3, axis=-1)
    filt = hyena_filters(u.shape[1], p)
    bias = p['hy_bias'].astype(jnp.float32)
    z = v
    for n, gate in enumerate((g1, g2)):
        z = gate * (two_sided_fft_conv(z, filt[:, n, 0], filt[:, n, 1]) + bias[n] * z)
    return z.astype(u.dtype)


def shortconv_mix(u, p, row_width):
    gb, gc, hin = jnp.split(u, 3, axis=-1)
    return gb * dwconv(gc * hin, p['w_sconv'], SC_WIDTH // 2, row_width)


def _linear_recurrence(e1, e2):
    a1, b1 = e1
    a2, b2 = e2
    return a1 * a2, a2 * b1 + b2


def rglru_bidir(xc, p, h0_fwd, h0_bwd):
    f32 = jnp.float32
    b, l, ch = xc.shape
    xf = xc.astype(f32)
    xh = xf.reshape(b, l, LRU_HEADS, LRU_HEAD_DIM)
    seqs = []
    for d, (h0, rev) in enumerate(((h0_fwd, False), (h0_bwd, True))):
        r = jax.nn.sigmoid(jnp.einsum('blhi,hij->blhj', xh, p['lru_wa'][d].astype(f32)).reshape(b, l, ch)
                           + p['lru_ba'][d].astype(f32))
        i = jax.nn.sigmoid(jnp.einsum('blhi,hij->blhj', xh, p['lru_wx'][d].astype(f32)).reshape(b, l, ch)
                           + p['lru_bx'][d].astype(f32))
        log_a = -LRU_C * r * jax.nn.softplus(-p['lru_lambda'][d].astype(f32))
        a = jnp.exp(log_a)
        inp = jnp.sqrt(-jnp.expm1(2.0 * log_a)) * (i * xf)
        edge = l - 1 if rev else 0
        inp = inp.at[:, edge].add(a[:, edge] * h0)
        _, h = lax.associative_scan(_linear_recurrence, (a, inp), reverse=rev, axis=1)
        seqs.append(h)
    return seqs[0], seqs[1]


def sequence_mixer(h, p, row_width, h0_fwd, h0_bwd, return_states):
    u = h @ p['w_in']
    y_fnet = fnet_mix(u[..., COL_FNET:COL_HYENA])
    y_hyena = hyena_mix(u[..., COL_HYENA:COL_SC], p, row_width)
    y_sconv = shortconv_mix(u[..., COL_SC:COL_LRU], p, row_width)
    u_lx, u_lg = jnp.split(u[..., COL_LRU:D_IN], 2, axis=-1)
    h_f, h_b = rglru_bidir(dwconv(u_lx, p['w_lconv'], LRU_CONV // 2, row_width), p, h0_fwd, h0_bwd)
    y_lru = (h_f + h_b).astype(h.dtype) * jax.nn.gelu(u_lg)
    branches = (y_fnet, y_hyena, y_sconv, y_lru)
    merged = sum(jax.nn.sigmoid(h @ p['w_merge'][k]) * (branches[k] @ p['w_branch'][k])
                 for k in range(N_BRANCH))
    out = merged @ p['w_out']
    if return_states:
        return out, h_f[:, -1], h_b[:, 0]
    return out


def context_lru_states(hc, p):
    u_lx = hc @ p['w_in'][:, COL_LRU:COL_LRU + D_BRANCH]
    zeros = jnp.zeros((hc.shape[0], D_BRANCH), jnp.float32)
    h_f, h_b = rglru_bidir(dwconv(u_lx, p['w_lconv'], LRU_CONV // 2, hc.shape[1]), p, zeros, zeros)
    return h_f[:, -1], h_b[:, 0]


def hierarchical_moe(h, p):
    f32 = jnp.float32
    lg = jnp.einsum('bld,dg->blg', h, p['w_rg']).astype(f32) + p['b_rg'].astype(f32)
    p_top, g_top = lax.top_k(jax.nn.softmax(lg, axis=-1), 1)
    le = jnp.einsum('bld,gde->blge', h, p['w_re']).astype(f32) + p['b_re'].astype(f32)
    le_sel = jnp.sum(le * jax.nn.one_hot(g_top[..., 0], N_GROUPS, dtype=f32)[..., None], axis=2)
    v_top, e_top = lax.top_k(le_sel, TOP_K)
    w_top = jax.nn.softmax(v_top, axis=-1) * p_top
    expert_id = g_top * EXPERTS_PER_GROUP + e_top
    combine = jnp.sum(jax.nn.one_hot(expert_id, N_EXPERTS, dtype=f32) * w_top[..., None], axis=2)
    combine = combine.astype(h.dtype)
    y = jnp.zeros_like(h)
    for e in range(N_EXPERTS):
        hid = jax.nn.silu(h @ p['w_e_gate'][e]) * (h @ p['w_e_up'][e])
        y = y + combine[..., e:e + 1] * (hid @ p['w_e_down'][e])
    return y


def setup_inputs(seed: int = 0) -> dict:
    key = jax.random.key(seed)
    ks = iter(jax.random.split(key, 40))
    f32 = jnp.float32

    def nrm(shape, scale):
        return scale * jax.random.normal(next(ks), shape, f32)

    L = DEPTH
    x = nrm((BATCH, SEQ, D_MODEL), 1.0)
    c = nrm((BATCH, D_MODEL), 1.0)
    ctx = nrm((BATCH, CTX_LEN, D_MODEL), 1.0)
    c_ctx = nrm((D_MODEL,), 1.0)
    w_mod = nrm((L, D_MODEL, 6 * D_MODEL), 0.5 * D_MODEL ** -0.5)
    b_mod = nrm((L, 6 * D_MODEL), 0.01)
    g_norm1 = 1.0 + nrm((L, D_MODEL), 0.05)
    g_norm2 = 1.0 + nrm((L, D_MODEL), 0.05)
    g_final = 1.0 + nrm((D_MODEL,), 0.05)
    w_in = nrm((L, D_MODEL, D_IN), D_MODEL ** -0.5)
    w_merge = nrm((L, N_BRANCH, D_MODEL, D_MODEL), D_MODEL ** -0.5)
    w_branch = nrm((L, N_BRANCH, D_BRANCH, D_MODEL), D_BRANCH ** -0.5)
    w_out = nrm((L, D_MODEL, D_MODEL), D_MODEL ** -0.5)
    w_hconv = nrm((L, HY_SHORT, 3 * D_BRANCH), HY_SHORT ** -0.5)
    hy_bias = nrm((L, HY_ORDER, D_BRANCH), 0.5)
    hf_w1 = nrm((L, HY_EMB, HY_FFN), HY_EMB ** -0.5)
    hf_b1 = nrm((L, HY_FFN), 0.1)
    hf_w2 = nrm((L, HY_FFN, HY_FFN), HY_FFN ** -0.5)
    hf_b2 = nrm((L, HY_FFN), 0.1)
    hf_w3 = nrm((L, HY_FFN, HY_ORDER * 2 * D_BRANCH), HY_FFN ** -0.5)
    hf_freq = 1.0 + nrm((L, 2, HY_FFN), 0.1)
    w_sconv = nrm((L, SC_WIDTH, D_BRANCH), SC_WIDTH ** -0.5)
    w_lconv = nrm((L, LRU_CONV, D_BRANCH), LRU_CONV ** -0.5)
    lru_wa = nrm((L, 2, LRU_HEADS, LRU_HEAD_DIM, LRU_HEAD_DIM), LRU_HEAD_DIM ** -0.5)
    lru_ba = nrm((L, 2, D_BRANCH), 0.05)
    lru_wx = nrm((L, 2, LRU_HEADS, LRU_HEAD_DIM, LRU_HEAD_DIM), LRU_HEAD_DIM ** -0.5)
    lru_bx = nrm((L, 2, D_BRANCH), 0.05)
    a_pow_c = jax.random.uniform(next(ks), (L, 2, D_BRANCH), f32, 0.9, 0.999)
    a_base = a_pow_c ** (1.0 / LRU_C)
    lru_lambda = jnp.log(a_base) - jnp.log1p(-a_base)
    w_rg = nrm((L, D_MODEL, N_GROUPS), D_MODEL ** -0.5)
    b_rg = nrm((L, N_GROUPS), 0.01)
    w_re = nrm((L, N_GROUPS, D_MODEL, EXPERTS_PER_GROUP), D_MODEL ** -0.5)
    b_re = nrm((L, N_GROUPS, EXPERTS_PER_GROUP), 0.01)
    w_e_gate = nrm((L, N_EXPERTS, D_MODEL, D_EXPERT), D_MODEL ** -0.5)
    w_e_up = nrm((L, N_EXPERTS, D_MODEL, D_EXPERT), D_MODEL ** -0.5)
    w_e_down = nrm((L, N_EXPERTS, D_EXPERT, D_MODEL), D_EXPERT ** -0.5)
    return {'x': x, 'c': c, 'ctx': ctx, 'c_ctx': c_ctx, 'w_mod': w_mod, 'b_mod': b_mod,
            'g_norm1': g_norm1, 'g_norm2': g_norm2, 'g_final': g_final, 'w_in': w_in,
            'w_merge': w_merge, 'w_branch': w_branch, 'w_out': w_out, 'w_hconv': w_hconv,
            'hy_bias': hy_bias, 'hf_w1': hf_w1, 'hf_b1': hf_b1, 'hf_w2': hf_w2, 'hf_b2': hf_b2,
            'hf_w3': hf_w3, 'hf_freq': hf_freq, 'w_sconv': w_sconv, 'w_lconv': w_lconv,
            'lru_wa': lru_wa, 'lru_ba': lru_ba, 'lru_wx': lru_wx, 'lru_bx': lru_bx,
            'lru_lambda': lru_lambda, 'w_rg': w_rg, 'b_rg': b_rg, 'w_re': w_re, 'b_re': b_re,
            'w_e_gate': w_e_gate, 'w_e_up': w_e_up, 'w_e_down': w_e_down}


def reference(x, c, ctx, c_ctx, w_mod, b_mod, g_norm1, g_norm2, g_final, w_in, w_merge, w_branch,
              w_out, w_hconv, hy_bias, hf_w1, hf_b1, hf_w2, hf_b2, hf_w3, hf_freq, w_sconv, w_lconv,
              lru_wa, lru_ba, lru_wx, lru_bx, lru_lambda, w_rg, b_rg, w_re, b_re, w_e_gate, w_e_up,
              w_e_down):
    ctx_len = ctx.shape[1]
    c_ctx_row = c_ctx[None, :]
    for l in range(DEPTH):
        p = dict(w_in=w_in[l], w_merge=w_merge[l], w_branch=w_branch[l], w_out=w_out[l],
                 w_hconv=w_hconv[l], hy_bias=hy_bias[l], hf_w1=hf_w1[l], hf_b1=hf_b1[l],
                 hf_w2=hf_w2[l], hf_b2=hf_b2[l], hf_w3=hf_w3[l], hf_freq=hf_freq[l],
                 w_sconv=w_sconv[l], w_lconv=w_lconv[l], lru_wa=lru_wa[l], lru_ba=lru_ba[l],
                 lru_wx=lru_wx[l], lru_bx=lru_bx[l], lru_lambda=lru_lambda[l], w_rg=w_rg[l],
                 b_rg=b_rg[l], w_re=w_re[l], b_re=b_re[l], w_e_gate=w_e_gate[l],
                 w_e_up=w_e_up[l], w_e_down=w_e_down[l])
        last = l == DEPTH - 1
        sh1, sc1, ga1, sh2, sc2, ga2 = adaln(c, w_mod[l], b_mod[l], 6)
        hx = modulate(rmsnorm(x, g_norm1[l]), sh1, sc1)
        if last:
            csh1, csc1 = adaln(c_ctx_row, w_mod[l][:, :2 * D_MODEL], b_mod[l][:2 * D_MODEL], 2)
            hc = modulate(rmsnorm(ctx, g_norm1[l]), csh1, csc1)
            st_f, st_b = context_lru_states(hc, p)
        else:
            csh1, csc1, cga1, csh2, csc2, cga2 = adaln(c_ctx_row, w_mod[l], b_mod[l], 6)
            hc = modulate(rmsnorm(ctx, g_norm1[l]), csh1, csc1)
            zeros = jnp.zeros((ctx.shape[0], D_BRANCH), jnp.float32)
            mix_c, st_f, st_b = sequence_mixer(hc, p, ctx_len, zeros, zeros, True)
            ctx_mid = ctx + cga1 * mix_c
        x = x + ga1 * sequence_mixer(hx, p, GRID_W, st_f, st_b, False)
        hx2 = modulate(rmsnorm(x, g_norm2[l]), sh2, sc2)
        if last:
            x = x + ga2 * hierarchical_moe(hx2, p)
        else:
            hc2 = modulate(rmsnorm(ctx_mid, g_norm2[l]), csh2, csc2)
            y_all = hierarchical_moe(jnp.concatenate([hc2, hx2], axis=1), p)
            ctx = ctx_mid + cga2 * y_all[:, :ctx_len]
            x = x + ga2 * y_all[:, ctx_len:]
    return rmsnorm(x, g_final)
```

```python
import functools
import math

import jax
import jax.numpy as jnp
from jax import lax
from jax.experimental import pallas as pl
from jax.experimental.pallas import tpu as pltpu

F32 = jnp.float32
BF16 = jnp.bfloat16
HIGHEST = lax.Precision.HIGHEST

NORM_EPS = 1e-6
GRID_W = 64
FNET_GROUPS = 4
LRU_C = 8.0
HY_DECAY_TARGET = 1e-2
HY_FAST_PCT = 0.3
HY_SLOW_PCT = 1.5
TOP_K = 2

TM = 512
TL = 512
TR = 512
TM_MOE = 512
TM_COMB = 256
TN_IN = 1536
TN_MERGE = 512
TF_MOE = 512
VMEM_LIMIT = 56 * 1024 * 1024


def _cp(*sem):
    return pltpu.CompilerParams(dimension_semantics=sem, vmem_limit_bytes=VMEM_LIMIT)


def _sigmoid(x):
    return 1.0 / (1.0 + jnp.exp(-x))


def _adaln_kernel(c_ref, w_ref, b_ref, o_ref):
    c = c_ref[...]
    s = c * _sigmoid(c)
    o_ref[...] = jnp.dot(s, w_ref[...], preferred_element_type=F32, precision=HIGHEST) + b_ref[...]


def _adaln(cvec, w_mod, b_mod):
    nl, d, n6 = w_mod.shape
    r = cvec.shape[0]
    tn = min(1024, n6)
    return pl.pallas_call(
        _adaln_kernel,
        grid=(nl, n6 // tn),
        in_specs=[pl.BlockSpec((r, d), lambda l, j: (0, 0)),
                  pl.BlockSpec((None, d, tn), lambda l, j: (l, 0, j)),
                  pl.BlockSpec((None, 1, tn), lambda l, j: (l, 0, j))],
        out_specs=pl.BlockSpec((None, r, tn), lambda l, j: (l, 0, j)),
        out_shape=jax.ShapeDtypeStruct((nl, r, n6), F32),
        compiler_params=_cp("parallel", "parallel"),
        name="adaln",
    )(cvec, w_mod, b_mod.reshape(nl, 1, n6))


def _norm_mod(x, g, shift, scale):
    y = x * lax.rsqrt(jnp.mean(x * x, axis=-1, keepdims=True) + NORM_EPS) * g
    return y * (1.0 + scale) + shift


def _mod_spec(d, chunk, nlt, tpb, nb):
    return pl.BlockSpec((None, 1, d), lambda i, *_: (jnp.where(i < nlt, i // tpb, nb), 0, chunk))


def _inproj_kernel(x_ref, sh_ref, sc_ref, g_ref, w_ref, u_ref, h_ref, hbf):
    @pl.when(pl.program_id(1) == 0)
    def _():
        hb = _norm_mod(x_ref[...], g_ref[...], sh_ref[...], sc_ref[...]).astype(BF16)
        hbf[...] = hb
        h_ref[...] = hb

    u_ref[...] = jnp.dot(hbf[...], w_ref[...], preferred_element_type=F32)


def _inproj(x_all, mod3, g, w_bf, nlt, tpb, nb):
    t, d = x_all.shape
    n = w_bf.shape[1]
    tn = TN_IN if n % TN_IN == 0 else n
    return pl.pallas_call(
        _inproj_kernel,
        grid=(t // TM, n // tn),
        in_specs=[pl.BlockSpec((TM, d), lambda i, j: (i, 0)),
                  _mod_spec(d, 0, nlt, tpb, nb), _mod_spec(d, 1, nlt, tpb, nb),
                  pl.BlockSpec((1, d), lambda i, j: (0, 0)),
                  pl.BlockSpec((d, tn), lambda i, j: (0, j))],
        out_specs=[pl.BlockSpec((TM, tn), lambda i, j: (i, j)),
                   pl.BlockSpec((TM, d), lambda i, j: (i, 0))],
        out_shape=[jax.ShapeDtypeStruct((t, n), F32), jax.ShapeDtypeStruct((t, d), BF16)],
        scratch_shapes=[pltpu.VMEM((TM, d), BF16)],
        compiler_params=_cp("parallel", "arbitrary"),
        name="inproj",
    )(x_all, mod3, mod3, g.reshape(1, d), w_bf)


def _dwconv(x, w_ref, left, rw):
    tl = x.shape[0]
    pos = lax.broadcasted_iota(jnp.int32, x.shape, 0) % rw
    acc = None
    for k in range(w_ref.shape[0]):
        off = k - left
        if off == 0:
            term = x
        else:
            shifted = pltpu.roll(x, (-off) % tl, axis=0)
            ok = (pos + off >= 0) & (pos + off < rw)
            term = jnp.where(ok, shifted, 0.0)
        term = term * w_ref[k:k + 1, :]
        acc = term if acc is None else acc + term
    return acc


def _conv_kernel(hv_ref, h1_ref, h2_ref, sb_ref, sc_ref, sh_ref, wh_ref, ws_ref,
                 uc_ref, vbf_ref, ys_ref, *, rw, db):
    for k, ref in enumerate((hv_ref, h1_ref, h2_ref)):
        y = _dwconv(ref[...], wh_ref.at[:, k * db:(k + 1) * db], 1, rw)
        uc_ref[:, k * db:(k + 1) * db] = y
        if k == 0:
            vbf_ref[...] = y.astype(BF16)
    ys_ref[...] = (sb_ref[...] * _dwconv(sc_ref[...] * sh_ref[...], ws_ref, 1, rw)).astype(BF16)


def _conv_stage(u_all, w_hconv, w_sconv, row0, nrows, tl, rw, db, ys_init, out_rows):
    t = out_rows
    ob = row0 // tl
    col = lambda c: pl.BlockSpec((tl, db), lambda i: (ob + i, c))
    aliases = {} if ys_init is None else {8: 2}
    args = [u_all] * 6 + [w_hconv, w_sconv]
    in_specs = [col(1), col(2), col(3), col(4), col(5), col(6),
                pl.BlockSpec(w_hconv.shape, lambda i: (0, 0)),
                pl.BlockSpec(w_sconv.shape, lambda i: (0, 0))]
    if ys_init is not None:
        args.append(ys_init)
        in_specs.append(pl.BlockSpec(memory_space=pl.ANY))
    kern = functools.partial(_conv_kernel, rw=rw, db=db)
    if ys_init is not None:
        kern = lambda *r, _k=kern: _k(*r[:8], *r[9:])
    return pl.pallas_call(
        kern,
        grid=(nrows // tl,),
        in_specs=in_specs,
        out_specs=[pl.BlockSpec((tl, 3 * db), lambda i: (i, 0)),
                   pl.BlockSpec((tl, db), lambda i: (i, 0)),
                   pl.BlockSpec((tl, db), lambda i: (ob + i, 0))],
        out_shape=[jax.ShapeDtypeStruct((nrows, 3 * db), F32),
                   jax.ShapeDtypeStruct((nrows, db), BF16),
                   jax.ShapeDtypeStruct((t, db), BF16)],
        input_output_aliases=aliases,
        compiler_params=_cp("parallel"),
        name="conv",
    )(*args)


def _dft_mats(l):
    n2 = 2 * l
    k = lax.broadcasted_iota(jnp.int32, (l, l), 0)
    n = lax.broadcasted_iota(jnp.int32, (l, l), 1)
    ang = ((k * n) % n2).astype(F32) * (2.0 * math.pi / n2)
    c = jnp.cos(ang)
    s = -jnp.sin(ang)
    alt = jnp.where(n % 2 == 0, 1.0, -1.0).astype(F32)
    fwd = jnp.stack([c, jnp.where(k == 0, alt, s)]).astype(BF16)
    alt_t = jnp.where(k % 2 == 0, 1.0, -1.0).astype(F32)
    inv = jnp.stack([c, jnp.where(n == 0, alt_t, s)]).astype(BF16)
    return fwd, inv


def _fnet_mats(l, gdim, groups):
    k = lax.broadcasted_iota(jnp.int32, (l, l), 0)
    n = lax.broadcasted_iota(jnp.int32, (l, l), 1)
    ang = ((k * n) % l).astype(F32) * (2.0 * math.pi / l)
    dl = jnp.stack([jnp.cos(ang), -jnp.sin(ang)]).astype(BF16)
    m = lax.broadcasted_iota(jnp.int32, (gdim, gdim), 0)
    g = lax.broadcasted_iota(jnp.int32, (gdim, gdim), 1)
    ang_g = ((m * g) % gdim).astype(F32) * (2.0 * math.pi / gdim)
    eye = jnp.eye(groups, dtype=F32)
    cg = jnp.kron(eye, jnp.cos(ang_g))
    sg = jnp.kron(eye, jnp.sin(ang_g))
    return dl, jnp.concatenate([cg, sg], axis=1).astype(BF16)


def _fnet_pq_kernel(u_ref, w_ref, o_ref):
    o_ref[...] = jnp.dot(u_ref[...].astype(BF16), w_ref[...], preferred_element_type=F32).astype(BF16)


def _fnet_pq(u_all, w_pq, db):
    t = u_all.shape[0]
    return pl.pallas_call(
        _fnet_pq_kernel,
        grid=(t // TM,),
        in_specs=[pl.BlockSpec((TM, db), lambda i: (i, 0)),
                  pl.BlockSpec(w_pq.shape, lambda i: (0, 0))],
        out_specs=pl.BlockSpec((TM, 2 * db), lambda i: (i, 0)),
        out_shape=jax.ShapeDtypeStruct((t, 2 * db), BF16),
        compiler_params=_cp("parallel"),
        name="fnet_pq",
    )(u_all, w_pq)


def _fnet_dft_kernel(d_ref, x_ref, *rest, db, scale):
    o_ref = rest[-1]
    acc = jnp.dot(d_ref[0], x_ref[:, :db], preferred_element_type=F32)
    acc = acc + jnp.dot(d_ref[1], x_ref[:, db:], preferred_element_type=F32)
    o_ref[...] = (acc * scale).astype(BF16)


def _fnet_dft(dl, pq, row0, nb, l, db, gdim, out_init, t_all):
    tr = min(TR, l)
    nrt = l // tr
    args = [dl, pq]
    in_specs = [pl.BlockSpec((2, tr, l), lambda i, b: (0, i, 0)),
                pl.BlockSpec((l, 2 * db), lambda i, b: (row0 // l + b, 0))]
    aliases = {}
    if out_init is not None:
        args.append(out_init)
        in_specs.append(pl.BlockSpec(memory_space=pl.ANY))
        aliases = {2: 0}
    return pl.pallas_call(
        functools.partial(_fnet_dft_kernel, db=db, scale=1.0 / math.sqrt(l * gdim)),
        grid=(nrt, nb),
        in_specs=in_specs,
        out_specs=pl.BlockSpec((tr, db), lambda i, b: (row0 // tr + b * nrt + i, 0)),
        out_shape=jax.ShapeDtypeStruct((t_all, db), BF16),
        input_output_aliases=aliases,
        compiler_params=_cp("parallel", "parallel"),
        name="fnet_dft",
    )(*args)


def _hy_fwd_kernel(f_ref, x_ref, *rest, mult):
    o_ref = rest[-1]
    zre = jnp.dot(f_ref[0], x_ref[...], preferred_element_type=F32)
    zim = jnp.dot(f_ref[1], x_ref[...], preferred_element_type=F32)
    if mult:
        k_ref = rest[0]
        p, q, r = k_ref[0], k_ref[1], k_ref[2]
        o_ref[0] = (zre * p - zim * q).astype(o_ref.dtype)
        o_ref[1] = (zre * q + zim * r).astype(o_ref.dtype)
    else:
        o_ref[0] = zre
        o_ref[1] = zim


def _hy_fwd(fwd, x, nb, l, db, pqr):
    tr = min(TR, l)
    args = [fwd, x]
    in_specs = [pl.BlockSpec((2, tr, l), lambda i, b: (0, i, 0)),
                pl.BlockSpec((l, db), lambda i, b: (b, 0))]
    if pqr is not None:
        args.append(pqr)
        in_specs.append(pl.BlockSpec((3, tr, db), lambda i, b: (0, i, 0)))
    return pl.pallas_call(
        functools.partial(_hy_fwd_kernel, mult=pqr is not None),
        grid=(l // tr, nb),
        in_specs=in_specs,
        out_specs=pl.BlockSpec((None, 2, tr, db), lambda i, b: (b, 0, i, 0)),
        out_shape=jax.ShapeDtypeStruct((nb, 2, l, db), F32 if pqr is None else BF16),
        compiler_params=_cp("parallel", "parallel"),
        name="hy_fwd",
    )(*args)


def _hy_inv_kernel(g_ref, y_ref, zp_ref, gate_ref, bias_ref, *rest):
    o_ref = rest[-1]
    y = jnp.dot(g_ref[0], y_ref[0], preferred_element_type=F32)
    y = y + jnp.dot(g_ref[1], y_ref[1], preferred_element_type=F32)
    o_ref[...] = (gate_ref[...] * (y + bias_ref[...] * zp_ref[...].astype(F32))).astype(BF16)


def _hy_inv(inv, yhat, zprev, zcol, uc, gcol, bias, nb, l, db, out_row0, out_rows, out_init):
    tr = min(TR, l)
    nrt = l // tr
    args = [inv, yhat, zprev, uc, bias]
    in_specs = [pl.BlockSpec((2, tr, l), lambda i, b: (0, i, 0)),
                pl.BlockSpec((None, 2, l, db), lambda i, b: (b, 0, 0, 0)),
                pl.BlockSpec((tr, db), lambda i, b: (b * nrt + i, zcol)),
                pl.BlockSpec((tr, db), lambda i, b: (b * nrt + i, gcol)),
                pl.BlockSpec((1, db), lambda i, b: (0, 0))]
    aliases = {}
    if out_init is not None:
        args.append(out_init)
        in_specs.append(pl.BlockSpec(memory_space=pl.ANY))
        aliases = {5: 0}
    return pl.pallas_call(
        _hy_inv_kernel,
        grid=(nrt, nb),
        in_specs=in_specs,
        out_specs=pl.BlockSpec((tr, db), lambda i, b: (out_row0 // tr + b * nrt + i, 0)),
        out_shape=jax.ShapeDtypeStruct((out_rows, db), BF16),
        input_output_aliases=aliases,
        compiler_params=_cp("parallel", "parallel"),
        name="hy_inv",
    )(*args)


def _hyena_filters(length, hf_w1, hf_b1, hf_w2, hf_b2, hf_w3, hf_freq, db):
    emb = hf_w1.shape[0]
    nbands = (emb - 1) // 2
    t = jnp.linspace(0.0, 1.0, length, dtype=F32)[:, None]
    w = (2.0 * math.pi / length) * jnp.arange(length, dtype=F32)[:, None]
    bands = jnp.linspace(1e-4, nbands - 1, nbands, dtype=F32)[None, :]
    feats = jnp.concatenate([t, jnp.cos(bands * w), -jnp.sin(bands * w)], axis=-1)
    z = jnp.sin(hf_freq[0] * (jnp.dot(feats, hf_w1, precision=HIGHEST) + hf_b1))
    z = jnp.sin(hf_freq[1] * (jnp.dot(z, hf_w2, precision=HIGHEST) + hf_b2))
    filt = jnp.dot(z, hf_w3, precision=HIGHEST).reshape(length, -1, 2, db)
    min_decay = math.log(HY_DECAY_TARGET) / HY_SLOW_PCT
    max_decay = math.log(HY_DECAY_TARGET) / HY_FAST_PCT
    deltas = jnp.abs(jnp.linspace(min_decay, max_decay, db, dtype=F32))
    filt = filt * jnp.exp(-t * deltas)[:, None, None, :]
    return filt * lax.rsqrt(jnp.sum(filt * filt, axis=(0, 2), keepdims=True) + NORM_EPS)


def _filter_spectra(fwd, filt, l, db):
    n_ord = filt.shape[1]
    cols = jnp.transpose(filt, (1, 2, 0, 3))
    cols = cols.at[:, 1, 0, :].set(0.0)
    x = cols.reshape(n_ord * 2 * l, db).astype(BF16)
    spec = _hy_fwd(fwd, x, n_ord * 2, l, db, None).reshape(n_ord, 2, 2, l, db)
    hf, hb = spec[:, 0], spec[:, 1]
    k_re = hf[:, 0] + hb[:, 0]
    k_im = hf[:, 1] - hb[:, 1]
    k_nyq = hf[:, 1, 0] + hb[:, 1, 0]
    first = (jnp.arange(l) == 0)[None, :, None]
    scale = jnp.where(first, 1.0 / (2 * l), 2.0 / (2 * l)).astype(F32)
    p = k_re * scale
    q = jnp.where(first, 0.0, k_im) * scale
    r = jnp.where(first, k_nyq[:, None, :], k_re) * scale
    return jnp.stack([p, q, r], axis=1)


def _lru_kernel(*refs, rw, reverse, final):
    if final:
        (x_ref, w_ref, wa_ref, wx_ref, ba_ref, bx_ref, lam_ref, h0_ref, hf_ref, ug_ref,
         _, o_ref, st_ref, a_s, b_s, carry) = refs
    else:
        (x_ref, w_ref, wa_ref, wx_ref, ba_ref, bx_ref, lam_ref, h0_ref,
         o_ref, st_ref, a_s, b_s, carry) = refs
    tl = x_ref.shape[0]

    @pl.when(pl.program_id(1) == 0)
    def _():
        carry[...] = h0_ref[...]

    xc = _dwconv(x_ref[...], w_ref, w_ref.shape[0] // 2, rw)
    xb = xc.astype(BF16)
    r = _sigmoid(jnp.dot(xb, wa_ref[...], preferred_element_type=F32) + ba_ref[...])
    ig = _sigmoid(jnp.dot(xb, wx_ref[...], preferred_element_type=F32) + bx_ref[...])
    lam = lam_ref[...]
    softplus = jnp.maximum(-lam, 0.0) + jnp.log(1.0 + jnp.exp(-jnp.abs(lam)))
    log_a = -LRU_C * r * softplus
    a = jnp.exp(log_a)
    bv = jnp.sqrt(1.0 - jnp.exp(2.0 * log_a)) * (ig * xc)

    pos8 = lax.broadcasted_iota(jnp.int32, a.shape, 0) % 8
    for s in (1, 2, 4):
        if reverse:
            a_sh, b_sh, ok = pltpu.roll(a, tl - s, axis=0), pltpu.roll(bv, tl - s, axis=0), pos8 < 8 - s
        else:
            a_sh, b_sh, ok = pltpu.roll(a, s, axis=0), pltpu.roll(bv, s, axis=0), pos8 >= s
        bv = jnp.where(ok, a * b_sh + bv, bv)
        a = jnp.where(ok, a * a_sh, a)
    a_s[...] = a
    b_s[...] = bv

    ng = tl // 8

    def body(gi, c):
        g = ng - 1 - gi if reverse else gi
        sl = pl.ds(pl.multiple_of(g * 8, 8), 8)
        h8 = b_s[sl, :] + a_s[sl, :] * c
        b_s[sl, :] = h8
        return h8[0:1, :] if reverse else h8[7:8, :]

    c_out = lax.fori_loop(0, ng, body, carry[...])
    carry[...] = c_out
    st_ref[...] = c_out
    if final:
        ug = ug_ref[...]
        gelu = 0.5 * ug * (1.0 + jnp.tanh(math.sqrt(2.0 / math.pi) * (ug + 0.044715 * ug * ug * ug)))
        o_ref[...] = ((hf_ref[...] + b_s[...]) * gelu).astype(BF16)
    else:
        o_ref[...] = b_s[...]


def _lru_pass(u_all, w_lconv, wa, wx, ba, bx, lam, h0, row0, nb, l, tl, rw, db, reverse, hf=None, out_init=None,
              t_all=None):
    final = hf is not None
    nc = l // tl
    ob = row0 // tl
    cidx = (lambda c: nc - 1 - c) if reverse else (lambda c: c)
    full = lambda a: pl.BlockSpec(a.shape, lambda b, c: (0,) * a.ndim)
    ncolx, ncolg = 7, 8
    args = [u_all, w_lconv, wa, wx, ba, bx, lam, h0]
    in_specs = [pl.BlockSpec((tl, db), lambda b, c: (ob + b * nc + cidx(c), ncolx)),
                full(w_lconv), full(wa), full(wx), full(ba), full(bx), full(lam),
                pl.BlockSpec((None, 1, db), lambda b, c: (b, 0, 0))]
    aliases = {}
    if final:
        args += [hf, u_all]
        in_specs += [pl.BlockSpec((tl, db), lambda b, c: (b * nc + cidx(c), 0)),
                     pl.BlockSpec((tl, db), lambda b, c: (ob + b * nc + cidx(c), ncolg))]
        if out_init is None:
            out_init = jnp.zeros((8, 128), BF16)
        else:
            aliases = {10: 0}
        args.append(out_init)
        in_specs.append(pl.BlockSpec(memory_space=pl.ANY))
        out_spec = pl.BlockSpec((tl, db), lambda b, c: (ob + b * nc + cidx(c), 0))
        out_shape = jax.ShapeDtypeStruct((t_all, db), BF16)
    else:
        out_spec = pl.BlockSpec((tl, db), lambda b, c: (b * nc + cidx(c), 0))
        out_shape = jax.ShapeDtypeStruct((nb * l, db), F32)
    return pl.pallas_call(
        functools.partial(_lru_kernel, rw=rw, reverse=reverse, final=final),
        grid=(nb, nc),
        in_specs=in_specs,
        out_specs=[out_spec, pl.BlockSpec((None, 1, db), lambda b, c: (b, 0, 0))],
        out_shape=[out_shape, jax.ShapeDtypeStruct((nb, 1, db), F32)],
        scratch_shapes=[pltpu.VMEM((tl, db), F32), pltpu.VMEM((tl, db), F32), pltpu.VMEM((1, db), F32)],
        input_output_aliases=aliases,
        compiler_params=_cp("arbitrary", "arbitrary"),
        name="lru_bwd" if reverse else "lru_fwd",
    )(*args)


def _block_diag(w):
    h, hd, _ = w.shape
    eye = jnp.eye(h, dtype=w.dtype)
    return (eye[:, None, :, None] * w[:, :, None, :]).reshape(h * hd, h * hd).astype(BF16)


def _merge_kernel(h_ref, b0_ref, b1_ref, b2_ref, b3_ref, wm_ref, wb_ref, wo_ref, x_ref, ga_ref, o_ref, acc):
    j = pl.program_id(1)

    @pl.when(j == 0)
    def _():
        acc[...] = jnp.zeros_like(acc)

    h = h_ref[...]
    m = None
    for k, br in enumerate((b0_ref, b1_ref, b2_ref, b3_ref)):
        g = jnp.dot(h, wm_ref[k], preferred_element_type=F32)
        p = jnp.dot(br[...], wb_ref[k], preferred_element_type=F32)
        term = _sigmoid(g) * p
        m = term if m is None else m + term
    acc[...] += jnp.dot(m.astype(BF16), wo_ref[...], preferred_element_type=F32)

    @pl.when(j == pl.num_programs(1) - 1)
    def _():
        o_ref[...] = x_ref[...] + ga_ref[...] * acc[...]


def _merge(h_bf, branches, wm_bf, wb_bf, wo_bf, x_all, mod3, ntiles, nlt, tpb, nb):
    t, d = x_all.shape
    db = wb_bf.shape[1]
    tn = min(TN_MERGE, d)
    row = lambda w: pl.BlockSpec((TM, w), lambda i, j: (i, 0))
    return pl.pallas_call(
        _merge_kernel,
        grid=(ntiles, d // tn),
        in_specs=[row(d), row(db), row(db), row(db), row(db),
                  pl.BlockSpec((4, d, tn), lambda i, j: (0, 0, j)),
                  pl.BlockSpec((4, db, tn), lambda i, j: (0, 0, j)),
                  pl.BlockSpec((tn, d), lambda i, j: (j, 0)),
                  row(d), _mod_spec(d, 2, nlt, tpb, nb)],
        out_specs=row(d),
        out_shape=jax.ShapeDtypeStruct((ntiles * TM, d), F32),
        scratch_shapes=[pltpu.VMEM((TM, d), F32)],
        compiler_params=_cp("parallel", "arbitrary"),
        name="merge",
    )(h_bf, *branches, wm_bf, wb_bf, wo_bf, x_all, mod3)


def _router_kernel(x_ref, sh_ref, sc_ref, g_ref, wr_ref, br_ref, h_ref, r_ref, *, ngroups, epg):
    h = _norm_mod(x_ref[...], g_ref[...], sh_ref[...], sc_ref[...])
    h_ref[...] = h
    logits = jnp.dot(h, wr_ref[...], preferred_element_type=F32, precision=HIGHEST) + br_ref[...]
    lane = lax.broadcasted_iota(jnp.int32, logits.shape, 1)
    lane_f = lane.astype(F32)
    neg = jnp.float32(-1e30)
    big = jnp.float32(1e6)
    is_g = lane < ngroups
    lg = jnp.where(is_g, logits, neg)
    mx = jnp.max(lg, axis=-1, keepdims=True)
    g_top = jnp.min(jnp.where(lg == mx, lane_f, big), axis=-1, keepdims=True)
    den = jnp.sum(jnp.where(is_g, jnp.exp(lg - mx), 0.0), axis=-1, keepdims=True)
    p_top = 1.0 / den
    el = lane_f - ngroups
    in_grp = (el >= g_top * epg) & (el < (g_top + 1.0) * epg)
    le = jnp.where(in_grp, logits, neg)
    v1 = jnp.max(le, axis=-1, keepdims=True)
    e1 = jnp.min(jnp.where(le == v1, el, big), axis=-1, keepdims=True)
    le2 = jnp.where(el == e1, neg, le)
    v2 = jnp.max(le2, axis=-1, keepdims=True)
    e2 = jnp.min(jnp.where(le2 == v2, el, big), axis=-1, keepdims=True)
    dlt = jnp.exp(v2 - v1)
    w1 = p_top / (1.0 + dlt)
    w2 = p_top * dlt / (1.0 + dlt)
    out = jnp.where(lane == 0, e1, 0.0)
    out = jnp.where(lane == 1, e2, out)
    out = jnp.where(lane == 2, w1, out)
    out = jnp.where(lane == 3, w2, out)
    r_ref[...] = out


def _router(x_all, mod3, g, w_router, b_router, ntiles, nlt, tpb, nb, ngroups, epg):
    t, d = x_all.shape
    tm = TM_COMB
    f = TM // tm
    mspec = lambda chunk: pl.BlockSpec(
        (None, 1, d), lambda i: (jnp.where(i < nlt * f, i // (tpb * f), nb), 0, chunk))
    return pl.pallas_call(
        functools.partial(_router_kernel, ngroups=ngroups, epg=epg),
        grid=(ntiles * f,),
        in_specs=[pl.BlockSpec((tm, d), lambda i: (i, 0)), mspec(3), mspec(4),
                  pl.BlockSpec((1, d), lambda i: (0, 0)),
                  pl.BlockSpec((d, 128), lambda i: (0, 0)),
                  pl.BlockSpec((1, 128), lambda i: (0, 0))],
        out_specs=[pl.BlockSpec((tm, d), lambda i: (i, 0)), pl.BlockSpec((tm, 128), lambda i: (i, 0))],
        out_shape=[jax.ShapeDtypeStruct((ntiles * TM, d), F32), jax.ShapeDtypeStruct((ntiles * TM, 128), F32)],
        compiler_params=_cp("parallel"),
        name="router",
    )(x_all, mod3, mod3, g.reshape(1, d), w_router, b_router)


def _route_tables(slab, n_experts, tm):
    t = slab.shape[0]
    e = slab[:, 0:TOP_K].astype(jnp.int32).reshape(-1)
    w = slab[:, TOP_K:2 * TOP_K].reshape(-1)
    npairs = e.shape[0]
    n_tiles = npairs // tm + n_experts
    onehot = (e[:, None] == jnp.arange(n_experts, dtype=jnp.int32)[None, :]).astype(jnp.int32)
    csum = jnp.cumsum(onehot, axis=0)
    rank = jnp.sum(onehot * csum, axis=1) - 1
    counts = csum[-1]
    tiles_per_e = (counts + tm - 1) // tm
    tile_end = jnp.cumsum(tiles_per_e)
    starts = (tile_end - tiles_per_e) * tm
    slot = starts[e] + rank
    tok = jnp.arange(npairs, dtype=jnp.int32) // TOP_K
    tok_of_slot = jnp.zeros((n_tiles * tm,), jnp.int32).at[slot].set(tok)
    w_of_slot = jnp.zeros((n_tiles * tm,), F32).at[slot].set(w)
    n_valid = tile_end[-1]
    tile_ids = jnp.arange(n_tiles, dtype=jnp.int32)
    tile_expert = jnp.searchsorted(tile_end, jnp.minimum(tile_ids, n_valid - 1), side="right").astype(jnp.int32)
    return (tok_of_slot.reshape(n_tiles, tm), w_of_slot.reshape(n_tiles * tm, 1), tile_expert,
            n_valid.reshape(1).astype(jnp.int32), slot.reshape(t, TOP_K))


def _gather_rows(idx_hbm_row, src_hbm, idx_smem, dst, sem_i, sem_x):
    n = dst.shape[0]
    cp = pltpu.make_async_copy(idx_hbm_row, idx_smem, sem_i)
    cp.start()
    cp.wait()

    def issue(r, c):
        pltpu.make_async_copy(src_hbm.at[pl.ds(idx_smem[r], 1), :], dst.at[pl.ds(r, 1), :], sem_x).start()
        return c

    lax.fori_loop(0, n, issue, 0)

    def drain(r, c):
        pltpu.make_async_copy(src_hbm.at[pl.ds(0, 1), :], dst.at[pl.ds(r, 1), :], sem_x).wait()
        return c

    lax.fori_loop(0, n, drain, 0)


def _moe_kernel(te_ref, nv_ref, idx_hbm, x_hbm, wg_ref, wu_ref, wd_ref, ws_ref, o_ref,
                idx_smem, xbuf, xbf, acc, sem_i, sem_x):
    i = pl.program_id(0)
    j = pl.program_id(1)
    valid = i < nv_ref[0]

    @pl.when(valid & (j == 0))
    def _():
        _gather_rows(idx_hbm.at[i], x_hbm, idx_smem, xbuf, sem_i, sem_x)
        xbf[...] = xbuf[...].astype(BF16)
        acc[...] = jnp.zeros_like(acc)

    @pl.when(valid)
    def _():
        x = xbf[...]
        g = jnp.dot(x, wg_ref[...].astype(BF16), preferred_element_type=F32)
        u = jnp.dot(x, wu_ref[...].astype(BF16), preferred_element_type=F32)
        hid = (g * _sigmoid(g) * u).astype(BF16)
        acc[...] += jnp.dot(hid, wd_ref[...].astype(BF16), preferred_element_type=F32)

    last = j == pl.num_programs(1) - 1

    @pl.when(valid & last)
    def _():
        o_ref[...] = acc[...] * ws_ref[...]

    @pl.when(jnp.logical_not(valid) & last)
    def _():
        o_ref[...] = jnp.zeros_like(o_ref)


def _moe_experts(h_all, tok_of_slot, w_of_slot, tile_expert, n_valid, w_gate, w_up, w_down):
    n_tiles, tm = tok_of_slot.shape
    d = h_all.shape[1]
    f = w_gate.shape[2]
    tf = min(TF_MOE, f)
    nj = f // tf
    jj = lambda i, j, nv: jnp.where(i < nv[0], j, nj - 1)
    grid_spec = pltpu.PrefetchScalarGridSpec(
        num_scalar_prefetch=2,
        grid=(n_tiles, nj),
        in_specs=[pl.BlockSpec(memory_space=pl.ANY),
                  pl.BlockSpec(memory_space=pl.ANY),
                  pl.BlockSpec((None, d, tf), lambda i, j, te, nv: (te[i], 0, jj(i, j, nv))),
                  pl.BlockSpec((None, d, tf), lambda i, j, te, nv: (te[i], 0, jj(i, j, nv))),
                  pl.BlockSpec((None, tf, d), lambda i, j, te, nv: (te[i], jj(i, j, nv), 0)),
                  pl.BlockSpec((tm, 1), lambda i, j, te, nv: (i, 0))],
        out_specs=pl.BlockSpec((tm, d), lambda i, j, te, nv: (i, 0)),
        scratch_shapes=[pltpu.SMEM((tm,), jnp.int32),
                        pltpu.VMEM((tm, d), F32), pltpu.VMEM((tm, d), BF16), pltpu.VMEM((tm, d), F32),
                        pltpu.SemaphoreType.DMA(()), pltpu.SemaphoreType.DMA(())],
    )
    return pl.pallas_call(
        _moe_kernel,
        grid_spec=grid_spec,
        out_shape=jax.ShapeDtypeStruct((n_tiles * tm, d), F32),
        compiler_params=_cp("arbitrary", "arbitrary"),
        name="moe_experts",
    )(tile_expert, n_valid, tok_of_slot, h_all, w_gate, w_up, w_down, w_of_slot)


def _combine_kernel(p0_hbm, p1_hbm, ys_hbm, x_ref, ga_ref, g_ref, o_ref,
                    i0_smem, i1_smem, buf0, buf1, sem_i, sem_x, *, final):
    i = pl.program_id(0)
    _gather_rows(p0_hbm.at[i], ys_hbm, i0_smem, buf0, sem_i, sem_x)
    _gather_rows(p1_hbm.at[i], ys_hbm, i1_smem, buf1, sem_i, sem_x)
    x = x_ref[...] + ga_ref[...] * (buf0[...] + buf1[...])
    if final:
        x = x * lax.rsqrt(jnp.mean(x * x, axis=-1, keepdims=True) + NORM_EPS) * g_ref[...]
    o_ref[...] = x


def _combine(ys, slot, x_all, mod3, g_final, ntiles, nlt, tpb, nb, final):
    d = x_all.shape[1]
    tm = TM_COMB
    f = TM // tm
    n = ntiles * f
    p0 = slot[:n * tm, 0].reshape(n, tm)
    p1 = slot[:n * tm, 1].reshape(n, tm)
    return pl.pallas_call(
        functools.partial(_combine_kernel, final=final),
        grid=(n,),
        in_specs=[pl.BlockSpec(memory_space=pl.ANY), pl.BlockSpec(memory_space=pl.ANY),
                  pl.BlockSpec(memory_space=pl.ANY),
                  pl.BlockSpec((tm, d), lambda i: (i, 0)),
                  pl.BlockSpec((None, 1, d), lambda i: (jnp.where(i < nlt * f, i // (tpb * f), nb), 0, 5)),
                  pl.BlockSpec((1, d), lambda i: (0, 0))],
        out_specs=pl.BlockSpec((tm, d), lambda i: (i, 0)),
        out_shape=jax.ShapeDtypeStruct((n * tm, d), F32),
        scratch_shapes=[pltpu.SMEM((tm,), jnp.int32), pltpu.SMEM((tm,), jnp.int32),
                        pltpu.VMEM((tm, d), F32), pltpu.VMEM((tm, d), F32),
                        pltpu.SemaphoreType.DMA(()), pltpu.SemaphoreType.DMA(())],
        compiler_params=_cp("arbitrary"),
        name="combine",
    )(p0, p1, ys, x_all, mod3, g_final.reshape(1, d))


def kernel(x, c, ctx, c_ctx, w_mod, b_mod, g_norm1, g_norm2, g_final, w_in, w_merge, w_branch, w_out, w_hconv,
           hy_bias, hf_w1, hf_b1, hf_w2, hf_b2, hf_w3, hf_freq, w_sconv, w_lconv, lru_wa, lru_ba, lru_wx, lru_bx,
           lru_lambda, w_rg, b_rg, w_re, b_re, w_e_gate, w_e_up, w_e_down):
    nb, seq, d = x.shape
    cl = ctx.shape[1]
    depth = w_in.shape[0]
    db = w_branch.shape[2]
    gdim = db // FNET_GROUPS
    ngroups = w_rg.shape[2]
    epg = w_re.shape[3]
    n_experts = ngroups * epg
    t_lat, t_ctx = nb * seq, nb * cl
    t_all = t_lat + t_ctx
    nlt, tpb = t_lat // TM, seq // TM
    nat = t_all // TM
    tl_lat, tl_ctx = min(TL, seq), min(TL, cl)

    x_all = jnp.concatenate([x.reshape(t_lat, d), ctx.reshape(t_ctx, d)], axis=0)
    nrows = -(-(nb + 1) // 8) * 8
    cvec = jnp.zeros((nrows, d), F32).at[:nb].set(c).at[nb].set(c_ctx)
    mods = _adaln(cvec, w_mod, b_mod)

    fwd_lat, inv_lat = _dft_mats(seq)
    fwd_ctx, inv_ctx = _dft_mats(cl)
    dl_lat, w_pq = _fnet_mats(seq, gdim, FNET_GROUPS)
    dl_ctx, _ = _fnet_mats(cl, gdim, FNET_GROUPS)

    out = None
    for l in range(depth):
        last = l == depth - 1
        mod3 = mods[l].reshape(nrows, 1, 6 * d)
        u_all, h_bf = _inproj(x_all, mod3, g_norm1[l], w_in[l].astype(BF16), nlt, tpb, nb)

        wa = [_block_diag(lru_wa[l, dd]) for dd in range(2)]
        wx = [_block_diag(lru_wx[l, dd]) for dd in range(2)]
        ba = [lru_ba[l, dd].reshape(1, db) for dd in range(2)]
        bx = [lru_bx[l, dd].reshape(1, db) for dd in range(2)]
        lam = [lru_lambda[l, dd].reshape(1, db) for dd in range(2)]
        zeros_st = jnp.zeros((nb, 1, db), F32)
        rows_out = t_lat if last else t_all

        def lru(row0, length, tl, rw, h0f, h0b, combine, out_init):
            hf, stf = _lru_pass(u_all, w_lconv[l], wa[0], wx[0], ba[0], bx[0], lam[0], h0f,
                                row0, nb, length, tl, rw, db, False)
            if combine:
                y, stb = _lru_pass(u_all, w_lconv[l], wa[1], wx[1], ba[1], bx[1], lam[1], h0b,
                                   row0, nb, length, tl, rw, db, True, hf=hf, out_init=out_init, t_all=rows_out)
            else:
                y, stb = _lru_pass(u_all, w_lconv[l], wa[1], wx[1], ba[1], bx[1], lam[1], h0b,
                                   row0, nb, length, tl, rw, db, True)
            return y, stf, stb

        def hyena(row0, length, tl, rw, fwd, inv, uc, v_bf, out_init):
            filt = _hyena_filters(length, hf_w1[l], hf_b1[l], hf_w2[l], hf_b2[l], hf_w3[l], hf_freq[l], db)
            pqr = _filter_spectra(fwd, filt, length, db)
            yhat = _hy_fwd(fwd, v_bf, nb, length, db, pqr[0])
            z2 = _hy_inv(inv, yhat, uc, 0, uc, 1, hy_bias[l, 0].reshape(1, db), nb, length, db,
                         0, nb * length, None)
            yhat = _hy_fwd(fwd, z2, nb, length, db, pqr[1])
            return _hy_inv(inv, yhat, z2, 0, uc, 2, hy_bias[l, 1].reshape(1, db), nb, length, db,
                           row0, rows_out, out_init)

        zbuf = lambda: None if last else jnp.zeros((rows_out, db), BF16)
        if last:
            _, st_f, st_b = lru(t_lat, cl, tl_ctx, cl, zeros_st, zeros_st, False, None)
        else:
            y_lru_c, st_f, st_b = lru(t_lat, cl, tl_ctx, cl, zeros_st, zeros_st, True, zbuf())
        y_lru, _, _ = lru(0, seq, tl_lat, GRID_W, st_f, st_b, True, None if last else y_lru_c)

        pq = _fnet_pq(u_all, w_pq, db)
        uc_lat, v_lat, y_sc = _conv_stage(u_all, w_hconv[l], w_sconv[l], 0, t_lat, tl_lat, GRID_W, db, zbuf(),
                                          rows_out)
        y_fn = _fnet_dft(dl_lat, pq, 0, nb, seq, db, gdim, zbuf(), rows_out)
        if not last:
            uc_ctx, v_ctx, y_sc = _conv_stage(u_all, w_hconv[l], w_sconv[l], t_lat, t_ctx, tl_ctx, cl, db, y_sc,
                                              rows_out)
            y_fn = _fnet_dft(dl_ctx, pq, t_lat, nb, cl, db, gdim, y_fn, rows_out)
            y_hy = hyena(t_lat, cl, tl_ctx, cl, fwd_ctx, inv_ctx, uc_ctx, v_ctx, zbuf())
        else:
            y_hy = None
        y_hy = hyena(0, seq, tl_lat, GRID_W, fwd_lat, inv_lat, uc_lat, v_lat, y_hy)

        ntiles = nlt if last else nat
        x_mid = _merge(h_bf, (y_fn, y_hy, y_sc, y_lru), w_merge[l].astype(BF16), w_branch[l].astype(BF16),
                       w_out[l].astype(BF16), x_all, mod3, ntiles, nlt, tpb, nb)

        w_router = jnp.zeros((d, 128), F32).at[:, :ngroups].set(w_rg[l])
        w_router = w_router.at[:, ngroups:ngroups + n_experts].set(
            jnp.transpose(w_re[l], (1, 0, 2)).reshape(d, n_experts))
        b_router = jnp.zeros((1, 128), F32).at[0, :ngroups].set(b_rg[l])
        b_router = b_router.at[0, ngroups:ngroups + n_experts].set(b_re[l].reshape(-1))
        h2, slab = _router(x_mid, mod3, g_norm2[l], w_router, b_router, ntiles, nlt, tpb, nb, ngroups, epg)
        tok_of_slot, w_of_slot, tile_expert, n_valid, slot = _route_tables(slab, n_experts, TM_MOE)
        ys = _moe_experts(h2, tok_of_slot, w_of_slot, tile_expert, n_valid, w_e_gate[l], w_e_up[l], w_e_down[l])
        x_all = _combine(ys, slot, x_mid, mod3, g_final, ntiles, nlt, tpb, nb, last)
        if last:
            out = x_all.reshape(nb, seq, d)
    return out
```

```python
import functools
import math

import jax
import jax.numpy as jnp
from jax import lax
from jax.experimental import pallas as pl
from jax.experimental.pallas import tpu as pltpu

F32 = jnp.float32
BF16 = jnp.bfloat16
HIGHEST = lax.Precision.HIGHEST

NORM_EPS = 1e-6
GRID_W = 64
FNET_GROUPS = 4
LRU_C = 8.0
HY_DECAY_TARGET = 1e-2
HY_FAST_PCT = 0.3
HY_SLOW_PCT = 1.5
TOP_K = 2

TM = 512
TL = 512
TR = 512
TM_MOE = 512
TM_COMB = 256
TN_IN = 1536
TN_MERGE = 512
TF_MOE = 512
VMEM_LIMIT = 56 * 1024 * 1024


def _cp(*sem):
    return pltpu.CompilerParams(dimension_semantics=sem, vmem_limit_bytes=VMEM_LIMIT)


def _sigmoid(x):
    return 1.0 / (1.0 + jnp.exp(-x))


def _adaln_kernel(c_ref, w_ref, b_ref, o_ref):
    c = c_ref[...]
    s = c * _sigmoid(c)
    o_ref[...] = jnp.dot(s, w_ref[...], preferred_element_type=F32, precision=HIGHEST) + b_ref[...]


def _adaln(cvec, w_mod, b_mod):
    nl, d, n6 = w_mod.shape
    r = cvec.shape[0]
    tn = min(1024, n6)
    return pl.pallas_call(
        _adaln_kernel,
        grid=(nl, n6 // tn),
        in_specs=[pl.BlockSpec((r, d), lambda l, j: (0, 0)),
                  pl.BlockSpec((None, d, tn), lambda l, j: (l, 0, j)),
                  pl.BlockSpec((None, 1, tn), lambda l, j: (l, 0, j))],
        out_specs=pl.BlockSpec((None, r, tn), lambda l, j: (l, 0, j)),
        out_shape=jax.ShapeDtypeStruct((nl, r, n6), F32),
        compiler_params=_cp("parallel", "parallel"),
        name="adaln",
    )(cvec, w_mod, b_mod.reshape(nl, 1, n6))


def _norm_mod(x, g, shift, scale):
    y = x * lax.rsqrt(jnp.mean(x * x, axis=-1, keepdims=True) + NORM_EPS) * g
    return y * (1.0 + scale) + shift


def _norm_mod_store(dst, x_ref, g_ref, sh_ref, sc_ref, rows=128):
    rows = min(rows, dst.shape[0])

    def body(c, carry):
        sl = pl.ds(pl.multiple_of(c * rows, rows), rows)
        dst[sl, :] = _norm_mod(x_ref[sl, :], g_ref[...], sh_ref[...], sc_ref[...]).astype(dst.dtype)
        return carry

    lax.fori_loop(0, dst.shape[0] // rows, body, 0)


def _mod_spec(d, chunk, nlt, tpb, nb):
    return pl.BlockSpec((None, 1, d), lambda i, *_: (jnp.where(i < nlt, i // tpb, nb), 0, chunk))


def _inproj_kernel(x_ref, sh_ref, sc_ref, g_ref, w_ref, u_ref, hbf, *, tn):
    j = pl.program_id(1)

    @pl.when(j == 0)
    def _():
        _norm_mod_store(hbf, x_ref, g_ref, sh_ref, sc_ref)

    for jc in range(w_ref.shape[1] // tn):
        @pl.when(j == jc)
        def _(jc=jc):
            u_ref[...] = jnp.dot(hbf[...], w_ref[:, jc * tn:(jc + 1) * tn], preferred_element_type=F32)


def _inproj(x_all, mod3, g, w_bf, nlt, tpb, nb):
    t, d = x_all.shape
    n = w_bf.shape[1]
    tn = TN_IN if n % TN_IN == 0 else n
    return pl.pallas_call(
        functools.partial(_inproj_kernel, tn=tn),
        grid=(t // TM, n // tn),
        in_specs=[pl.BlockSpec((TM, d), lambda i, j: (i, 0)),
                  _mod_spec(d, 0, nlt, tpb, nb), _mod_spec(d, 1, nlt, tpb, nb),
                  pl.BlockSpec((1, d), lambda i, j: (0, 0)),
                  pl.BlockSpec((d, n), lambda i, j: (0, 0))],
        out_specs=pl.BlockSpec((TM, tn), lambda i, j: (i, j)),
        out_shape=jax.ShapeDtypeStruct((t, n), F32),
        scratch_shapes=[pltpu.VMEM((TM, d), BF16)],
        compiler_params=_cp("parallel", "arbitrary"),
        name="inproj",
    )(x_all, mod3, mod3, g.reshape(1, d), w_bf)


def _dwconv(x, w_ref, left, rw):
    tl = x.shape[0]
    pos = lax.broadcasted_iota(jnp.int32, x.shape, 0) % rw
    acc = None
    for k in range(w_ref.shape[0]):
        off = k - left
        if off == 0:
            term = x
        else:
            shifted = pltpu.roll(x, (-off) % tl, axis=0)
            ok = (pos + off >= 0) & (pos + off < rw)
            term = jnp.where(ok, shifted, 0.0)
        term = term * w_ref[k:k + 1, :]
        acc = term if acc is None else acc + term
    return acc


def _conv_kernel(hv_ref, h1_ref, h2_ref, sb_ref, sc_ref, sh_ref, wh_ref, ws_ref,
                 uc_ref, vbf_ref, ys_ref, *, rw, db):
    for k, ref in enumerate((hv_ref, h1_ref, h2_ref)):
        y = _dwconv(ref[...], wh_ref.at[:, k * db:(k + 1) * db], 1, rw)
        uc_ref[:, k * db:(k + 1) * db] = y
        if k == 0:
            vbf_ref[...] = y.astype(BF16)
    ys_ref[...] = (sb_ref[...] * _dwconv(sc_ref[...] * sh_ref[...], ws_ref, 1, rw)).astype(BF16)


def _conv_stage(u_all, w_hconv, w_sconv, row0, nrows, tl, rw, db, ys_init, out_rows):
    t = out_rows
    ob = row0 // tl
    col = lambda c: pl.BlockSpec((tl, db), lambda i: (ob + i, c))
    aliases = {} if ys_init is None else {8: 2}
    args = [u_all] * 6 + [w_hconv, w_sconv]
    in_specs = [col(1), col(2), col(3), col(4), col(5), col(6),
                pl.BlockSpec(w_hconv.shape, lambda i: (0, 0)),
                pl.BlockSpec(w_sconv.shape, lambda i: (0, 0))]
    if ys_init is not None:
        args.append(ys_init)
        in_specs.append(pl.BlockSpec(memory_space=pl.ANY))
    kern = functools.partial(_conv_kernel, rw=rw, db=db)
    if ys_init is not None:
        kern = lambda *r, _k=kern: _k(*r[:8], *r[9:])
    return pl.pallas_call(
        kern,
        grid=(nrows // tl,),
        in_specs=in_specs,
        out_specs=[pl.BlockSpec((tl, 3 * db), lambda i: (i, 0)),
                   pl.BlockSpec((tl, db), lambda i: (i, 0)),
                   pl.BlockSpec((tl, db), lambda i: (ob + i, 0))],
        out_shape=[jax.ShapeDtypeStruct((nrows, 3 * db), F32),
                   jax.ShapeDtypeStruct((nrows, db), BF16),
                   jax.ShapeDtypeStruct((t, db), BF16)],
        input_output_aliases=aliases,
        compiler_params=_cp("parallel"),
        name="conv",
    )(*args)


def _trig_outer(l, period):
    q = 1 << ((l.bit_length() - 1) // 2)
    n = lax.broadcasted_iota(jnp.int32, (1, l), 1)
    scale = 2.0 * math.pi / period

    def table(rows, step):
        r = lax.broadcasted_iota(jnp.int32, (rows, 1), 0) * step
        ang = ((r * n) % period).astype(F32) * scale
        return jnp.cos(ang), jnp.sin(ang)

    ac, as_ = table(l // q, q)
    bc, bs = table(q, 1)
    c = ac[:, None, :] * bc[None] - as_[:, None, :] * bs[None]
    s = as_[:, None, :] * bc[None] + ac[:, None, :] * bs[None]
    return c.reshape(l, l), s.reshape(l, l)


def _dft_mats(l):
    c, s = _trig_outer(l, 2 * l)
    k = lax.broadcasted_iota(jnp.int32, (l, l), 0)
    n = lax.broadcasted_iota(jnp.int32, (l, l), 1)
    alt_n = jnp.where(n % 2 == 0, 1.0, -1.0).astype(F32)
    alt_k = jnp.where(k % 2 == 0, 1.0, -1.0).astype(F32)
    return c.astype(BF16), jnp.where(k == 0, alt_n, -s).astype(BF16), jnp.where(n == 0, alt_k, -s).astype(BF16)


def _fnet_mats(l, gdim, groups):
    c, s = _trig_outer(l, l)
    cg, sg = _trig_outer(gdim, gdim)
    eye = jnp.eye(groups, dtype=F32)
    w_pq = jnp.concatenate([jnp.kron(eye, cg), jnp.kron(eye, sg)], axis=1).astype(BF16)
    return c.astype(BF16), (-s).astype(BF16), w_pq


def _fnet_pq_kernel(u_ref, w_ref, o_ref):
    o_ref[...] = jnp.dot(u_ref[...].astype(BF16), w_ref[...], preferred_element_type=F32).astype(BF16)


def _fnet_pq(u_all, w_pq, db):
    t = u_all.shape[0]
    return pl.pallas_call(
        _fnet_pq_kernel,
        grid=(t // TM,),
        in_specs=[pl.BlockSpec((TM, db), lambda i: (i, 0)),
                  pl.BlockSpec(w_pq.shape, lambda i: (0, 0))],
        out_specs=pl.BlockSpec((TM, 2 * db), lambda i: (i, 0)),
        out_shape=jax.ShapeDtypeStruct((t, 2 * db), BF16),
        compiler_params=_cp("parallel"),
        name="fnet_pq",
    )(u_all, w_pq)


def _fnet_dft_kernel(d0_ref, d1_ref, x_ref, *rest, db, scale):
    o_ref = rest[-1]
    acc = jnp.dot(d0_ref[...], x_ref[:, :db], preferred_element_type=F32)
    acc = acc + jnp.dot(d1_ref[...], x_ref[:, db:], preferred_element_type=F32)
    o_ref[...] = (acc * scale).astype(BF16)


def _mat_spec(tr, l):
    return pl.BlockSpec((tr, l), lambda i, b: (i, 0))


def _fnet_dft(d0, d1, pq, row0, nb, l, db, gdim, out_init, t_all):
    tr = min(TR, l)
    nrt = l // tr
    args = [d0, d1, pq]
    in_specs = [_mat_spec(tr, l), _mat_spec(tr, l),
                pl.BlockSpec((l, 2 * db), lambda i, b: (row0 // l + b, 0))]
    aliases = {}
    if out_init is not None:
        args.append(out_init)
        in_specs.append(pl.BlockSpec(memory_space=pl.ANY))
        aliases = {3: 0}
    return pl.pallas_call(
        functools.partial(_fnet_dft_kernel, db=db, scale=1.0 / math.sqrt(l * gdim)),
        grid=(nrt, nb),
        in_specs=in_specs,
        out_specs=pl.BlockSpec((tr, db), lambda i, b: (row0 // tr + b * nrt + i, 0)),
        out_shape=jax.ShapeDtypeStruct((t_all, db), BF16),
        input_output_aliases=aliases,
        compiler_params=_cp("parallel", "parallel"),
        name="fnet_dft",
    )(*args)


def _hy_fwd_kernel(f0_ref, f1_ref, x_ref, *rest, mult):
    o_ref = rest[-1]
    zre = jnp.dot(f0_ref[...], x_ref[...], preferred_element_type=F32)
    zim = jnp.dot(f1_ref[...], x_ref[...], preferred_element_type=F32)
    if mult:
        k_ref = rest[0]
        p, q, r = k_ref[0], k_ref[1], k_ref[2]
        o_ref[0] = (zre * p - zim * q).astype(o_ref.dtype)
        o_ref[1] = (zre * q + zim * r).astype(o_ref.dtype)
    else:
        o_ref[0] = zre
        o_ref[1] = zim


def _hy_fwd(fwd, x, nb, l, db, pqr):
    tr = min(TR, l)
    args = [fwd[0], fwd[1], x]
    in_specs = [_mat_spec(tr, l), _mat_spec(tr, l),
                pl.BlockSpec((l, db), lambda i, b: (b, 0))]
    if pqr is not None:
        args.append(pqr)
        in_specs.append(pl.BlockSpec((3, tr, db), lambda i, b: (0, i, 0)))
    return pl.pallas_call(
        functools.partial(_hy_fwd_kernel, mult=pqr is not None),
        grid=(l // tr, nb),
        in_specs=in_specs,
        out_specs=pl.BlockSpec((None, 2, tr, db), lambda i, b: (b, 0, i, 0)),
        out_shape=jax.ShapeDtypeStruct((nb, 2, l, db), F32 if pqr is None else BF16),
        compiler_params=_cp("parallel", "parallel"),
        name="hy_fwd",
    )(*args)


def _hy_inv_kernel(g0_ref, g1_ref, y_ref, zp_ref, gate_ref, bias_ref, *rest):
    o_ref = rest[-1]
    y = jnp.dot(g0_ref[...], y_ref[0], preferred_element_type=F32)
    y = y + jnp.dot(g1_ref[...], y_ref[1], preferred_element_type=F32)
    o_ref[...] = (gate_ref[...] * (y + bias_ref[...] * zp_ref[...].astype(F32))).astype(BF16)


def _hy_inv(inv, yhat, zprev, zcol, uc, gcol, bias, nb, l, db, out_row0, out_rows, out_init):
    tr = min(TR, l)
    nrt = l // tr
    args = [inv[0], inv[1], yhat, zprev, uc, bias]
    in_specs = [_mat_spec(tr, l), _mat_spec(tr, l),
                pl.BlockSpec((None, 2, l, db), lambda i, b: (b, 0, 0, 0)),
                pl.BlockSpec((tr, db), lambda i, b: (b * nrt + i, zcol)),
                pl.BlockSpec((tr, db), lambda i, b: (b * nrt + i, gcol)),
                pl.BlockSpec((1, db), lambda i, b: (0, 0))]
    aliases = {}
    if out_init is not None:
        args.append(out_init)
        in_specs.append(pl.BlockSpec(memory_space=pl.ANY))
        aliases = {6: 0}
    return pl.pallas_call(
        _hy_inv_kernel,
        grid=(nrt, nb),
        in_specs=in_specs,
        out_specs=pl.BlockSpec((tr, db), lambda i, b: (out_row0 // tr + b * nrt + i, 0)),
        out_shape=jax.ShapeDtypeStruct((out_rows, db), BF16),
        input_output_aliases=aliases,
        compiler_params=_cp("parallel", "parallel"),
        name="hy_inv",
    )(*args)


def _hyena_filters(length, hf_w1, hf_b1, hf_w2, hf_b2, hf_w3, hf_freq, db):
    emb = hf_w1.shape[0]
    nbands = (emb - 1) // 2
    t = jnp.linspace(0.0, 1.0, length, dtype=F32)[:, None]
    w = (2.0 * math.pi / length) * jnp.arange(length, dtype=F32)[:, None]
    bands = jnp.linspace(1e-4, nbands - 1, nbands, dtype=F32)[None, :]
    feats = jnp.concatenate([t, jnp.cos(bands * w), -jnp.sin(bands * w)], axis=-1)
    z = jnp.sin(hf_freq[0] * (jnp.dot(feats, hf_w1, precision=HIGHEST) + hf_b1))
    z = jnp.sin(hf_freq[1] * (jnp.dot(z, hf_w2, precision=HIGHEST) + hf_b2))
    filt = jnp.dot(z, hf_w3, precision=HIGHEST).reshape(length, -1, 2, db)
    min_decay = math.log(HY_DECAY_TARGET) / HY_SLOW_PCT
    max_decay = math.log(HY_DECAY_TARGET) / HY_FAST_PCT
    deltas = jnp.abs(jnp.linspace(min_decay, max_decay, db, dtype=F32))
    filt = filt * jnp.exp(-t * deltas)[:, None, None, :]
    return filt * lax.rsqrt(jnp.sum(filt * filt, axis=(0, 2), keepdims=True) + NORM_EPS)


def _filter_spectra(fwd, filt, l, db):
    n_ord = filt.shape[1]
    cols = jnp.transpose(filt, (1, 2, 0, 3))
    cols = cols.at[:, 1, 0, :].set(0.0)
    x = cols.reshape(n_ord * 2 * l, db).astype(BF16)
    spec = _hy_fwd(fwd, x, n_ord * 2, l, db, None).reshape(n_ord, 2, 2, l, db)
    hf, hb = spec[:, 0], spec[:, 1]
    k_re = hf[:, 0] + hb[:, 0]
    k_im = hf[:, 1] - hb[:, 1]
    k_nyq = hf[:, 1, 0] + hb[:, 1, 0]
    first = (jnp.arange(l) == 0)[None, :, None]
    scale = jnp.where(first, 1.0 / (2 * l), 2.0 / (2 * l)).astype(F32)
    p = k_re * scale
    q = jnp.where(first, 0.0, k_im) * scale
    r = jnp.where(first, k_nyq[:, None, :], k_re) * scale
    return jnp.stack([p, q, r], axis=1)


def _lru_kernel(*refs, rw, reverse, final):
    if final:
        (x_ref, w_ref, wa_ref, wx_ref, ba_ref, bx_ref, lam_ref, h0_ref, hf_ref, ug_ref,
         _, o_ref, st_ref, a_s, b_s, carry) = refs
    else:
        (x_ref, w_ref, wa_ref, wx_ref, ba_ref, bx_ref, lam_ref, h0_ref,
         o_ref, st_ref, a_s, b_s, carry) = refs
    tl = x_ref.shape[0]

    @pl.when(pl.program_id(1) == 0)
    def _():
        carry[...] = h0_ref[...]

    xc = _dwconv(x_ref[...], w_ref, w_ref.shape[0] // 2, rw)
    xb = xc.astype(BF16)
    r = _sigmoid(jnp.dot(xb, wa_ref[...], preferred_element_type=F32) + ba_ref[...])
    ig = _sigmoid(jnp.dot(xb, wx_ref[...], preferred_element_type=F32) + bx_ref[...])
    lam = lam_ref[...]
    softplus = jnp.maximum(-lam, 0.0) + jnp.log(1.0 + jnp.exp(-jnp.abs(lam)))
    log_a = -LRU_C * r * softplus
    a = jnp.exp(log_a)
    bv = jnp.sqrt(1.0 - jnp.exp(2.0 * log_a)) * (ig * xc)

    pos8 = lax.broadcasted_iota(jnp.int32, a.shape, 0) % 8
    for s in (1, 2, 4):
        if reverse:
            a_sh, b_sh, ok = pltpu.roll(a, tl - s, axis=0), pltpu.roll(bv, tl - s, axis=0), pos8 < 8 - s
        else:
            a_sh, b_sh, ok = pltpu.roll(a, s, axis=0), pltpu.roll(bv, s, axis=0), pos8 >= s
        bv = jnp.where(ok, a * b_sh + bv, bv)
        a = jnp.where(ok, a * a_sh, a)
    a_s[...] = a
    b_s[...] = bv

    ng = tl // 8

    def body(gi, c):
        g = ng - 1 - gi if reverse else gi
        sl = pl.ds(pl.multiple_of(g * 8, 8), 8)
        h8 = b_s[sl, :] + a_s[sl, :] * c
        b_s[sl, :] = h8
        return h8[0:1, :] if reverse else h8[7:8, :]

    c_out = lax.fori_loop(0, ng, body, carry[...])
    carry[...] = c_out
    st_ref[...] = c_out
    if final:
        ug = ug_ref[...]
        gelu = 0.5 * ug * (1.0 + jnp.tanh(math.sqrt(2.0 / math.pi) * (ug + 0.044715 * ug * ug * ug)))
        o_ref[...] = ((hf_ref[...] + b_s[...]) * gelu).astype(BF16)
    else:
        o_ref[...] = b_s[...]


def _lru_pass(u_all, w_lconv, wa, wx, ba, bx, lam, h0, row0, nb, l, tl, rw, db, reverse, hf=None, out_init=None,
              t_all=None):
    final = hf is not None
    nc = l // tl
    ob = row0 // tl
    cidx = (lambda c: nc - 1 - c) if reverse else (lambda c: c)
    full = lambda a: pl.BlockSpec(a.shape, lambda b, c: (0,) * a.ndim)
    ncolx, ncolg = 7, 8
    args = [u_all, w_lconv, wa, wx, ba, bx, lam, h0]
    in_specs = [pl.BlockSpec((tl, db), lambda b, c: (ob + b * nc + cidx(c), ncolx)),
                full(w_lconv), full(wa), full(wx), full(ba), full(bx), full(lam),
                pl.BlockSpec((None, 1, db), lambda b, c: (b, 0, 0))]
    aliases = {}
    if final:
        args += [hf, u_all]
        in_specs += [pl.BlockSpec((tl, db), lambda b, c: (b * nc + cidx(c), 0)),
                     pl.BlockSpec((tl, db), lambda b, c: (ob + b * nc + cidx(c), ncolg))]
        if out_init is None:
            out_init = jnp.zeros((8, 128), BF16)
        else:
            aliases = {10: 0}
        args.append(out_init)
        in_specs.append(pl.BlockSpec(memory_space=pl.ANY))
        out_spec = pl.BlockSpec((tl, db), lambda b, c: (ob + b * nc + cidx(c), 0))
        out_shape = jax.ShapeDtypeStruct((t_all, db), BF16)
    else:
        out_spec = pl.BlockSpec((tl, db), lambda b, c: (b * nc + cidx(c), 0))
        out_shape = jax.ShapeDtypeStruct((nb * l, db), F32)
    return pl.pallas_call(
        functools.partial(_lru_kernel, rw=rw, reverse=reverse, final=final),
        grid=(nb, nc),
        in_specs=in_specs,
        out_specs=[out_spec, pl.BlockSpec((None, 1, db), lambda b, c: (b, 0, 0))],
        out_shape=[out_shape, jax.ShapeDtypeStruct((nb, 1, db), F32)],
        scratch_shapes=[pltpu.VMEM((tl, db), F32), pltpu.VMEM((tl, db), F32), pltpu.VMEM((1, db), F32)],
        input_output_aliases=aliases,
        compiler_params=_cp("arbitrary", "arbitrary"),
        name="lru_bwd" if reverse else "lru_fwd",
    )(*args)


def _block_diag(w):
    h, hd, _ = w.shape
    eye = jnp.eye(h, dtype=w.dtype)
    return (eye[:, None, :, None] * w[:, :, None, :]).reshape(h * hd, h * hd).astype(BF16)


def _merge_kernel(x_ref, sh_ref, sc_ref, g_ref, b0_ref, b1_ref, b2_ref, b3_ref, wm_ref, wb_ref, wo_ref, ga_ref,
                  o_ref, hbf):
    j = pl.program_id(1)

    @pl.when(j == 0)
    def _():
        _norm_mod_store(hbf, x_ref, g_ref, sh_ref, sc_ref)

    h = hbf[...]
    m = None
    for k, br in enumerate((b0_ref, b1_ref, b2_ref, b3_ref)):
        g = jnp.dot(h, wm_ref[k], preferred_element_type=F32)
        p = jnp.dot(br[...], wb_ref[k], preferred_element_type=F32)
        term = _sigmoid(g) * p
        m = term if m is None else m + term
    part = jnp.dot(m.astype(BF16), wo_ref[...], preferred_element_type=F32)

    @pl.when(j == 0)
    def _():
        o_ref[...] = part

    @pl.when(j > 0)
    def _():
        o_ref[...] += part

    @pl.when(j == pl.num_programs(1) - 1)
    def _():
        o_ref[...] = x_ref[...] + ga_ref[...] * o_ref[...]


def _merge(branches, wm_bf, wb_bf, wo_bf, x_all, mod3, g, ntiles, nlt, tpb, nb):
    t, d = x_all.shape
    db = wb_bf.shape[1]
    tn = min(TN_MERGE, d)
    row = lambda w: pl.BlockSpec((TM, w), lambda i, j: (i, 0))
    return pl.pallas_call(
        _merge_kernel,
        grid=(ntiles, d // tn),
        in_specs=[row(d), _mod_spec(d, 0, nlt, tpb, nb), _mod_spec(d, 1, nlt, tpb, nb),
                  pl.BlockSpec((1, d), lambda i, j: (0, 0)),
                  row(db), row(db), row(db), row(db),
                  pl.BlockSpec((4, d, tn), lambda i, j: (0, 0, j)),
                  pl.BlockSpec((4, db, tn), lambda i, j: (0, 0, j)),
                  pl.BlockSpec((tn, d), lambda i, j: (j, 0)),
                  _mod_spec(d, 2, nlt, tpb, nb)],
        out_specs=row(d),
        out_shape=jax.ShapeDtypeStruct((ntiles * TM, d), F32),
        scratch_shapes=[pltpu.VMEM((TM, d), BF16)],
        compiler_params=_cp("parallel", "arbitrary"),
        name="merge",
    )(x_all, mod3, mod3, g.reshape(1, d), *branches, wm_bf, wb_bf, wo_bf, mod3)


def _router_kernel(x_ref, sh_ref, sc_ref, g_ref, wr_ref, br_ref, h_ref, r_ref, c_ref, *, ngroups, epg):
    h = _norm_mod(x_ref[...], g_ref[...], sh_ref[...], sc_ref[...])
    h_ref[...] = h
    logits = jnp.dot(h, wr_ref[...], preferred_element_type=F32, precision=HIGHEST) + br_ref[...]
    lane = lax.broadcasted_iota(jnp.int32, logits.shape, 1)
    lane_f = lane.astype(F32)
    neg = jnp.float32(-1e30)
    big = jnp.float32(1e6)
    is_g = lane < ngroups
    lg = jnp.where(is_g, logits, neg)
    mx = jnp.max(lg, axis=-1, keepdims=True)
    g_top = jnp.min(jnp.where(lg == mx, lane_f, big), axis=-1, keepdims=True)
    den = jnp.sum(jnp.where(is_g, jnp.exp(lg - mx), 0.0), axis=-1, keepdims=True)
    p_top = 1.0 / den
    el = lane_f - ngroups
    in_grp = (el >= g_top * epg) & (el < (g_top + 1.0) * epg)
    le = jnp.where(in_grp, logits, neg)
    v1 = jnp.max(le, axis=-1, keepdims=True)
    e1 = jnp.min(jnp.where(le == v1, el, big), axis=-1, keepdims=True)
    le2 = jnp.where(el == e1, neg, le)
    v2 = jnp.max(le2, axis=-1, keepdims=True)
    e2 = jnp.min(jnp.where(le2 == v2, el, big), axis=-1, keepdims=True)
    dlt = jnp.exp(v2 - v1)
    w1 = p_top / (1.0 + dlt)
    w2 = p_top * dlt / (1.0 + dlt)
    tm = logits.shape[0]
    pick1 = el == e1
    pick2 = el == e2
    onehot = jnp.where(pick1 | pick2, 1.0, 0.0)
    row = lax.broadcasted_iota(jnp.int32, (tm, tm), 0)
    col = lax.broadcasted_iota(jnp.int32, (tm, tm), 1)
    earlier = jnp.where(col < row, 1.0, 0.0).astype(BF16)
    before = jnp.dot(earlier, onehot.astype(BF16), preferred_element_type=F32)
    rank1 = jnp.sum(jnp.where(pick1, before, 0.0), axis=-1, keepdims=True)
    rank2 = jnp.sum(jnp.where(pick2, before, 0.0), axis=-1, keepdims=True)
    out = jnp.where(lane == 0, e1, 0.0)
    out = jnp.where(lane == 1, e2, out)
    out = jnp.where(lane == 2, w1, out)
    out = jnp.where(lane == 3, w2, out)
    out = jnp.where(lane == 4, rank1, out)
    out = jnp.where(lane == 5, rank2, out)
    r_ref[...] = out
    c_ref[...] = jnp.broadcast_to(jnp.sum(onehot, axis=0, keepdims=True), c_ref.shape)


def _router(x_all, mod3, g, w_router, b_router, ntiles, nlt, tpb, nb, ngroups, epg):
    t, d = x_all.shape
    return pl.pallas_call(
        functools.partial(_router_kernel, ngroups=ngroups, epg=epg),
        grid=(ntiles,),
        in_specs=[pl.BlockSpec((TM, d), lambda i: (i, 0)),
                  _mod_spec(d, 3, nlt, tpb, nb), _mod_spec(d, 4, nlt, tpb, nb),
                  pl.BlockSpec((1, d), lambda i: (0, 0)),
                  pl.BlockSpec((d, 128), lambda i: (0, 0)),
                  pl.BlockSpec((1, 128), lambda i: (0, 0))],
        out_specs=[pl.BlockSpec((TM, d), lambda i: (i, 0)), pl.BlockSpec((TM, 128), lambda i: (i, 0)),
                   pl.BlockSpec((None, 8, 128), lambda i: (i, 0, 0))],
        out_shape=[jax.ShapeDtypeStruct((ntiles * TM, d), F32), jax.ShapeDtypeStruct((ntiles * TM, 128), F32),
                   jax.ShapeDtypeStruct((ntiles, 8, 128), F32)],
        compiler_params=_cp("parallel"),
        name="router",
    )(x_all, mod3, mod3, g.reshape(1, d), w_router, b_router)


def _route_tables(slab, tile_counts, ngroups, n_experts, tm):
    ntiles = tile_counts.shape[0]
    e = slab[:, 0:TOP_K].astype(jnp.int32).reshape(ntiles, TM, TOP_K)
    rank = slab[:, 4:4 + TOP_K].astype(jnp.int32).reshape(ntiles, TM, TOP_K)
    counts_t = tile_counts[:, 0, ngroups:ngroups + n_experts].astype(jnp.int32)
    base = jnp.cumsum(counts_t, axis=0) - counts_t
    counts = jnp.sum(counts_t, axis=0)
    tiles_per_e = (counts + tm - 1) // tm
    tile_end = jnp.cumsum(tiles_per_e)
    starts = (tile_end - tiles_per_e) * tm
    off = starts[None, :] + base
    sel = e[..., None] == jnp.arange(n_experts, dtype=jnp.int32)
    slot = jnp.sum(jnp.where(sel, off[:, None, None, :], 0), axis=-1) + rank
    n_slot_tiles = (ntiles * TM * TOP_K) // tm + n_experts
    n_valid = tile_end[-1]
    tile_ids = jnp.minimum(jnp.arange(n_slot_tiles, dtype=jnp.int32), n_valid - 1)
    tile_expert = jnp.sum((tile_end[None, :] <= tile_ids[:, None]).astype(jnp.int32), axis=1)
    return slot, tile_expert, n_valid.reshape(1).astype(jnp.int32), n_slot_tiles


ROW_DMA_UNROLL = 8


def _stage_indices(idx_hbm_row, idx_smem, sem_i):
    cp = pltpu.make_async_copy(idx_hbm_row, idx_smem, sem_i)
    cp.start()
    cp.wait()


def _dispatch_kernel(p_hbm, h_hbm, xs_in, xs_hbm, idx, sem_i, sem_x):
    del xs_in
    i = pl.program_id(0)
    tm = idx.shape[0] // TOP_K

    def wait_tile():
        pltpu.make_async_copy(h_hbm.at[pl.ds(0, TOP_K * tm), :], xs_hbm.at[pl.ds(0, TOP_K * tm), :], sem_x).wait()

    @pl.when(i > 0)
    def _():
        wait_tile()

    _stage_indices(p_hbm.at[i], idx, sem_i)

    def issue(r, c):
        src = h_hbm.at[pl.ds(i * tm + r, 1), :]
        for k in range(TOP_K):
            pltpu.make_async_copy(src, xs_hbm.at[pl.ds(idx[k * tm + r], 1), :], sem_x).start()
        return c

    lax.fori_loop(0, tm, issue, 0, unroll=ROW_DMA_UNROLL)

    @pl.when(i == pl.num_programs(0) - 1)
    def _():
        wait_tile()


def _dispatch(h_all, slot, n_slot_rows):
    ntiles = slot.shape[0]
    d = h_all.shape[1]
    p = jnp.transpose(slot, (0, 2, 1)).reshape(ntiles, TOP_K * TM)
    xs0 = jnp.zeros((n_slot_rows, d), F32)
    return pl.pallas_call(
        _dispatch_kernel,
        grid=(ntiles,),
        in_specs=[pl.BlockSpec(memory_space=pl.ANY)] * 3,
        out_specs=pl.BlockSpec(memory_space=pl.ANY),
        out_shape=jax.ShapeDtypeStruct((n_slot_rows, d), F32),
        scratch_shapes=[pltpu.SMEM((TOP_K * TM,), jnp.int32),
                        pltpu.SemaphoreType.DMA(()), pltpu.SemaphoreType.DMA(())],
        input_output_aliases={2: 0},
        compiler_params=_cp("arbitrary"),
        name="dispatch",
    )(p, h_all, xs0)


def _moe_kernel(te_ref, nv_ref, x_ref, wg_ref, wu_ref, wd_ref, o_ref):
    del te_ref
    valid = pl.program_id(0) < nv_ref[0]

    @pl.when(valid)
    def _():
        x = x_ref[...].astype(BF16)
        g = jnp.dot(x, wg_ref[...], preferred_element_type=F32)
        u = jnp.dot(x, wu_ref[...], preferred_element_type=F32)
        hid = (g * _sigmoid(g) * u).astype(BF16)
        o_ref[...] = jnp.dot(hid, wd_ref[...], preferred_element_type=F32)

    @pl.when(jnp.logical_not(valid))
    def _():
        o_ref[...] = jnp.zeros_like(o_ref)


def _moe_experts(xs, tile_expert, n_valid, w_gate, w_up, w_down, tm):
    n_rows, d = xs.shape
    f = w_gate.shape[2]
    xi = lambda i, te, nv: (jnp.minimum(i, nv[0] - 1), 0)
    grid_spec = pltpu.PrefetchScalarGridSpec(
        num_scalar_prefetch=2,
        grid=(n_rows // tm,),
        in_specs=[pl.BlockSpec((tm, d), xi),
                  pl.BlockSpec((None, d, f), lambda i, te, nv: (te[i], 0, 0)),
                  pl.BlockSpec((None, d, f), lambda i, te, nv: (te[i], 0, 0)),
                  pl.BlockSpec((None, f, d), lambda i, te, nv: (te[i], 0, 0))],
        out_specs=pl.BlockSpec((tm, d), lambda i, te, nv: (i, 0)),
    )
    return pl.pallas_call(
        _moe_kernel,
        grid_spec=grid_spec,
        out_shape=jax.ShapeDtypeStruct((n_rows, d), F32),
        compiler_params=_cp("arbitrary"),
        name="moe_experts",
    )(tile_expert, n_valid, xs, w_gate, w_up, w_down)


def _combine_kernel(p_hbm, ys_hbm, x_ref, slab_ref, ga_ref, g_ref, o_ref, idx, buf, sem_i, sem_x, *, final):
    i = pl.program_id(0)
    n = pl.num_programs(0)
    rows = idx.shape[0]
    tm = rows // TOP_K

    def fetch(tile, s):
        _stage_indices(p_hbm.at[tile], idx, sem_i)

        def issue(r, c):
            pltpu.make_async_copy(ys_hbm.at[pl.ds(idx[r], 1), :], buf.at[s, pl.ds(r, 1), :], sem_x.at[s]).start()
            return c

        lax.fori_loop(0, rows, issue, 0, unroll=ROW_DMA_UNROLL)

    @pl.when(i == 0)
    def _():
        fetch(0, 0)

    @pl.when(i + 1 < n)
    def _():
        fetch(i + 1, (i + 1) % 2)

    s = i % 2
    pltpu.make_async_copy(ys_hbm.at[pl.ds(0, rows), :], buf.at[s], sem_x.at[s]).wait()
    slab = slab_ref[...]
    y = slab[:, TOP_K:TOP_K + 1] * buf[s, 0:tm, :]
    for k in range(1, TOP_K):
        y = y + slab[:, TOP_K + k:TOP_K + k + 1] * buf[s, k * tm:(k + 1) * tm, :]
    x = x_ref[...] + ga_ref[...] * y
    if final:
        x = x * lax.rsqrt(jnp.mean(x * x, axis=-1, keepdims=True) + NORM_EPS) * g_ref[...]
    o_ref[...] = x


def _combine(ys, slot, slab, x_all, mod3, g_final, ntiles, nlt, tpb, nb, final):
    d = x_all.shape[1]
    tm = TM_COMB
    f = TM // tm
    n = ntiles * f
    p = jnp.transpose(slot.reshape(n, tm, TOP_K), (0, 2, 1)).reshape(n, TOP_K * tm)
    return pl.pallas_call(
        functools.partial(_combine_kernel, final=final),
        grid=(n,),
        in_specs=[pl.BlockSpec(memory_space=pl.ANY), pl.BlockSpec(memory_space=pl.ANY),
                  pl.BlockSpec((tm, d), lambda i: (i, 0)),
                  pl.BlockSpec((tm, 128), lambda i: (i, 0)),
                  pl.BlockSpec((None, 1, d), lambda i: (jnp.where(i < nlt * f, i // (tpb * f), nb), 0, 5)),
                  pl.BlockSpec((1, d), lambda i: (0, 0))],
        out_specs=pl.BlockSpec((tm, d), lambda i: (i, 0)),
        out_shape=jax.ShapeDtypeStruct((n * tm, d), F32),
        scratch_shapes=[pltpu.SMEM((TOP_K * tm,), jnp.int32),
                        pltpu.VMEM((2, TOP_K * tm, d), F32),
                        pltpu.SemaphoreType.DMA(()), pltpu.SemaphoreType.DMA((2,))],
        compiler_params=_cp("arbitrary"),
        name="combine",
    )(p, ys, x_all, slab, mod3, g_final.reshape(1, d))


def kernel(x, c, ctx, c_ctx, w_mod, b_mod, g_norm1, g_norm2, g_final, w_in, w_merge, w_branch, w_out, w_hconv,
           hy_bias, hf_w1, hf_b1, hf_w2, hf_b2, hf_w3, hf_freq, w_sconv, w_lconv, lru_wa, lru_ba, lru_wx, lru_bx,
           lru_lambda, w_rg, b_rg, w_re, b_re, w_e_gate, w_e_up, w_e_down):
    nb, seq, d = x.shape
    cl = ctx.shape[1]
    depth = w_in.shape[0]
    db = w_branch.shape[2]
    gdim = db // FNET_GROUPS
    ngroups = w_rg.shape[2]
    epg = w_re.shape[3]
    n_experts = ngroups * epg
    t_lat, t_ctx = nb * seq, nb * cl
    t_all = t_lat + t_ctx
    nlt, tpb = t_lat // TM, seq // TM
    nat = t_all // TM
    tl_lat, tl_ctx = min(TL, seq), min(TL, cl)

    x_all = jnp.concatenate([x.reshape(t_lat, d), ctx.reshape(t_ctx, d)], axis=0)
    nrows = -(-(nb + 1) // 8) * 8
    cvec = jnp.zeros((nrows, d), F32).at[:nb].set(c).at[nb].set(c_ctx)
    mods = _adaln(cvec, w_mod, b_mod)

    c_lat, sf_lat, si_lat = _dft_mats(seq)
    c_ctx, sf_ctx, si_ctx = _dft_mats(cl)
    fwd_lat, inv_lat = (c_lat, sf_lat), (c_lat, si_lat)
    fwd_ctx, inv_ctx = (c_ctx, sf_ctx), (c_ctx, si_ctx)
    fc_lat, fs_lat, w_pq = _fnet_mats(seq, gdim, FNET_GROUPS)
    fc_ctx, fs_ctx, _ = _fnet_mats(cl, gdim, FNET_GROUPS)

    out = None
    for l in range(depth):
        last = l == depth - 1
        mod3 = mods[l].reshape(nrows, 1, 6 * d)
        u_all = _inproj(x_all, mod3, g_norm1[l], w_in[l].astype(BF16), nlt, tpb, nb)

        wa = [_block_diag(lru_wa[l, dd]) for dd in range(2)]
        wx = [_block_diag(lru_wx[l, dd]) for dd in range(2)]
        ba = [lru_ba[l, dd].reshape(1, db) for dd in range(2)]
        bx = [lru_bx[l, dd].reshape(1, db) for dd in range(2)]
        lam = [lru_lambda[l, dd].reshape(1, db) for dd in range(2)]
        zeros_st = jnp.zeros((nb, 1, db), F32)
        rows_out = t_lat if last else t_all

        def lru(row0, length, tl, rw, h0f, h0b, combine, out_init):
            hf, stf = _lru_pass(u_all, w_lconv[l], wa[0], wx[0], ba[0], bx[0], lam[0], h0f,
                                row0, nb, length, tl, rw, db, False)
            if combine:
                y, stb = _lru_pass(u_all, w_lconv[l], wa[1], wx[1], ba[1], bx[1], lam[1], h0b,
                                   row0, nb, length, tl, rw, db, True, hf=hf, out_init=out_init, t_all=rows_out)
            else:
                y, stb = _lru_pass(u_all, w_lconv[l], wa[1], wx[1], ba[1], bx[1], lam[1], h0b,
                                   row0, nb, length, tl, rw, db, True)
            return y, stf, stb

        def hyena(row0, length, tl, rw, fwd, inv, uc, v_bf, out_init):
            filt = _hyena_filters(length, hf_w1[l], hf_b1[l], hf_w2[l], hf_b2[l], hf_w3[l], hf_freq[l], db)
            pqr = _filter_spectra(fwd, filt, length, db)
            yhat = _hy_fwd(fwd, v_bf, nb, length, db, pqr[0])
            z2 = _hy_inv(inv, yhat, uc, 0, uc, 1, hy_bias[l, 0].reshape(1, db), nb, length, db,
                         0, nb * length, None)
            yhat = _hy_fwd(fwd, z2, nb, length, db, pqr[1])
            return _hy_inv(inv, yhat, z2, 0, uc, 2, hy_bias[l, 1].reshape(1, db), nb, length, db,
                           row0, rows_out, out_init)

        zbuf = lambda: None if last else jnp.zeros((rows_out, db), BF16)
        if last:
            _, st_f, st_b = lru(t_lat, cl, tl_ctx, cl, zeros_st, zeros_st, False, None)
        else:
            y_lru_c, st_f, st_b = lru(t_lat, cl, tl_ctx, cl, zeros_st, zeros_st, True, zbuf())
        y_lru, _, _ = lru(0, seq, tl_lat, GRID_W, st_f, st_b, True, None if last else y_lru_c)

        pq = _fnet_pq(u_all, w_pq, db)
        uc_lat, v_lat, y_sc = _conv_stage(u_all, w_hconv[l], w_sconv[l], 0, t_lat, tl_lat, GRID_W, db, zbuf(),
                                          rows_out)
        y_fn = _fnet_dft(fc_lat, fs_lat, pq, 0, nb, seq, db, gdim, zbuf(), rows_out)
        if not last:
            uc_ctx, v_ctx, y_sc = _conv_stage(u_all, w_hconv[l], w_sconv[l], t_lat, t_ctx, tl_ctx, cl, db, y_sc,
                                              rows_out)
            y_fn = _fnet_dft(fc_ctx, fs_ctx, pq, t_lat, nb, cl, db, gdim, y_fn, rows_out)
            y_hy = hyena(t_lat, cl, tl_ctx, cl, fwd_ctx, inv_ctx, uc_ctx, v_ctx, zbuf())
        else:
            y_hy = None
        y_hy = hyena(0, seq, tl_lat, GRID_W, fwd_lat, inv_lat, uc_lat, v_lat, y_hy)

        ntiles = nlt if last else nat
        x_mid = _merge((y_fn, y_hy, y_sc, y_lru), w_merge[l].astype(BF16), w_branch[l].astype(BF16),
                       w_out[l].astype(BF16), x_all, mod3, g_norm1[l], ntiles, nlt, tpb, nb)

        w_router = jnp.zeros((d, 128), F32).at[:, :ngroups].set(w_rg[l])
        w_router = w_router.at[:, ngroups:ngroups + n_experts].set(
            jnp.transpose(w_re[l], (1, 0, 2)).reshape(d, n_experts))
        b_router = jnp.zeros((1, 128), F32).at[0, :ngroups].set(b_rg[l])
        b_router = b_router.at[0, ngroups:ngroups + n_experts].set(b_re[l].reshape(-1))
        h2, slab, tile_counts = _router(x_mid, mod3, g_norm2[l], w_router, b_router, ntiles, nlt, tpb, nb,
                                        ngroups, epg)
        slot, tile_expert, n_valid, n_slot_tiles = _route_tables(slab, tile_counts, ngroups, n_experts, TM_MOE)
        xs = _dispatch(h2, slot, n_slot_tiles * TM_MOE)
        ys = _moe_experts(xs, tile_expert, n_valid, w_e_gate[l].astype(BF16), w_e_up[l].astype(BF16),
                          w_e_down[l].astype(BF16), TM_MOE)
        x_all = _combine(ys, slot, slab, x_mid, mod3, g_final, ntiles, nlt, tpb, nb, last)
        if last:
            out = x_all.reshape(nb, seq, d)
    return out
```

```python
import functools
import math

import jax
import jax.numpy as jnp
from jax import lax
from jax.experimental import pallas as pl
from jax.experimental.pallas import tpu as pltpu

F32 = jnp.float32
BF16 = jnp.bfloat16
HIGHEST = lax.Precision.HIGHEST

NORM_EPS = 1e-6
GRID_W = 64
FNET_GROUPS = 4
LRU_C = 8.0
HY_DECAY_TARGET = 1e-2
HY_FAST_PCT = 0.3
HY_SLOW_PCT = 1.5
TOP_K = 2

TM = 512
TL = 512
TR = 512
TM_MOE = 512
TM_COMB = 256
TN_IN = 1536
TN_MERGE = 512
TF_MOE = 512
VMEM_LIMIT = 56 * 1024 * 1024


def _cp(*sem):
    return pltpu.CompilerParams(dimension_semantics=sem, vmem_limit_bytes=VMEM_LIMIT)


def _sigmoid(x):
    return 1.0 / (1.0 + jnp.exp(-x))


def _adaln_kernel(c_ref, w_ref, b_ref, o_ref):
    c = c_ref[...]
    s = c * _sigmoid(c)
    o_ref[...] = jnp.dot(s, w_ref[...], preferred_element_type=F32, precision=HIGHEST) + b_ref[...]


def _adaln(cvec, w_mod, b_mod):
    nl, d, n6 = w_mod.shape
    r = cvec.shape[0]
    tn = min(1024, n6)
    return pl.pallas_call(
        _adaln_kernel,
        grid=(nl, n6 // tn),
        in_specs=[pl.BlockSpec((r, d), lambda l, j: (0, 0)),
                  pl.BlockSpec((None, d, tn), lambda l, j: (l, 0, j)),
                  pl.BlockSpec((None, 1, tn), lambda l, j: (l, 0, j))],
        out_specs=pl.BlockSpec((None, r, tn), lambda l, j: (l, 0, j)),
        out_shape=jax.ShapeDtypeStruct((nl, r, n6), F32),
        compiler_params=_cp("parallel", "parallel"),
        name="adaln",
    )(cvec, w_mod, b_mod.reshape(nl, 1, n6))


def _norm_mod(x, g, shift, scale):
    y = x * lax.rsqrt(jnp.mean(x * x, axis=-1, keepdims=True) + NORM_EPS) * g
    return y * (1.0 + scale) + shift


def _norm_mod_store(dst, x_ref, g_ref, sh_ref, sc_ref, rows=128):
    rows = min(rows, dst.shape[0])

    def body(c, carry):
        sl = pl.ds(pl.multiple_of(c * rows, rows), rows)
        dst[sl, :] = _norm_mod(x_ref[sl, :], g_ref[...], sh_ref[...], sc_ref[...]).astype(dst.dtype)
        return carry

    lax.fori_loop(0, dst.shape[0] // rows, body, 0)


def _mod_spec(d, chunk, nlt, tpb, nb):
    return pl.BlockSpec((None, 1, d), lambda i, *_: (jnp.where(i < nlt, i // tpb, nb), 0, chunk))


def _inproj_kernel(x_ref, sh_ref, sc_ref, g_ref, w_ref, u_ref, hbf, *, tn):
    j = pl.program_id(1)

    @pl.when(j == 0)
    def _():
        _norm_mod_store(hbf, x_ref, g_ref, sh_ref, sc_ref)

    for jc in range(w_ref.shape[1] // tn):
        @pl.when(j == jc)
        def _(jc=jc):
            u_ref[...] = jnp.dot(hbf[...], w_ref[:, jc * tn:(jc + 1) * tn], preferred_element_type=F32)


def _inproj(x_all, mod3, g, w_bf, nlt, tpb, nb):
    t, d = x_all.shape
    n = w_bf.shape[1]
    tn = TN_IN if n % TN_IN == 0 else n
    return pl.pallas_call(
        functools.partial(_inproj_kernel, tn=tn),
        grid=(t // TM, n // tn),
        in_specs=[pl.BlockSpec((TM, d), lambda i, j: (i, 0)),
                  _mod_spec(d, 0, nlt, tpb, nb), _mod_spec(d, 1, nlt, tpb, nb),
                  pl.BlockSpec((1, d), lambda i, j: (0, 0)),
                  pl.BlockSpec((d, n), lambda i, j: (0, 0))],
        out_specs=pl.BlockSpec((TM, tn), lambda i, j: (i, j)),
        out_shape=jax.ShapeDtypeStruct((t, n), F32),
        scratch_shapes=[pltpu.VMEM((TM, d), BF16)],
        compiler_params=_cp("parallel", "arbitrary"),
        name="inproj",
    )(x_all, mod3, mod3, g.reshape(1, d), w_bf)


def _dwconv(x, w_ref, left, rw):
    tl = x.shape[0]
    pos = lax.broadcasted_iota(jnp.int32, x.shape, 0) % rw
    acc = None
    for k in range(w_ref.shape[0]):
        off = k - left
        if off == 0:
            term = x
        else:
            shifted = pltpu.roll(x, (-off) % tl, axis=0)
            ok = (pos + off >= 0) & (pos + off < rw)
            term = jnp.where(ok, shifted, 0.0)
        term = term * w_ref[k:k + 1, :]
        acc = term if acc is None else acc + term
    return acc


def _conv_kernel(hv_ref, h1_ref, h2_ref, sb_ref, sc_ref, sh_ref, wh_ref, ws_ref,
                 uc_ref, vbf_ref, ys_ref, *, rw, db):
    for k, ref in enumerate((hv_ref, h1_ref, h2_ref)):
        y = _dwconv(ref[...], wh_ref.at[:, k * db:(k + 1) * db], 1, rw)
        uc_ref[:, k * db:(k + 1) * db] = y
        if k == 0:
            vbf_ref[...] = y.astype(BF16)
    ys_ref[...] = (sb_ref[...] * _dwconv(sc_ref[...] * sh_ref[...], ws_ref, 1, rw)).astype(BF16)


def _conv_stage(u_all, w_hconv, w_sconv, row0, nrows, tl, rw, db, ys_init, out_rows):
    t = out_rows
    ob = row0 // tl
    col = lambda c: pl.BlockSpec((tl, db), lambda i: (ob + i, c))
    aliases = {} if ys_init is None else {8: 2}
    args = [u_all] * 6 + [w_hconv, w_sconv]
    in_specs = [col(1), col(2), col(3), col(4), col(5), col(6),
                pl.BlockSpec(w_hconv.shape, lambda i: (0, 0)),
                pl.BlockSpec(w_sconv.shape, lambda i: (0, 0))]
    if ys_init is not None:
        args.append(ys_init)
        in_specs.append(pl.BlockSpec(memory_space=pl.ANY))
    kern = functools.partial(_conv_kernel, rw=rw, db=db)
    if ys_init is not None:
        kern = lambda *r, _k=kern: _k(*r[:8], *r[9:])
    return pl.pallas_call(
        kern,
        grid=(nrows // tl,),
        in_specs=in_specs,
        out_specs=[pl.BlockSpec((tl, 3 * db), lambda i: (i, 0)),
                   pl.BlockSpec((tl, db), lambda i: (i, 0)),
                   pl.BlockSpec((tl, db), lambda i: (ob + i, 0))],
        out_shape=[jax.ShapeDtypeStruct((nrows, 3 * db), F32),
                   jax.ShapeDtypeStruct((nrows, db), BF16),
                   jax.ShapeDtypeStruct((t, db), BF16)],
        input_output_aliases=aliases,
        compiler_params=_cp("parallel"),
        name="conv",
    )(*args)


def _trig_outer(l, period):
    q = 1 << ((l.bit_length() - 1) // 2)
    n = lax.broadcasted_iota(jnp.int32, (1, l), 1)
    scale = 2.0 * math.pi / period

    def table(rows, step):
        r = lax.broadcasted_iota(jnp.int32, (rows, 1), 0) * step
        ang = ((r * n) % period).astype(F32) * scale
        return jnp.cos(ang), jnp.sin(ang)

    ac, as_ = table(l // q, q)
    bc, bs = table(q, 1)
    c = ac[:, None, :] * bc[None] - as_[:, None, :] * bs[None]
    s = as_[:, None, :] * bc[None] + ac[:, None, :] * bs[None]
    return c.reshape(l, l), s.reshape(l, l)


def _dft_mats(l):
    c, s = _trig_outer(l, 2 * l)
    k = lax.broadcasted_iota(jnp.int32, (l, l), 0)
    n = lax.broadcasted_iota(jnp.int32, (l, l), 1)
    alt_n = jnp.where(n % 2 == 0, 1.0, -1.0).astype(F32)
    alt_k = jnp.where(k % 2 == 0, 1.0, -1.0).astype(F32)
    return c.astype(BF16), jnp.where(k == 0, alt_n, -s).astype(BF16), jnp.where(n == 0, alt_k, -s).astype(BF16)


def _fnet_mats(l, gdim, groups):
    c, s = _trig_outer(l, l)
    cg, sg = _trig_outer(gdim, gdim)
    eye = jnp.eye(groups, dtype=F32)
    w_pq = jnp.concatenate([jnp.kron(eye, cg), jnp.kron(eye, sg)], axis=1).astype(BF16)
    return c.astype(BF16), (-s).astype(BF16), w_pq


def _fnet_pq_kernel(u_ref, w_ref, o_ref):
    o_ref[...] = jnp.dot(u_ref[...].astype(BF16), w_ref[...], preferred_element_type=F32).astype(BF16)


def _fnet_pq(u_all, w_pq, db):
    t = u_all.shape[0]
    return pl.pallas_call(
        _fnet_pq_kernel,
        grid=(t // TM,),
        in_specs=[pl.BlockSpec((TM, db), lambda i: (i, 0)),
                  pl.BlockSpec(w_pq.shape, lambda i: (0, 0))],
        out_specs=pl.BlockSpec((TM, 2 * db), lambda i: (i, 0)),
        out_shape=jax.ShapeDtypeStruct((t, 2 * db), BF16),
        compiler_params=_cp("parallel"),
        name="fnet_pq",
    )(u_all, w_pq)


def _fnet_dft_kernel(d0_ref, d1_ref, x_ref, *rest, db, scale):
    o_ref = rest[-1]
    acc = jnp.dot(d0_ref[...], x_ref[:, :db], preferred_element_type=F32)
    acc = acc + jnp.dot(d1_ref[...], x_ref[:, db:], preferred_element_type=F32)
    o_ref[...] = (acc * scale).astype(BF16)


def _mat_spec(tr, l):
    return pl.BlockSpec((tr, l), lambda i, b: (i, 0))


def _fnet_dft(d0, d1, pq, row0, nb, l, db, gdim, out_init, t_all):
    tr = min(TR, l)
    nrt = l // tr
    args = [d0, d1, pq]
    in_specs = [_mat_spec(tr, l), _mat_spec(tr, l),
                pl.BlockSpec((l, 2 * db), lambda i, b: (row0 // l + b, 0))]
    aliases = {}
    if out_init is not None:
        args.append(out_init)
        in_specs.append(pl.BlockSpec(memory_space=pl.ANY))
        aliases = {3: 0}
    return pl.pallas_call(
        functools.partial(_fnet_dft_kernel, db=db, scale=1.0 / math.sqrt(l * gdim)),
        grid=(nrt, nb),
        in_specs=in_specs,
        out_specs=pl.BlockSpec((tr, db), lambda i, b: (row0 // tr + b * nrt + i, 0)),
        out_shape=jax.ShapeDtypeStruct((t_all, db), BF16),
        input_output_aliases=aliases,
        compiler_params=_cp("parallel", "parallel"),
        name="fnet_dft",
    )(*args)


def _hy_fwd_kernel(f0_ref, f1_ref, x_ref, *rest, mult):
    o_ref = rest[-1]
    zre = jnp.dot(f0_ref[...], x_ref[...], preferred_element_type=F32)
    zim = jnp.dot(f1_ref[...], x_ref[...], preferred_element_type=F32)
    if mult:
        k_ref = rest[0]
        p, q, r = k_ref[0], k_ref[1], k_ref[2]
        o_ref[0] = (zre * p - zim * q).astype(o_ref.dtype)
        o_ref[1] = (zre * q + zim * r).astype(o_ref.dtype)
    else:
        o_ref[0] = zre
        o_ref[1] = zim


def _hy_fwd(fwd, x, nb, l, db, pqr):
    tr = min(TR, l)
    args = [fwd[0], fwd[1], x]
    in_specs = [_mat_spec(tr, l), _mat_spec(tr, l),
                pl.BlockSpec((l, db), lambda i, b: (b, 0))]
    if pqr is not None:
        args.append(pqr)
        in_specs.append(pl.BlockSpec((3, tr, db), lambda i, b: (0, i, 0)))
    return pl.pallas_call(
        functools.partial(_hy_fwd_kernel, mult=pqr is not None),
        grid=(l // tr, nb),
        in_specs=in_specs,
        out_specs=pl.BlockSpec((None, 2, tr, db), lambda i, b: (b, 0, i, 0)),
        out_shape=jax.ShapeDtypeStruct((nb, 2, l, db), F32 if pqr is None else BF16),
        compiler_params=_cp("parallel", "parallel"),
        name="hy_fwd",
    )(*args)


def _hy_inv_kernel(g0_ref, g1_ref, y_ref, zp_ref, gate_ref, bias_ref, *rest):
    o_ref = rest[-1]
    y = jnp.dot(g0_ref[...], y_ref[0], preferred_element_type=F32)
    y = y + jnp.dot(g1_ref[...], y_ref[1], preferred_element_type=F32)
    o_ref[...] = (gate_ref[...] * (y + bias_ref[...] * zp_ref[...].astype(F32))).astype(BF16)


def _hy_inv(inv, yhat, zprev, zcol, uc, gcol, bias, nb, l, db, out_row0, out_rows, out_init):
    tr = min(TR, l)
    nrt = l // tr
    args = [inv[0], inv[1], yhat, zprev, uc, bias]
    in_specs = [_mat_spec(tr, l), _mat_spec(tr, l),
                pl.BlockSpec((None, 2, l, db), lambda i, b: (b, 0, 0, 0)),
                pl.BlockSpec((tr, db), lambda i, b: (b * nrt + i, zcol)),
                pl.BlockSpec((tr, db), lambda i, b: (b * nrt + i, gcol)),
                pl.BlockSpec((1, db), lambda i, b: (0, 0))]
    aliases = {}
    if out_init is not None:
        args.append(out_init)
        in_specs.append(pl.BlockSpec(memory_space=pl.ANY))
        aliases = {6: 0}
    return pl.pallas_call(
        _hy_inv_kernel,
        grid=(nrt, nb),
        in_specs=in_specs,
        out_specs=pl.BlockSpec((tr, db), lambda i, b: (out_row0 // tr + b * nrt + i, 0)),
        out_shape=jax.ShapeDtypeStruct((out_rows, db), BF16),
        input_output_aliases=aliases,
        compiler_params=_cp("parallel", "parallel"),
        name="hy_inv",
    )(*args)


def _hyena_filters(length, hf_w1, hf_b1, hf_w2, hf_b2, hf_w3, hf_freq, db):
    emb = hf_w1.shape[0]
    nbands = (emb - 1) // 2
    t = jnp.linspace(0.0, 1.0, length, dtype=F32)[:, None]
    w = (2.0 * math.pi / length) * jnp.arange(length, dtype=F32)[:, None]
    bands = jnp.linspace(1e-4, nbands - 1, nbands, dtype=F32)[None, :]
    feats = jnp.concatenate([t, jnp.cos(bands * w), -jnp.sin(bands * w)], axis=-1)
    z = jnp.sin(hf_freq[0] * (jnp.dot(feats, hf_w1, precision=HIGHEST) + hf_b1))
    z = jnp.sin(hf_freq[1] * (jnp.dot(z, hf_w2, precision=HIGHEST) + hf_b2))
    filt = jnp.dot(z, hf_w3, precision=HIGHEST).reshape(length, -1, 2, db)
    min_decay = math.log(HY_DECAY_TARGET) / HY_SLOW_PCT
    max_decay = math.log(HY_DECAY_TARGET) / HY_FAST_PCT
    deltas = jnp.abs(jnp.linspace(min_decay, max_decay, db, dtype=F32))
    filt = filt * jnp.exp(-t * deltas)[:, None, None, :]
    return filt * lax.rsqrt(jnp.sum(filt * filt, axis=(0, 2), keepdims=True) + NORM_EPS)


def _filter_spectra(fwd, filt, l, db):
    n_ord = filt.shape[1]
    cols = jnp.transpose(filt, (1, 2, 0, 3))
    cols = cols.at[:, 1, 0, :].set(0.0)
    x = cols.reshape(n_ord * 2 * l, db).astype(BF16)
    spec = _hy_fwd(fwd, x, n_ord * 2, l, db, None).reshape(n_ord, 2, 2, l, db)
    hf, hb = spec[:, 0], spec[:, 1]
    k_re = hf[:, 0] + hb[:, 0]
    k_im = hf[:, 1] - hb[:, 1]
    k_nyq = hf[:, 1, 0] + hb[:, 1, 0]
    first = (jnp.arange(l) == 0)[None, :, None]
    scale = jnp.where(first, 1.0 / (2 * l), 2.0 / (2 * l)).astype(F32)
    p = k_re * scale
    q = jnp.where(first, 0.0, k_im) * scale
    r = jnp.where(first, k_nyq[:, None, :], k_re) * scale
    return jnp.stack([p, q, r], axis=1)


def _lru_kernel(*refs, rw, reverse, final):
    if final:
        (x_ref, w_ref, wa_ref, wx_ref, ba_ref, bx_ref, lam_ref, h0_ref, hf_ref, ug_ref,
         _, o_ref, st_ref, a_s, b_s, carry) = refs
    else:
        (x_ref, w_ref, wa_ref, wx_ref, ba_ref, bx_ref, lam_ref, h0_ref,
         o_ref, st_ref, a_s, b_s, carry) = refs
    tl = x_ref.shape[0]

    @pl.when(pl.program_id(1) == 0)
    def _():
        carry[...] = h0_ref[...]

    xc = _dwconv(x_ref[...], w_ref, w_ref.shape[0] // 2, rw)
    xb = xc.astype(BF16)
    r = _sigmoid(jnp.dot(xb, wa_ref[...], preferred_element_type=F32) + ba_ref[...])
    ig = _sigmoid(jnp.dot(xb, wx_ref[...], preferred_element_type=F32) + bx_ref[...])
    lam = lam_ref[...]
    softplus = jnp.maximum(-lam, 0.0) + jnp.log(1.0 + jnp.exp(-jnp.abs(lam)))
    log_a = -LRU_C * r * softplus
    a = jnp.exp(log_a)
    bv = jnp.sqrt(1.0 - jnp.exp(2.0 * log_a)) * (ig * xc)

    pos8 = lax.broadcasted_iota(jnp.int32, a.shape, 0) % 8
    for s in (1, 2, 4):
        if reverse:
            a_sh, b_sh, ok = pltpu.roll(a, tl - s, axis=0), pltpu.roll(bv, tl - s, axis=0), pos8 < 8 - s
        else:
            a_sh, b_sh, ok = pltpu.roll(a, s, axis=0), pltpu.roll(bv, s, axis=0), pos8 >= s
        bv = jnp.where(ok, a * b_sh + bv, bv)
        a = jnp.where(ok, a * a_sh, a)
    a_s[...] = a
    b_s[...] = bv

    ng = tl // 8

    def body(gi, c):
        g = ng - 1 - gi if reverse else gi
        sl = pl.ds(pl.multiple_of(g * 8, 8), 8)
        h8 = b_s[sl, :] + a_s[sl, :] * c
        b_s[sl, :] = h8
        return h8[0:1, :] if reverse else h8[7:8, :]

    c_out = lax.fori_loop(0, ng, body, carry[...])
    carry[...] = c_out
    st_ref[...] = c_out
    if final:
        ug = ug_ref[...]
        gelu = 0.5 * ug * (1.0 + jnp.tanh(math.sqrt(2.0 / math.pi) * (ug + 0.044715 * ug * ug * ug)))
        o_ref[...] = ((hf_ref[...] + b_s[...]) * gelu).astype(BF16)
    else:
        o_ref[...] = b_s[...]


def _lru_pass(u_all, w_lconv, wa, wx, ba, bx, lam, h0, row0, nb, l, tl, rw, db, reverse, hf=None, out_init=None,
              t_all=None):
    final = hf is not None
    nc = l // tl
    ob = row0 // tl
    cidx = (lambda c: nc - 1 - c) if reverse else (lambda c: c)
    full = lambda a: pl.BlockSpec(a.shape, lambda b, c: (0,) * a.ndim)
    ncolx, ncolg = 7, 8
    args = [u_all, w_lconv, wa, wx, ba, bx, lam, h0]
    in_specs = [pl.BlockSpec((tl, db), lambda b, c: (ob + b * nc + cidx(c), ncolx)),
                full(w_lconv), full(wa), full(wx), full(ba), full(bx), full(lam),
                pl.BlockSpec((None, 1, db), lambda b, c: (b, 0, 0))]
    aliases = {}
    if final:
        args += [hf, u_all]
        in_specs += [pl.BlockSpec((tl, db), lambda b, c: (b * nc + cidx(c), 0)),
                     pl.BlockSpec((tl, db), lambda b, c: (ob + b * nc + cidx(c), ncolg))]
        if out_init is None:
            out_init = jnp.zeros((8, 128), BF16)
        else:
            aliases = {10: 0}
        args.append(out_init)
        in_specs.append(pl.BlockSpec(memory_space=pl.ANY))
        out_spec = pl.BlockSpec((tl, db), lambda b, c: (ob + b * nc + cidx(c), 0))
        out_shape = jax.ShapeDtypeStruct((t_all, db), BF16)
    else:
        out_spec = pl.BlockSpec((tl, db), lambda b, c: (b * nc + cidx(c), 0))
        out_shape = jax.ShapeDtypeStruct((nb * l, db), F32)
    return pl.pallas_call(
        functools.partial(_lru_kernel, rw=rw, reverse=reverse, final=final),
        grid=(nb, nc),
        in_specs=in_specs,
        out_specs=[out_spec, pl.BlockSpec((None, 1, db), lambda b, c: (b, 0, 0))],
        out_shape=[out_shape, jax.ShapeDtypeStruct((nb, 1, db), F32)],
        scratch_shapes=[pltpu.VMEM((tl, db), F32), pltpu.VMEM((tl, db), F32), pltpu.VMEM((1, db), F32)],
        input_output_aliases=aliases,
        compiler_params=_cp("arbitrary", "arbitrary"),
        name="lru_bwd" if reverse else "lru_fwd",
    )(*args)


def _block_diag(w):
    h, hd, _ = w.shape
    eye = jnp.eye(h, dtype=w.dtype)
    return (eye[:, None, :, None] * w[:, :, None, :]).reshape(h * hd, h * hd).astype(BF16)


def _merge_kernel(x_ref, sh_ref, sc_ref, g_ref, b0_ref, b1_ref, b2_ref, b3_ref, wm_ref, wb_ref, wo_ref, ga_ref,
                  o_ref, hbf):
    j = pl.program_id(1)

    @pl.when(j == 0)
    def _():
        _norm_mod_store(hbf, x_ref, g_ref, sh_ref, sc_ref)

    h = hbf[...]
    m = None
    for k, br in enumerate((b0_ref, b1_ref, b2_ref, b3_ref)):
        g = jnp.dot(h, wm_ref[k], preferred_element_type=F32)
        p = jnp.dot(br[...], wb_ref[k], preferred_element_type=F32)
        term = _sigmoid(g) * p
        m = term if m is None else m + term
    part = jnp.dot(m.astype(BF16), wo_ref[...], preferred_element_type=F32)

    @pl.when(j == 0)
    def _():
        o_ref[...] = part

    @pl.when(j > 0)
    def _():
        o_ref[...] += part

    @pl.when(j == pl.num_programs(1) - 1)
    def _():
        o_ref[...] = x_ref[...] + ga_ref[...] * o_ref[...]


def _merge(branches, wm_bf, wb_bf, wo_bf, x_all, mod3, g, ntiles, nlt, tpb, nb):
    t, d = x_all.shape
    db = wb_bf.shape[1]
    tn = min(TN_MERGE, d)
    row = lambda w: pl.BlockSpec((TM, w), lambda i, j: (i, 0))
    return pl.pallas_call(
        _merge_kernel,
        grid=(ntiles, d // tn),
        in_specs=[row(d), _mod_spec(d, 0, nlt, tpb, nb), _mod_spec(d, 1, nlt, tpb, nb),
                  pl.BlockSpec((1, d), lambda i, j: (0, 0)),
                  row(db), row(db), row(db), row(db),
                  pl.BlockSpec((4, d, tn), lambda i, j: (0, 0, j)),
                  pl.BlockSpec((4, db, tn), lambda i, j: (0, 0, j)),
                  pl.BlockSpec((tn, d), lambda i, j: (j, 0)),
                  _mod_spec(d, 2, nlt, tpb, nb)],
        out_specs=row(d),
        out_shape=jax.ShapeDtypeStruct((ntiles * TM, d), F32),
        scratch_shapes=[pltpu.VMEM((TM, d), BF16)],
        compiler_params=_cp("parallel", "arbitrary"),
        name="merge",
    )(x_all, mod3, mod3, g.reshape(1, d), *branches, wm_bf, wb_bf, wo_bf, mod3)


def _router_kernel(x_ref, sh_ref, sc_ref, g_ref, wr_ref, br_ref, h_ref, r_ref, c_ref, *, ngroups, epg):
    h = _norm_mod(x_ref[...], g_ref[...], sh_ref[...], sc_ref[...])
    h_ref[...] = h
    logits = jnp.dot(h, wr_ref[...], preferred_element_type=F32, precision=HIGHEST) + br_ref[...]
    lane = lax.broadcasted_iota(jnp.int32, logits.shape, 1)
    lane_f = lane.astype(F32)
    neg = jnp.float32(-1e30)
    big = jnp.float32(1e6)
    is_g = lane < ngroups
    lg = jnp.where(is_g, logits, neg)
    mx = jnp.max(lg, axis=-1, keepdims=True)
    g_top = jnp.min(jnp.where(lg == mx, lane_f, big), axis=-1, keepdims=True)
    den = jnp.sum(jnp.where(is_g, jnp.exp(lg - mx), 0.0), axis=-1, keepdims=True)
    p_top = 1.0 / den
    el = lane_f - ngroups
    in_grp = (el >= g_top * epg) & (el < (g_top + 1.0) * epg)
    le = jnp.where(in_grp, logits, neg)
    v1 = jnp.max(le, axis=-1, keepdims=True)
    e1 = jnp.min(jnp.where(le == v1, el, big), axis=-1, keepdims=True)
    le2 = jnp.where(el == e1, neg, le)
    v2 = jnp.max(le2, axis=-1, keepdims=True)
    e2 = jnp.min(jnp.where(le2 == v2, el, big), axis=-1, keepdims=True)
    dlt = jnp.exp(v2 - v1)
    w1 = p_top / (1.0 + dlt)
    w2 = p_top * dlt / (1.0 + dlt)
    tm = logits.shape[0]
    pick1 = el == e1
    pick2 = el == e2
    onehot = jnp.where(pick1 | pick2, 1.0, 0.0)
    row = lax.broadcasted_iota(jnp.int32, (tm, tm), 0)
    col = lax.broadcasted_iota(jnp.int32, (tm, tm), 1)
    earlier = jnp.where(col < row, 1.0, 0.0).astype(BF16)
    before = jnp.dot(earlier, onehot.astype(BF16), preferred_element_type=F32)
    rank1 = jnp.sum(jnp.where(pick1, before, 0.0), axis=-1, keepdims=True)
    rank2 = jnp.sum(jnp.where(pick2, before, 0.0), axis=-1, keepdims=True)
    out = jnp.where(lane == 0, e1, 0.0)
    out = jnp.where(lane == 1, e2, out)
    out = jnp.where(lane == 2, w1, out)
    out = jnp.where(lane == 3, w2, out)
    out = jnp.where(lane == 4, rank1, out)
    out = jnp.where(lane == 5, rank2, out)
    r_ref[...] = out
    c_ref[...] = jnp.broadcast_to(jnp.sum(onehot, axis=0, keepdims=True), c_ref.shape)


def _router(x_all, mod3, g, w_router, b_router, ntiles, nlt, tpb, nb, ngroups, epg):
    t, d = x_all.shape
    return pl.pallas_call(
        functools.partial(_router_kernel, ngroups=ngroups, epg=epg),
        grid=(ntiles,),
        in_specs=[pl.BlockSpec((TM, d), lambda i: (i, 0)),
                  _mod_spec(d, 3, nlt, tpb, nb), _mod_spec(d, 4, nlt, tpb, nb),
                  pl.BlockSpec((1, d), lambda i: (0, 0)),
                  pl.BlockSpec((d, 128), lambda i: (0, 0)),
                  pl.BlockSpec((1, 128), lambda i: (0, 0))],
        out_specs=[pl.BlockSpec((TM, d), lambda i: (i, 0)), pl.BlockSpec((TM, 128), lambda i: (i, 0)),
                   pl.BlockSpec((None, 8, 128), lambda i: (i, 0, 0))],
        out_shape=[jax.ShapeDtypeStruct((ntiles * TM, d), F32), jax.ShapeDtypeStruct((ntiles * TM, 128), F32),
                   jax.ShapeDtypeStruct((ntiles, 8, 128), F32)],
        compiler_params=_cp("parallel"),
        name="router",
    )(x_all, mod3, mod3, g.reshape(1, d), w_router, b_router)


def _route_tables(slab, tile_counts, ngroups, n_experts, tm):
    ntiles = tile_counts.shape[0]
    e = slab[:, 0:TOP_K].astype(jnp.int32).reshape(ntiles, TM, TOP_K)
    rank = slab[:, 4:4 + TOP_K].astype(jnp.int32).reshape(ntiles, TM, TOP_K)
    counts_t = tile_counts[:, 0, ngroups:ngroups + n_experts].astype(jnp.int32)
    base = jnp.cumsum(counts_t, axis=0) - counts_t
    counts = jnp.sum(counts_t, axis=0)
    tiles_per_e = (counts + tm - 1) // tm
    tile_end = jnp.cumsum(tiles_per_e)
    starts = (tile_end - tiles_per_e) * tm
    off = starts[None, :] + base
    sel = e[..., None] == jnp.arange(n_experts, dtype=jnp.int32)
    slot = jnp.sum(jnp.where(sel, off[:, None, None, :], 0), axis=-1) + rank
    n_slot_tiles = (ntiles * TM * TOP_K) // tm + n_experts
    n_valid = tile_end[-1]
    tile_ids = jnp.minimum(jnp.arange(n_slot_tiles, dtype=jnp.int32), n_valid - 1)
    tile_expert = jnp.sum((tile_end[None, :] <= tile_ids[:, None]).astype(jnp.int32), axis=1)
    return slot, tile_expert, n_valid.reshape(1).astype(jnp.int32), n_slot_tiles


ROW_DMA_UNROLL = 8


def _stage_indices(idx_hbm_row, idx_smem, sem_i):
    cp = pltpu.make_async_copy(idx_hbm_row, idx_smem, sem_i)
    cp.start()
    cp.wait()


def _dispatch_kernel(p_hbm, h_ref, xs_in, xs_hbm, idx, sem_i, sem_x):
    del xs_in
    i = pl.program_id(0)
    tm = h_ref.shape[0]
    _stage_indices(p_hbm.at[i], idx, sem_i)

    def issue(r, c):
        src = h_ref.at[pl.ds(r, 1), :]
        for k in range(TOP_K):
            pltpu.make_async_copy(src, xs_hbm.at[pl.ds(idx[k * tm + r], 1), :], sem_x).start()
        return c

    lax.fori_loop(0, tm, issue, 0, unroll=ROW_DMA_UNROLL)
    for k in range(TOP_K):
        pltpu.make_async_copy(h_ref, xs_hbm.at[pl.ds(0, tm), :], sem_x).wait()


def _dispatch(h_all, slot, n_slot_rows):
    ntiles = slot.shape[0]
    d = h_all.shape[1]
    p = jnp.transpose(slot, (0, 2, 1)).reshape(ntiles, TOP_K * TM)
    xs0 = jnp.zeros((n_slot_rows, d), F32)
    return pl.pallas_call(
        _dispatch_kernel,
        grid=(ntiles,),
        in_specs=[pl.BlockSpec(memory_space=pl.ANY), pl.BlockSpec((TM, d), lambda i: (i, 0)),
                  pl.BlockSpec(memory_space=pl.ANY)],
        out_specs=pl.BlockSpec(memory_space=pl.ANY),
        out_shape=jax.ShapeDtypeStruct((n_slot_rows, d), F32),
        scratch_shapes=[pltpu.SMEM((TOP_K * TM,), jnp.int32),
                        pltpu.SemaphoreType.DMA(()), pltpu.SemaphoreType.DMA(())],
        input_output_aliases={2: 0},
        compiler_params=_cp("arbitrary"),
        name="dispatch",
    )(p, h_all, xs0)


def _cast_kernel(w_ref, o_ref):
    o_ref[...] = w_ref[...].astype(o_ref.dtype)


def _layer_bf16(w_stack, l):
    _, e, a, b = w_stack.shape
    return pl.pallas_call(
        _cast_kernel,
        grid=(e,),
        in_specs=[pl.BlockSpec((None, None, a, b), lambda i: (l, i, 0, 0))],
        out_specs=pl.BlockSpec((None, a, b), lambda i: (i, 0, 0)),
        out_shape=jax.ShapeDtypeStruct((e, a, b), BF16),
        compiler_params=_cp("parallel"),
        name="cast_bf16",
    )(w_stack)


def _moe_kernel(te_ref, nv_ref, x_ref, wg_ref, wu_ref, wd_ref, o_ref):
    del te_ref
    valid = pl.program_id(0) < nv_ref[0]

    @pl.when(valid)
    def _():
        x = x_ref[...].astype(BF16)
        g = jnp.dot(x, wg_ref[...], preferred_element_type=F32)
        u = jnp.dot(x, wu_ref[...], preferred_element_type=F32)
        hid = (g * _sigmoid(g) * u).astype(BF16)
        o_ref[...] = jnp.dot(hid, wd_ref[...], preferred_element_type=F32)

    @pl.when(jnp.logical_not(valid))
    def _():
        o_ref[...] = jnp.zeros_like(o_ref)


def _moe_experts(xs, tile_expert, n_valid, w_gate, w_up, w_down, tm):
    n_rows, d = xs.shape
    f = w_gate.shape[2]
    xi = lambda i, te, nv: (jnp.maximum(jnp.minimum(i, nv[0] - 1), 0), 0)
    grid_spec = pltpu.PrefetchScalarGridSpec(
        num_scalar_prefetch=2,
        grid=(n_rows // tm,),
        in_specs=[pl.BlockSpec((tm, d), xi),
                  pl.BlockSpec((None, d, f), lambda i, te, nv: (te[i], 0, 0)),
                  pl.BlockSpec((None, d, f), lambda i, te, nv: (te[i], 0, 0)),
                  pl.BlockSpec((None, f, d), lambda i, te, nv: (te[i], 0, 0))],
        out_specs=pl.BlockSpec((tm, d), lambda i, te, nv: (i, 0)),
    )
    return pl.pallas_call(
        _moe_kernel,
        grid_spec=grid_spec,
        out_shape=jax.ShapeDtypeStruct((n_rows, d), F32),
        compiler_params=_cp("arbitrary"),
        name="moe_experts",
    )(tile_expert, n_valid, xs, w_gate, w_up, w_down)


def _combine_kernel(p_hbm, ys_hbm, x_ref, slab_ref, ga_ref, g_ref, o_ref, idx, buf, sem_i, sem_x, *, final):
    i = pl.program_id(0)
    n = pl.num_programs(0)
    rows = idx.shape[0]
    tm = rows // TOP_K

    def fetch(tile, s):
        _stage_indices(p_hbm.at[tile], idx, sem_i)

        def issue(r, c):
            pltpu.make_async_copy(ys_hbm.at[pl.ds(idx[r], 1), :], buf.at[s, pl.ds(r, 1), :], sem_x.at[s]).start()
            return c

        lax.fori_loop(0, rows, issue, 0, unroll=ROW_DMA_UNROLL)

    @pl.when(i == 0)
    def _():
        fetch(0, 0)

    @pl.when(i + 1 < n)
    def _():
        fetch(i + 1, (i + 1) % 2)

    s = i % 2
    pltpu.make_async_copy(ys_hbm.at[pl.ds(0, rows), :], buf.at[s], sem_x.at[s]).wait()
    slab = slab_ref[...]
    y = slab[:, TOP_K:TOP_K + 1] * buf[s, 0:tm, :]
    for k in range(1, TOP_K):
        y = y + slab[:, TOP_K + k:TOP_K + k + 1] * buf[s, k * tm:(k + 1) * tm, :]
    x = x_ref[...] + ga_ref[...] * y
    if final:
        x = x * lax.rsqrt(jnp.mean(x * x, axis=-1, keepdims=True) + NORM_EPS) * g_ref[...]
    o_ref[...] = x


def _combine(ys, slot, slab, x_all, mod3, g_final, ntiles, nlt, tpb, nb, final):
    d = x_all.shape[1]
    tm = TM_COMB
    f = TM // tm
    n = ntiles * f
    p = jnp.transpose(slot.reshape(n, tm, TOP_K), (0, 2, 1)).reshape(n, TOP_K * tm)
    return pl.pallas_call(
        functools.partial(_combine_kernel, final=final),
        grid=(n,),
        in_specs=[pl.BlockSpec(memory_space=pl.ANY), pl.BlockSpec(memory_space=pl.ANY),
                  pl.BlockSpec((tm, d), lambda i: (i, 0)),
                  pl.BlockSpec((tm, 128), lambda i: (i, 0)),
                  pl.BlockSpec((None, 1, d), lambda i: (jnp.where(i < nlt * f, i // (tpb * f), nb), 0, 5)),
                  pl.BlockSpec((1, d), lambda i: (0, 0))],
        out_specs=pl.BlockSpec((tm, d), lambda i: (i, 0)),
        out_shape=jax.ShapeDtypeStruct((n * tm, d), F32),
        scratch_shapes=[pltpu.SMEM((TOP_K * tm,), jnp.int32),
                        pltpu.VMEM((2, TOP_K * tm, d), F32),
                        pltpu.SemaphoreType.DMA(()), pltpu.SemaphoreType.DMA((2,))],
        compiler_params=_cp("arbitrary"),
        name="combine",
    )(p, ys, x_all, slab, mod3, g_final.reshape(1, d))


def kernel(x, c, ctx, c_ctx, w_mod, b_mod, g_norm1, g_norm2, g_final, w_in, w_merge, w_branch, w_out, w_hconv,
           hy_bias, hf_w1, hf_b1, hf_w2, hf_b2, hf_w3, hf_freq, w_sconv, w_lconv, lru_wa, lru_ba, lru_wx, lru_bx,
           lru_lambda, w_rg, b_rg, w_re, b_re, w_e_gate, w_e_up, w_e_down):
    nb, seq, d = x.shape
    cl = ctx.shape[1]
    depth = w_in.shape[0]
    db = w_branch.shape[2]
    gdim = db // FNET_GROUPS
    ngroups = w_rg.shape[2]
    epg = w_re.shape[3]
    n_experts = ngroups * epg
    t_lat, t_ctx = nb * seq, nb * cl
    t_all = t_lat + t_ctx
    nlt, tpb = t_lat // TM, seq // TM
    nat = t_all // TM
    tl_lat, tl_ctx = min(TL, seq), min(TL, cl)

    x_all = jnp.concatenate([x.reshape(t_lat, d), ctx.reshape(t_ctx, d)], axis=0)
    nrows = -(-(nb + 1) // 8) * 8
    cvec = jnp.zeros((nrows, d), F32).at[:nb].set(c).at[nb].set(c_ctx)
    mods = _adaln(cvec, w_mod, b_mod)

    c_lat, sf_lat, si_lat = _dft_mats(seq)
    c_ctx, sf_ctx, si_ctx = _dft_mats(cl)
    fwd_lat, inv_lat = (c_lat, sf_lat), (c_lat, si_lat)
    fwd_ctx, inv_ctx = (c_ctx, sf_ctx), (c_ctx, si_ctx)
    fc_lat, fs_lat, w_pq = _fnet_mats(seq, gdim, FNET_GROUPS)
    fc_ctx, fs_ctx, _ = _fnet_mats(cl, gdim, FNET_GROUPS)

    out = None
    for l in range(depth):
        last = l == depth - 1
        mod3 = mods[l].reshape(nrows, 1, 6 * d)
        u_all = _inproj(x_all, mod3, g_norm1[l], w_in[l].astype(BF16), nlt, tpb, nb)

        wa = [_block_diag(lru_wa[l, dd]) for dd in range(2)]
        wx = [_block_diag(lru_wx[l, dd]) for dd in range(2)]
        ba = [lru_ba[l, dd].reshape(1, db) for dd in range(2)]
        bx = [lru_bx[l, dd].reshape(1, db) for dd in range(2)]
        lam = [lru_lambda[l, dd].reshape(1, db) for dd in range(2)]
        zeros_st = jnp.zeros((nb, 1, db), F32)
        rows_out = t_lat if last else t_all

        def lru(row0, length, tl, rw, h0f, h0b, combine, out_init):
            hf, stf = _lru_pass(u_all, w_lconv[l], wa[0], wx[0], ba[0], bx[0], lam[0], h0f,
                                row0, nb, length, tl, rw, db, False)
            if combine:
                y, stb = _lru_pass(u_all, w_lconv[l], wa[1], wx[1], ba[1], bx[1], lam[1], h0b,
                                   row0, nb, length, tl, rw, db, True, hf=hf, out_init=out_init, t_all=rows_out)
            else:
                y, stb = _lru_pass(u_all, w_lconv[l], wa[1], wx[1], ba[1], bx[1], lam[1], h0b,
                                   row0, nb, length, tl, rw, db, True)
            return y, stf, stb

        def hyena(row0, length, tl, rw, fwd, inv, uc, v_bf, out_init):
            filt = _hyena_filters(length, hf_w1[l], hf_b1[l], hf_w2[l], hf_b2[l], hf_w3[l], hf_freq[l], db)
            pqr = _filter_spectra(fwd, filt, length, db)
            yhat = _hy_fwd(fwd, v_bf, nb, length, db, pqr[0])
            z2 = _hy_inv(inv, yhat, uc, 0, uc, 1, hy_bias[l, 0].reshape(1, db), nb, length, db,
                         0, nb * length, None)
            yhat = _hy_fwd(fwd, z2, nb, length, db, pqr[1])
            return _hy_inv(inv, yhat, z2, 0, uc, 2, hy_bias[l, 1].reshape(1, db), nb, length, db,
                           row0, rows_out, out_init)

        zbuf = lambda: None if last else jnp.zeros((rows_out, db), BF16)
        if last:
            _, st_f, st_b = lru(t_lat, cl, tl_ctx, cl, zeros_st, zeros_st, False, None)
        else:
            y_lru_c, st_f, st_b = lru(t_lat, cl, tl_ctx, cl, zeros_st, zeros_st, True, zbuf())
        y_lru, _, _ = lru(0, seq, tl_lat, GRID_W, st_f, st_b, True, None if last else y_lru_c)

        pq = _fnet_pq(u_all, w_pq, db)
        uc_lat, v_lat, y_sc = _conv_stage(u_all, w_hconv[l], w_sconv[l], 0, t_lat, tl_lat, GRID_W, db, zbuf(),
                                          rows_out)
        y_fn = _fnet_dft(fc_lat, fs_lat, pq, 0, nb, seq, db, gdim, zbuf(), rows_out)
        if not last:
            uc_ctx, v_ctx, y_sc = _conv_stage(u_all, w_hconv[l], w_sconv[l], t_lat, t_ctx, tl_ctx, cl, db, y_sc,
                                              rows_out)
            y_fn = _fnet_dft(fc_ctx, fs_ctx, pq, t_lat, nb, cl, db, gdim, y_fn, rows_out)
            y_hy = hyena(t_lat, cl, tl_ctx, cl, fwd_ctx, inv_ctx, uc_ctx, v_ctx, zbuf())
        else:
            y_hy = None
        y_hy = hyena(0, seq, tl_lat, GRID_W, fwd_lat, inv_lat, uc_lat, v_lat, y_hy)

        ntiles = nlt if last else nat
        x_mid = _merge((y_fn, y_hy, y_sc, y_lru), w_merge[l].astype(BF16), w_branch[l].astype(BF16),
                       w_out[l].astype(BF16), x_all, mod3, g_norm1[l], ntiles, nlt, tpb, nb)

        w_router = jnp.zeros((d, 128), F32).at[:, :ngroups].set(w_rg[l])
        w_router = w_router.at[:, ngroups:ngroups + n_experts].set(
            jnp.transpose(w_re[l], (1, 0, 2)).reshape(d, n_experts))
        b_router = jnp.zeros((1, 128), F32).at[0, :ngroups].set(b_rg[l])
        b_router = b_router.at[0, ngroups:ngroups + n_experts].set(b_re[l].reshape(-1))
        h2, slab, tile_counts = _router(x_mid, mod3, g_norm2[l], w_router, b_router, ntiles, nlt, tpb, nb,
                                        ngroups, epg)
        slot, tile_expert, n_valid, n_slot_tiles = _route_tables(slab, tile_counts, ngroups, n_experts, TM_MOE)
        xs = _dispatch(h2, slot, n_slot_tiles * TM_MOE)
        ys = _moe_experts(xs, tile_expert, n_valid, _layer_bf16(w_e_gate, l), _layer_bf16(w_e_up, l),
                          _layer_bf16(w_e_down, l), TM_MOE)
        x_all = _combine(ys, slot, slab, x_mid, mod3, g_final, ntiles, nlt, tpb, nb, last)
        if last:
            out = x_all.reshape(nb, seq, d)
    return out
```

```python
import functools
import math

import jax
import jax.numpy as jnp
from jax import lax
from jax.experimental import pallas as pl
from jax.experimental.pallas import tpu as pltpu

F32 = jnp.float32
BF16 = jnp.bfloat16
HIGHEST = lax.Precision.HIGHEST

NORM_EPS = 1e-6
GRID_W = 64
FNET_GROUPS = 4
LRU_C = 8.0
HY_DECAY_TARGET = 1e-2
HY_FAST_PCT = 0.3
HY_SLOW_PCT = 1.5
TOP_K = 2

TM = 512
TL = 512
TR = 512
TM_MOE = 512
TM_COMB = 256
TN_IN = 1536
TN_MERGE = 512
TF_MOE = 512
VMEM_LIMIT = 56 * 1024 * 1024


def _cp(*sem):
    return pltpu.CompilerParams(dimension_semantics=sem, vmem_limit_bytes=VMEM_LIMIT)


def _sigmoid(x):
    return 1.0 / (1.0 + jnp.exp(-x))


def _adaln_kernel(c_ref, w_ref, b_ref, o_ref):
    c = c_ref[...]
    s = c * _sigmoid(c)
    o_ref[...] = jnp.dot(s, w_ref[...], preferred_element_type=F32, precision=HIGHEST) + b_ref[...]


def _adaln(cvec, w_mod, b_mod):
    nl, d, n6 = w_mod.shape
    r = cvec.shape[0]
    tn = min(1024, n6)
    return pl.pallas_call(
        _adaln_kernel,
        grid=(nl, n6 // tn),
        in_specs=[pl.BlockSpec((r, d), lambda l, j: (0, 0)),
                  pl.BlockSpec((None, d, tn), lambda l, j: (l, 0, j)),
                  pl.BlockSpec((None, 1, tn), lambda l, j: (l, 0, j))],
        out_specs=pl.BlockSpec((None, r, tn), lambda l, j: (l, 0, j)),
        out_shape=jax.ShapeDtypeStruct((nl, r, n6), F32),
        compiler_params=_cp("parallel", "parallel"),
        name="adaln",
    )(cvec, w_mod, b_mod.reshape(nl, 1, n6))


def _norm_mod(x, g, shift, scale):
    y = x * lax.rsqrt(jnp.mean(x * x, axis=-1, keepdims=True) + NORM_EPS) * g
    return y * (1.0 + scale) + shift


def _norm_mod_store(dst, x_ref, g_ref, sh_ref, sc_ref, rows=128):
    rows = min(rows, dst.shape[0])

    def body(c, carry):
        sl = pl.ds(pl.multiple_of(c * rows, rows), rows)
        dst[sl, :] = _norm_mod(x_ref[sl, :], g_ref[...], sh_ref[...], sc_ref[...]).astype(dst.dtype)
        return carry

    lax.fori_loop(0, dst.shape[0] // rows, body, 0)


def _mod_spec(d, chunk, nlt, tpb, nb):
    return pl.BlockSpec((None, 1, d), lambda i, *_: (jnp.where(i < nlt, i // tpb, nb), 0, chunk))


def _inproj_kernel(x_ref, sh_ref, sc_ref, g_ref, w_ref, u_ref, hbf, *, tn):
    j = pl.program_id(1)

    @pl.when(j == 0)
    def _():
        _norm_mod_store(hbf, x_ref, g_ref, sh_ref, sc_ref)

    for jc in range(w_ref.shape[1] // tn):
        @pl.when(j == jc)
        def _(jc=jc):
            u_ref[...] = jnp.dot(hbf[...], w_ref[:, jc * tn:(jc + 1) * tn], preferred_element_type=F32)


def _inproj(x_all, mod3, g, w_bf, nlt, tpb, nb):
    t, d = x_all.shape
    n = w_bf.shape[1]
    tn = TN_IN if n % TN_IN == 0 else n
    return pl.pallas_call(
        functools.partial(_inproj_kernel, tn=tn),
        grid=(t // TM, n // tn),
        in_specs=[pl.BlockSpec((TM, d), lambda i, j: (i, 0)),
                  _mod_spec(d, 0, nlt, tpb, nb), _mod_spec(d, 1, nlt, tpb, nb),
                  pl.BlockSpec((1, d), lambda i, j: (0, 0)),
                  pl.BlockSpec((d, n), lambda i, j: (0, 0))],
        out_specs=pl.BlockSpec((TM, tn), lambda i, j: (i, j)),
        out_shape=jax.ShapeDtypeStruct((t, n), F32),
        scratch_shapes=[pltpu.VMEM((TM, d), BF16)],
        compiler_params=_cp("parallel", "arbitrary"),
        name="inproj",
    )(x_all, mod3, mod3, g.reshape(1, d), w_bf)


def _dwconv(x, w_ref, left, rw):
    tl = x.shape[0]
    pos = lax.broadcasted_iota(jnp.int32, x.shape, 0) % rw
    acc = None
    for k in range(w_ref.shape[0]):
        off = k - left
        if off == 0:
            term = x
        else:
            shifted = pltpu.roll(x, (-off) % tl, axis=0)
            ok = (pos + off >= 0) & (pos + off < rw)
            term = jnp.where(ok, shifted, 0.0)
        term = term * w_ref[k:k + 1, :]
        acc = term if acc is None else acc + term
    return acc


def _conv_kernel(hv_ref, h1_ref, h2_ref, sb_ref, sc_ref, sh_ref, wh_ref, ws_ref, ys_in,
                 v_ref, x1_ref, x2_ref, vbf_ref, ys_ref, *, rw, db):
    del ys_in
    for k, (ref, dst) in enumerate(((hv_ref, v_ref), (h1_ref, x1_ref), (h2_ref, x2_ref))):
        y = _dwconv(ref[...], wh_ref.at[:, k * db:(k + 1) * db], 1, rw)
        dst[...] = y
        if k == 0:
            vbf_ref[...] = y.astype(BF16)
    ys_ref[...] = (sb_ref[...] * _dwconv(sc_ref[...] * sh_ref[...], ws_ref, 1, rw)).astype(BF16)


def _conv_stage(u_all, w_hconv, w_sconv, row0, nrows, tl, rw, db, ys_init):
    ob = row0 // tl
    col = lambda c: pl.BlockSpec((tl, db), lambda i: (ob + i, c))
    own = pl.BlockSpec((tl, db), lambda i: (i, 0))
    return pl.pallas_call(
        functools.partial(_conv_kernel, rw=rw, db=db),
        grid=(nrows // tl,),
        in_specs=[col(1), col(2), col(3), col(4), col(5), col(6),
                  pl.BlockSpec(w_hconv.shape, lambda i: (0, 0)),
                  pl.BlockSpec(w_sconv.shape, lambda i: (0, 0)),
                  pl.BlockSpec(memory_space=pl.ANY)],
        out_specs=[own, own, own, own, pl.BlockSpec((tl, db), lambda i: (ob + i, 0))],
        out_shape=[jax.ShapeDtypeStruct((nrows, db), F32)] * 3
                  + [jax.ShapeDtypeStruct((nrows, db), BF16), jax.ShapeDtypeStruct(ys_init.shape, BF16)],
        input_output_aliases={8: 4},
        compiler_params=_cp("parallel"),
        name="conv",
    )(*([u_all] * 6), w_hconv, w_sconv, ys_init)


def _trig_outer(l, period):
    q = 1 << ((l.bit_length() - 1) // 2)
    n = lax.broadcasted_iota(jnp.int32, (1, l), 1)
    scale = 2.0 * math.pi / period

    def table(rows, step):
        r = lax.broadcasted_iota(jnp.int32, (rows, 1), 0) * step
        ang = ((r * n) % period).astype(F32) * scale
        return jnp.cos(ang), jnp.sin(ang)

    ac, as_ = table(l // q, q)
    bc, bs = table(q, 1)
    c = ac[:, None, :] * bc[None] - as_[:, None, :] * bs[None]
    s = as_[:, None, :] * bc[None] + ac[:, None, :] * bs[None]
    return c.reshape(l, l), s.reshape(l, l)


def _dft_mats(l):
    c, s = _trig_outer(l, 2 * l)
    k = lax.broadcasted_iota(jnp.int32, (l, l), 0)
    n = lax.broadcasted_iota(jnp.int32, (l, l), 1)
    alt_n = jnp.where(n % 2 == 0, 1.0, -1.0).astype(F32)
    alt_k = jnp.where(k % 2 == 0, 1.0, -1.0).astype(F32)
    return c.astype(BF16), jnp.where(k == 0, alt_n, -s).astype(BF16), jnp.where(n == 0, alt_k, -s).astype(BF16)


def _fnet_mats(l, gdim, groups):
    c, s = _trig_outer(l, l)
    cg, sg = _trig_outer(gdim, gdim)
    eye = jnp.eye(groups, dtype=F32)
    w_pq = jnp.concatenate([jnp.kron(eye, cg), jnp.kron(eye, sg)], axis=1).astype(BF16)
    return c.astype(BF16), (-s).astype(BF16), w_pq


def _fnet_pq_kernel(u_ref, w_ref, o_ref):
    o_ref[...] = jnp.dot(u_ref[...].astype(BF16), w_ref[...], preferred_element_type=F32).astype(BF16)


def _fnet_pq(u_all, w_pq, db):
    t = u_all.shape[0]
    return pl.pallas_call(
        _fnet_pq_kernel,
        grid=(t // TM,),
        in_specs=[pl.BlockSpec((TM, db), lambda i: (i, 0)),
                  pl.BlockSpec(w_pq.shape, lambda i: (0, 0))],
        out_specs=pl.BlockSpec((TM, 2 * db), lambda i: (i, 0)),
        out_shape=jax.ShapeDtypeStruct((t, 2 * db), BF16),
        compiler_params=_cp("parallel"),
        name="fnet_pq",
    )(u_all, w_pq)


def _fnet_dft_kernel(d0_ref, d1_ref, x_ref, *rest, db, scale):
    o_ref = rest[-1]
    acc = jnp.dot(d0_ref[...], x_ref[:, :db], preferred_element_type=F32)
    acc = acc + jnp.dot(d1_ref[...], x_ref[:, db:], preferred_element_type=F32)
    o_ref[...] = (acc * scale).astype(BF16)


def _mat_spec(tr, l):
    return pl.BlockSpec((tr, l), lambda i, b: (i, 0))


def _fnet_dft(d0, d1, pq, row0, nb, l, db, gdim, out_init, t_all):
    tr = min(TR, l)
    nrt = l // tr
    args = [d0, d1, pq]
    in_specs = [_mat_spec(tr, l), _mat_spec(tr, l),
                pl.BlockSpec((l, 2 * db), lambda i, b: (row0 // l + b, 0))]
    aliases = {}
    if out_init is not None:
        args.append(out_init)
        in_specs.append(pl.BlockSpec(memory_space=pl.ANY))
        aliases = {3: 0}
    return pl.pallas_call(
        functools.partial(_fnet_dft_kernel, db=db, scale=1.0 / math.sqrt(l * gdim)),
        grid=(nrt, nb),
        in_specs=in_specs,
        out_specs=pl.BlockSpec((tr, db), lambda i, b: (row0 // tr + b * nrt + i, 0)),
        out_shape=jax.ShapeDtypeStruct((t_all, db), BF16),
        input_output_aliases=aliases,
        compiler_params=_cp("parallel", "parallel"),
        name="fnet_dft",
    )(*args)


def _hy_fwd_kernel(f0_ref, f1_ref, x_ref, *rest, mult):
    o_ref = rest[-1]
    zre = jnp.dot(f0_ref[...], x_ref[...], preferred_element_type=F32)
    zim = jnp.dot(f1_ref[...], x_ref[...], preferred_element_type=F32)
    if mult:
        k_ref = rest[0]
        p, q, r = k_ref[0], k_ref[1], k_ref[2]
        o_ref[0] = (zre * p - zim * q).astype(o_ref.dtype)
        o_ref[1] = (zre * q + zim * r).astype(o_ref.dtype)
    else:
        o_ref[0] = zre
        o_ref[1] = zim


def _hy_fwd(fwd, x, nb, l, db, pqr):
    tr = min(TR, l)
    args = [fwd[0], fwd[1], x]
    in_specs = [_mat_spec(tr, l), _mat_spec(tr, l),
                pl.BlockSpec((l, db), lambda i, b: (b, 0))]
    if pqr is not None:
        args.append(pqr)
        in_specs.append(pl.BlockSpec((3, tr, db), lambda i, b: (0, i, 0)))
    return pl.pallas_call(
        functools.partial(_hy_fwd_kernel, mult=pqr is not None),
        grid=(l // tr, nb),
        in_specs=in_specs,
        out_specs=pl.BlockSpec((None, 2, tr, db), lambda i, b: (b, 0, i, 0)),
        out_shape=jax.ShapeDtypeStruct((nb, 2, l, db), F32 if pqr is None else BF16),
        compiler_params=_cp("parallel", "parallel"),
        name="hy_fwd",
    )(*args)


def _hy_inv_kernel(g0_ref, g1_ref, y_ref, zp_ref, gate_ref, bias_ref, *rest):
    o_ref = rest[-1]
    y = jnp.dot(g0_ref[...], y_ref[0], preferred_element_type=F32)
    y = y + jnp.dot(g1_ref[...], y_ref[1], preferred_element_type=F32)
    o_ref[...] = (gate_ref[...] * (y + bias_ref[...] * zp_ref[...].astype(F32))).astype(BF16)


def _hy_inv(inv, yhat, zprev, zcol, gate, gcol, bias, nb, l, db, out_row0, out_rows, out_init):
    tr = min(TR, l)
    nrt = l // tr
    args = [inv[0], inv[1], yhat, zprev, gate, bias]
    in_specs = [_mat_spec(tr, l), _mat_spec(tr, l),
                pl.BlockSpec((None, 2, l, db), lambda i, b: (b, 0, 0, 0)),
                pl.BlockSpec((tr, db), lambda i, b: (b * nrt + i, zcol)),
                pl.BlockSpec((tr, db), lambda i, b: (b * nrt + i, gcol)),
                pl.BlockSpec((1, db), lambda i, b: (0, 0))]
    aliases = {}
    if out_init is not None:
        args.append(out_init)
        in_specs.append(pl.BlockSpec(memory_space=pl.ANY))
        aliases = {6: 0}
    return pl.pallas_call(
        _hy_inv_kernel,
        grid=(nrt, nb),
        in_specs=in_specs,
        out_specs=pl.BlockSpec((tr, db), lambda i, b: (out_row0 // tr + b * nrt + i, 0)),
        out_shape=jax.ShapeDtypeStruct((out_rows, db), BF16),
        input_output_aliases=aliases,
        compiler_params=_cp("parallel", "parallel"),
        name="hy_inv",
    )(*args)


def _hyena_filters(length, hf_w1, hf_b1, hf_w2, hf_b2, hf_w3, hf_freq, db):
    emb = hf_w1.shape[0]
    nbands = (emb - 1) // 2
    t = jnp.linspace(0.0, 1.0, length, dtype=F32)[:, None]
    w = (2.0 * math.pi / length) * jnp.arange(length, dtype=F32)[:, None]
    bands = jnp.linspace(1e-4, nbands - 1, nbands, dtype=F32)[None, :]
    feats = jnp.concatenate([t, jnp.cos(bands * w), -jnp.sin(bands * w)], axis=-1)
    z = jnp.sin(hf_freq[0] * (jnp.dot(feats, hf_w1, precision=HIGHEST) + hf_b1))
    z = jnp.sin(hf_freq[1] * (jnp.dot(z, hf_w2, precision=HIGHEST) + hf_b2))
    filt = jnp.dot(z, hf_w3, precision=HIGHEST).reshape(length, -1, 2, db)
    min_decay = math.log(HY_DECAY_TARGET) / HY_SLOW_PCT
    max_decay = math.log(HY_DECAY_TARGET) / HY_FAST_PCT
    deltas = jnp.abs(jnp.linspace(min_decay, max_decay, db, dtype=F32))
    filt = filt * jnp.exp(-t * deltas)[:, None, None, :]
    return filt * lax.rsqrt(jnp.sum(filt * filt, axis=(0, 2), keepdims=True) + NORM_EPS)


def _filter_spectra(fwd, filt, l, db):
    n_ord = filt.shape[1]
    cols = jnp.transpose(filt, (1, 2, 0, 3))
    cols = cols.at[:, 1, 0, :].set(0.0)
    x = cols.reshape(n_ord * 2 * l, db).astype(BF16)
    spec = _hy_fwd(fwd, x, n_ord * 2, l, db, None).reshape(n_ord, 2, 2, l, db)
    hf, hb = spec[:, 0], spec[:, 1]
    k_re = hf[:, 0] + hb[:, 0]
    k_im = hf[:, 1] - hb[:, 1]
    k_nyq = hf[:, 1, 0] + hb[:, 1, 0]
    first = (jnp.arange(l) == 0)[None, :, None]
    scale = jnp.where(first, 1.0 / (2 * l), 2.0 / (2 * l)).astype(F32)
    p = k_re * scale
    q = jnp.where(first, 0.0, k_im) * scale
    r = jnp.where(first, k_nyq[:, None, :], k_re) * scale
    return jnp.stack([p, q, r], axis=1)


FFT_N2 = 128
FFT_TC = 8192
FFT_KC = 8


def _fft_consts(l):
    n, n2 = 2 * l, FFT_N2
    n1 = n // n2
    h = n1 // 2
    cat = jnp.concatenate
    ia = jnp.arange(n1, dtype=jnp.int32)
    ang = ((ia[:, None] * ia[None, :]) % n1).astype(F32) * (2.0 * math.pi / n1)
    fr, fi = jnp.cos(ang), -jnp.sin(ang)
    mat_data = cat([cat([fr[:, :h], -fi[:, :h]], 1), cat([fi[:, :h], fr[:, :h]], 1)], 0)
    mat_filt = cat([fr, fi], 0)
    gr, gi = fr[:, :h].T, -fi[:, :h].T
    mat_inv = jnp.stack([cat([gr, -gi], 1), cat([gi, gr], 1)])
    k = ia[:, None, None] + n1 * jnp.arange(n2, dtype=jnp.int32)[None, :, None]
    nn = jnp.arange(n2, dtype=jnp.int32)[None, None, :]
    angb = ((k * nn) % n).astype(F32) * (2.0 * math.pi / n)
    er, ei = jnp.cos(angb), -jnp.sin(angb)
    mb = cat([cat([er, -ei], 2), cat([ei, er], 2)], 1)
    bf = lambda a: a.astype(BF16)
    return dict(n1=n1, h=h, mat_data=bf(mat_data), mat_filt=bf(mat_filt), mat_inv=bf(mat_inv),
                mb=bf(mb), mib=bf(jnp.transpose(mb, (0, 2, 1))))


def _fft_a_kernel(m_ref, a_ref, b_ref, o_ref):
    h = a_ref.shape[0]
    acc = jnp.dot(m_ref[:, :h], a_ref[...], preferred_element_type=F32)
    acc = acc + jnp.dot(m_ref[:, h:], b_ref[...], preferred_element_type=F32)
    o_ref[...] = acc.astype(o_ref.dtype)


def _fft_stage_a(mat, x3, npairs, stride, offset):
    _, h, cols = x3.shape
    tc = min(FFT_TC, cols)
    return pl.pallas_call(
        _fft_a_kernel,
        grid=(npairs, cols // tc),
        in_specs=[pl.BlockSpec(mat.shape, lambda p, j: (0, 0)),
                  pl.BlockSpec((None, h, tc), lambda p, j: (p * stride, 0, j)),
                  pl.BlockSpec((None, h, tc), lambda p, j: (p * stride + offset, 0, j))],
        out_specs=pl.BlockSpec((None, mat.shape[0], tc), lambda p, j: (p, 0, j)),
        out_shape=jax.ShapeDtypeStruct((npairs, mat.shape[0], cols), BF16),
        compiler_params=_cp("parallel", "parallel"),
        name="fft_a",
    )(mat, x3, x3)


def _fft_b_kernel(m_ref, ar_ref, ai_ref, o_ref):
    for kk in range(m_ref.shape[0]):
        a = jnp.concatenate([ar_ref[kk], ai_ref[kk]], axis=0)
        o_ref[kk] = jnp.dot(m_ref[kk], a, preferred_element_type=F32)


def _fft_bb_kernel(m_ref, mi_ref, ar_ref, ai_ref, k_ref, o_ref):
    n2 = ar_ref.shape[1]
    for kk in range(m_ref.shape[0]):
        a = jnp.concatenate([ar_ref[kk], ai_ref[kk]], axis=0)
        x = jnp.dot(m_ref[kk], a, preferred_element_type=F32)
        xr, xi = x[:n2], x[n2:]
        kr, ki = k_ref[kk, :n2], k_ref[kk, n2:]
        y = jnp.concatenate([(xr * kr - xi * ki).astype(BF16), (xr * ki + xi * kr).astype(BF16)], axis=0)
        b = jnp.dot(mi_ref[kk], y, preferred_element_type=F32)
        o_ref[0, kk] = b[:n2].astype(o_ref.dtype)
        o_ref[1, kk] = b[n2:].astype(o_ref.dtype)


def _fft_stage_b(fc, a, kspec, db):
    mb, mib = fc["mb"], fc["mib"]
    npairs = a.shape[0]
    n1, r2, _ = mb.shape
    n2 = r2 // 2
    kc = min(FFT_KC, n1)
    a5 = a.reshape(npairs, 2, n1, n2, db)
    mspec = pl.BlockSpec((kc, r2, r2), lambda c, p: (c, 0, 0))
    aspec = lambda part: pl.BlockSpec((None, None, kc, n2, db), lambda c, p: (p, part, c, 0, 0))
    if kspec is None:
        kern, args, in_specs = _fft_b_kernel, [mb, a5, a5], [mspec, aspec(0), aspec(1)]
        out_spec = pl.BlockSpec((None, kc, r2, db), lambda c, p: (p, c, 0, 0))
        out_shape = jax.ShapeDtypeStruct((npairs, n1, r2, db), F32)
    else:
        kern, args = _fft_bb_kernel, [mb, mib, a5, a5, kspec]
        in_specs = [mspec, mspec, aspec(0), aspec(1), pl.BlockSpec((kc, r2, db), lambda c, p: (c, 0, 0))]
        out_spec = pl.BlockSpec((None, 2, kc, n2, db), lambda c, p: (p, 0, c, 0, 0))
        out_shape = jax.ShapeDtypeStruct((npairs, 2, n1, n2, db), BF16)
    return pl.pallas_call(
        kern,
        grid=(n1 // kc, npairs),
        in_specs=in_specs,
        out_specs=out_spec,
        out_shape=out_shape,
        compiler_params=_cp("parallel", "parallel"),
        name="fft_b",
    )(*args)


def _ifft_a_kernel(m_ref, b_ref, zp_ref, gate_ref, bias_ref, *rest):
    o_ref = rest[-1]
    y = jnp.dot(m_ref[...], b_ref[...], preferred_element_type=F32)
    o_ref[...] = (gate_ref[...] * (y + bias_ref[...] * zp_ref[...].astype(F32))).astype(o_ref.dtype)


def _ifft_stage_a(mat_inv, b5, zprev, gate, bias, nb, db, out_init, out_rows):
    npairs = b5.shape[0]
    _, h, r1 = mat_inv.shape
    cols = b5.shape[3] * db
    tc = min(FFT_TC, cols)
    b3 = b5.reshape(npairs, r1, cols)
    view = lambda a: a.reshape(a.shape[0] * db // cols, cols)
    blk = pl.BlockSpec((h, tc), lambda b, j: (b, j))
    bias_t = jnp.tile(bias, (1, tc // db))
    args = [mat_inv, b3, view(zprev), view(gate), bias_t]
    in_specs = [pl.BlockSpec((None, h, r1), lambda b, j: (b // npairs, 0, 0)),
                pl.BlockSpec((None, r1, tc), lambda b, j: (b % npairs, 0, j)),
                blk, blk, pl.BlockSpec((1, tc), lambda b, j: (0, 0))]
    aliases = {}
    if out_init is not None:
        args.append(view(out_init))
        in_specs.append(pl.BlockSpec(memory_space=pl.ANY))
        aliases = {5: 0}
    out = pl.pallas_call(
        _ifft_a_kernel,
        grid=(nb, cols // tc),
        in_specs=in_specs,
        out_specs=blk,
        out_shape=jax.ShapeDtypeStruct((out_rows * db // cols, cols), BF16),
        input_output_aliases=aliases,
        compiler_params=_cp("parallel", "parallel"),
        name="ifft_a",
    )(*args)
    return out.reshape(out_rows, db)


def _fft_filter_spectra(fc, filt, l, db):
    n_ord = filt.shape[1]
    h_fwd = jnp.transpose(filt[:, :, 0, :], (1, 0, 2))
    h_bwd = jnp.transpose(filt[:, :, 1, :], (1, 0, 2))
    ker = jnp.concatenate([h_fwd, jnp.zeros((n_ord, 1, db), F32), h_bwd[:, :0:-1, :]], axis=1)
    x3 = ker.astype(BF16).reshape(n_ord * 2, fc["h"], FFT_N2 * db)
    a = _fft_stage_a(fc["mat_filt"], x3, n_ord, 2, 1)
    return _fft_stage_b(fc, a, None, db) * (1.0 / (2 * l))


def _fft_conv(fc, z_bf, kspec, zprev, gate, bias, nb, db, out_init, out_rows):
    h = fc["h"]
    x3 = z_bf.reshape(nb, h, FFT_N2 * db)
    a = _fft_stage_a(fc["mat_data"], x3, nb // 2, 1, nb // 2)
    b5 = _fft_stage_b(fc, a, kspec, db)
    return _ifft_stage_a(fc["mat_inv"], b5, zprev, gate, bias, nb, db, out_init, out_rows)


def _lru_kernel(*refs, rw, reverse, final):
    if final:
        (x_ref, w_ref, wa_ref, wx_ref, ba_ref, bx_ref, lam_ref, h0_ref, hf_ref, ug_ref,
         _, o_ref, st_ref, a_s, b_s, carry) = refs
    else:
        (x_ref, w_ref, wa_ref, wx_ref, ba_ref, bx_ref, lam_ref, h0_ref,
         o_ref, st_ref, a_s, b_s, carry) = refs
    tl = x_ref.shape[0]

    @pl.when(pl.program_id(1) == 0)
    def _():
        carry[...] = h0_ref[...]

    xc = _dwconv(x_ref[...], w_ref, w_ref.shape[0] // 2, rw)
    xb = xc.astype(BF16)
    r = _sigmoid(jnp.dot(xb, wa_ref[...], preferred_element_type=F32) + ba_ref[...])
    ig = _sigmoid(jnp.dot(xb, wx_ref[...], preferred_element_type=F32) + bx_ref[...])
    lam = lam_ref[...]
    softplus = jnp.maximum(-lam, 0.0) + jnp.log(1.0 + jnp.exp(-jnp.abs(lam)))
    log_a = -LRU_C * r * softplus
    a = jnp.exp(log_a)
    bv = jnp.sqrt(1.0 - jnp.exp(2.0 * log_a)) * (ig * xc)

    pos8 = lax.broadcasted_iota(jnp.int32, a.shape, 0) % 8
    for s in (1, 2, 4):
        if reverse:
            a_sh, b_sh, ok = pltpu.roll(a, tl - s, axis=0), pltpu.roll(bv, tl - s, axis=0), pos8 < 8 - s
        else:
            a_sh, b_sh, ok = pltpu.roll(a, s, axis=0), pltpu.roll(bv, s, axis=0), pos8 >= s
        bv = jnp.where(ok, a * b_sh + bv, bv)
        a = jnp.where(ok, a * a_sh, a)
    a_s[...] = a
    b_s[...] = bv

    ng = tl // 8

    def body(gi, c):
        g = ng - 1 - gi if reverse else gi
        sl = pl.ds(pl.multiple_of(g * 8, 8), 8)
        h8 = b_s[sl, :] + a_s[sl, :] * c
        b_s[sl, :] = h8
        return h8[0:1, :] if reverse else h8[7:8, :]

    c_out = lax.fori_loop(0, ng, body, carry[...])
    carry[...] = c_out
    st_ref[...] = c_out
    if final:
        ug = ug_ref[...]
        gelu = 0.5 * ug * (1.0 + jnp.tanh(math.sqrt(2.0 / math.pi) * (ug + 0.044715 * ug * ug * ug)))
        o_ref[...] = ((hf_ref[...] + b_s[...]) * gelu).astype(BF16)
    else:
        o_ref[...] = b_s[...]


def _lru_pass(u_all, w_lconv, wa, wx, ba, bx, lam, h0, row0, nb, l, tl, rw, db, reverse, hf=None, out_init=None,
              t_all=None):
    final = hf is not None
    nc = l // tl
    ob = row0 // tl
    cidx = (lambda c: nc - 1 - c) if reverse else (lambda c: c)
    full = lambda a: pl.BlockSpec(a.shape, lambda b, c: (0,) * a.ndim)
    ncolx, ncolg = 7, 8
    args = [u_all, w_lconv, wa, wx, ba, bx, lam, h0]
    in_specs = [pl.BlockSpec((tl, db), lambda b, c: (ob + b * nc + cidx(c), ncolx)),
                full(w_lconv), full(wa), full(wx), full(ba), full(bx), full(lam),
                pl.BlockSpec((None, 1, db), lambda b, c: (b, 0, 0))]
    aliases = {}
    if final:
        args += [hf, u_all]
        in_specs += [pl.BlockSpec((tl, db), lambda b, c: (b * nc + cidx(c), 0)),
                     pl.BlockSpec((tl, db), lambda b, c: (ob + b * nc + cidx(c), ncolg))]
        if out_init is None:
            out_init = jnp.zeros((8, 128), BF16)
        else:
            aliases = {10: 0}
        args.append(out_init)
        in_specs.append(pl.BlockSpec(memory_space=pl.ANY))
        out_spec = pl.BlockSpec((tl, db), lambda b, c: (ob + b * nc + cidx(c), 0))
        out_shape = jax.ShapeDtypeStruct((t_all, db), BF16)
    else:
        out_spec = pl.BlockSpec((tl, db), lambda b, c: (b * nc + cidx(c), 0))
        out_shape = jax.ShapeDtypeStruct((nb * l, db), F32)
    return pl.pallas_call(
        functools.partial(_lru_kernel, rw=rw, reverse=reverse, final=final),
        grid=(nb, nc),
        in_specs=in_specs,
        out_specs=[out_spec, pl.BlockSpec((None, 1, db), lambda b, c: (b, 0, 0))],
        out_shape=[out_shape, jax.ShapeDtypeStruct((nb, 1, db), F32)],
        scratch_shapes=[pltpu.VMEM((tl, db), F32), pltpu.VMEM((tl, db), F32), pltpu.VMEM((1, db), F32)],
        input_output_aliases=aliases,
        compiler_params=_cp("arbitrary", "arbitrary"),
        name="lru_bwd" if reverse else "lru_fwd",
    )(*args)


def _block_diag(w):
    h, hd, _ = w.shape
    eye = jnp.eye(h, dtype=w.dtype)
    return (eye[:, None, :, None] * w[:, :, None, :]).reshape(h * hd, h * hd).astype(BF16)


def _merge_kernel(x_ref, sh_ref, sc_ref, g_ref, b0_ref, b1_ref, b2_ref, b3_ref, wm_ref, wb_ref, wo_ref, ga_ref,
                  o_ref, hbf):
    j = pl.program_id(1)

    @pl.when(j == 0)
    def _():
        _norm_mod_store(hbf, x_ref, g_ref, sh_ref, sc_ref)

    h = hbf[...]
    m = None
    for k, br in enumerate((b0_ref, b1_ref, b2_ref, b3_ref)):
        g = jnp.dot(h, wm_ref[k], preferred_element_type=F32)
        p = jnp.dot(br[...], wb_ref[k], preferred_element_type=F32)
        term = _sigmoid(g) * p
        m = term if m is None else m + term
    part = jnp.dot(m.astype(BF16), wo_ref[...], preferred_element_type=F32)

    @pl.when(j == 0)
    def _():
        o_ref[...] = part

    @pl.when(j > 0)
    def _():
        o_ref[...] += part

    @pl.when(j == pl.num_programs(1) - 1)
    def _():
        o_ref[...] = x_ref[...] + ga_ref[...] * o_ref[...]


def _merge(branches, wm_bf, wb_bf, wo_bf, x_all, mod3, g, ntiles, nlt, tpb, nb):
    t, d = x_all.shape
    db = wb_bf.shape[1]
    tn = min(TN_MERGE, d)
    row = lambda w: pl.BlockSpec((TM, w), lambda i, j: (i, 0))
    return pl.pallas_call(
        _merge_kernel,
        grid=(ntiles, d // tn),
        in_specs=[row(d), _mod_spec(d, 0, nlt, tpb, nb), _mod_spec(d, 1, nlt, tpb, nb),
                  pl.BlockSpec((1, d), lambda i, j: (0, 0)),
                  row(db), row(db), row(db), row(db),
                  pl.BlockSpec((4, d, tn), lambda i, j: (0, 0, j)),
                  pl.BlockSpec((4, db, tn), lambda i, j: (0, 0, j)),
                  pl.BlockSpec((tn, d), lambda i, j: (j, 0)),
                  _mod_spec(d, 2, nlt, tpb, nb)],
        out_specs=row(d),
        out_shape=jax.ShapeDtypeStruct((ntiles * TM, d), F32),
        scratch_shapes=[pltpu.VMEM((TM, d), BF16)],
        compiler_params=_cp("parallel", "arbitrary"),
        name="merge",
    )(x_all, mod3, mod3, g.reshape(1, d), *branches, wm_bf, wb_bf, wo_bf, mod3)


def _router_kernel(x_ref, sh_ref, sc_ref, g_ref, wr_ref, br_ref, h_ref, r_ref, c_ref, *, ngroups, epg):
    h = _norm_mod(x_ref[...], g_ref[...], sh_ref[...], sc_ref[...])
    h_ref[...] = h
    logits = jnp.dot(h, wr_ref[...], preferred_element_type=F32, precision=HIGHEST) + br_ref[...]
    lane = lax.broadcasted_iota(jnp.int32, logits.shape, 1)
    lane_f = lane.astype(F32)
    neg = jnp.float32(-1e30)
    big = jnp.float32(1e6)
    is_g = lane < ngroups
    lg = jnp.where(is_g, logits, neg)
    mx = jnp.max(lg, axis=-1, keepdims=True)
    g_top = jnp.min(jnp.where(lg == mx, lane_f, big), axis=-1, keepdims=True)
    den = jnp.sum(jnp.where(is_g, jnp.exp(lg - mx), 0.0), axis=-1, keepdims=True)
    p_top = 1.0 / den
    el = lane_f - ngroups
    in_grp = (el >= g_top * epg) & (el < (g_top + 1.0) * epg)
    le = jnp.where(in_grp, logits, neg)
    v1 = jnp.max(le, axis=-1, keepdims=True)
    e1 = jnp.min(jnp.where(le == v1, el, big), axis=-1, keepdims=True)
    le2 = jnp.where(el == e1, neg, le)
    v2 = jnp.max(le2, axis=-1, keepdims=True)
    e2 = jnp.min(jnp.where(le2 == v2, el, big), axis=-1, keepdims=True)
    dlt = jnp.exp(v2 - v1)
    w1 = p_top / (1.0 + dlt)
    w2 = p_top * dlt / (1.0 + dlt)
    tm = logits.shape[0]
    pick1 = el == e1
    pick2 = el == e2
    onehot = jnp.where(pick1 | pick2, 1.0, 0.0)
    row = lax.broadcasted_iota(jnp.int32, (tm, tm), 0)
    col = lax.broadcasted_iota(jnp.int32, (tm, tm), 1)
    earlier = jnp.where(col < row, 1.0, 0.0).astype(BF16)
    before = jnp.dot(earlier, onehot.astype(BF16), preferred_element_type=F32)
    rank1 = jnp.sum(jnp.where(pick1, before, 0.0), axis=-1, keepdims=True)
    rank2 = jnp.sum(jnp.where(pick2, before, 0.0), axis=-1, keepdims=True)
    out = jnp.where(lane == 0, e1, 0.0)
    out = jnp.where(lane == 1, e2, out)
    out = jnp.where(lane == 2, w1, out)
    out = jnp.where(lane == 3, w2, out)
    out = jnp.where(lane == 4, rank1, out)
    out = jnp.where(lane == 5, rank2, out)
    r_ref[...] = out
    c_ref[...] = jnp.broadcast_to(jnp.sum(onehot, axis=0, keepdims=True), c_ref.shape)


def _router(x_all, mod3, g, w_router, b_router, ntiles, nlt, tpb, nb, ngroups, epg):
    t, d = x_all.shape
    return pl.pallas_call(
        functools.partial(_router_kernel, ngroups=ngroups, epg=epg),
        grid=(ntiles,),
        in_specs=[pl.BlockSpec((TM, d), lambda i: (i, 0)),
                  _mod_spec(d, 3, nlt, tpb, nb), _mod_spec(d, 4, nlt, tpb, nb),
                  pl.BlockSpec((1, d), lambda i: (0, 0)),
                  pl.BlockSpec((d, 128), lambda i: (0, 0)),
                  pl.BlockSpec((1, 128), lambda i: (0, 0))],
        out_specs=[pl.BlockSpec((TM, d), lambda i: (i, 0)), pl.BlockSpec((TM, 128), lambda i: (i, 0)),
                   pl.BlockSpec((None, 8, 128), lambda i: (i, 0, 0))],
        out_shape=[jax.ShapeDtypeStruct((ntiles * TM, d), F32), jax.ShapeDtypeStruct((ntiles * TM, 128), F32),
                   jax.ShapeDtypeStruct((ntiles, 8, 128), F32)],
        compiler_params=_cp("parallel"),
        name="router",
    )(x_all, mod3, mod3, g.reshape(1, d), w_router, b_router)


def _route_tables(slab, tile_counts, ngroups, n_experts, tm):
    ntiles = tile_counts.shape[0]
    e = slab[:, 0:TOP_K].astype(jnp.int32).reshape(ntiles, TM, TOP_K)
    rank = slab[:, 4:4 + TOP_K].astype(jnp.int32).reshape(ntiles, TM, TOP_K)
    counts_t = tile_counts[:, 0, ngroups:ngroups + n_experts].astype(jnp.int32)
    base = jnp.cumsum(counts_t, axis=0) - counts_t
    counts = jnp.sum(counts_t, axis=0)
    tiles_per_e = (counts + tm - 1) // tm
    tile_end = jnp.cumsum(tiles_per_e)
    starts = (tile_end - tiles_per_e) * tm
    off = starts[None, :] + base
    sel = e[..., None] == jnp.arange(n_experts, dtype=jnp.int32)
    slot = jnp.sum(jnp.where(sel, off[:, None, None, :], 0), axis=-1) + rank
    n_slot_tiles = (ntiles * TM * TOP_K) // tm + n_experts
    n_valid = tile_end[-1]
    tile_ids = jnp.minimum(jnp.arange(n_slot_tiles, dtype=jnp.int32), n_valid - 1)
    tile_expert = jnp.sum((tile_end[None, :] <= tile_ids[:, None]).astype(jnp.int32), axis=1)
    return slot, tile_expert, n_valid.reshape(1).astype(jnp.int32), n_slot_tiles


ROW_DMA_UNROLL = 8


def _stage_indices(idx_hbm_row, idx_smem, sem_i):
    cp = pltpu.make_async_copy(idx_hbm_row, idx_smem, sem_i)
    cp.start()
    cp.wait()


def _dispatch_kernel(p_hbm, h_ref, xs_in, xs_hbm, idx, sem_i, sem_x):
    del xs_in
    i = pl.program_id(0)
    tm = h_ref.shape[0]
    _stage_indices(p_hbm.at[i], idx, sem_i)

    def issue(r, c):
        src = h_ref.at[pl.ds(r, 1), :]
        for k in range(TOP_K):
            pltpu.make_async_copy(src, xs_hbm.at[pl.ds(idx[k * tm + r], 1), :], sem_x).start()
        return c

    lax.fori_loop(0, tm, issue, 0, unroll=ROW_DMA_UNROLL)
    for k in range(TOP_K):
        pltpu.make_async_copy(h_ref, xs_hbm.at[pl.ds(0, tm), :], sem_x).wait()


def _dispatch(h_all, slot, n_slot_rows):
    ntiles = slot.shape[0]
    d = h_all.shape[1]
    p = jnp.transpose(slot, (0, 2, 1)).reshape(ntiles, TOP_K * TM)
    xs0 = jnp.zeros((n_slot_rows, d), F32)
    return pl.pallas_call(
        _dispatch_kernel,
        grid=(ntiles,),
        in_specs=[pl.BlockSpec(memory_space=pl.ANY), pl.BlockSpec((TM, d), lambda i: (i, 0)),
                  pl.BlockSpec(memory_space=pl.ANY)],
        out_specs=pl.BlockSpec(memory_space=pl.ANY),
        out_shape=jax.ShapeDtypeStruct((n_slot_rows, d), F32),
        scratch_shapes=[pltpu.SMEM((TOP_K * TM,), jnp.int32),
                        pltpu.SemaphoreType.DMA(()), pltpu.SemaphoreType.DMA(())],
        input_output_aliases={2: 0},
        compiler_params=_cp("arbitrary"),
        name="dispatch",
    )(p, h_all, xs0)


def _cast_kernel(w_ref, o_ref):
    o_ref[...] = w_ref[...].astype(o_ref.dtype)


def _layer_bf16(w_stack, l):
    _, e, a, b = w_stack.shape
    return pl.pallas_call(
        _cast_kernel,
        grid=(e,),
        in_specs=[pl.BlockSpec((None, None, a, b), lambda i: (l, i, 0, 0))],
        out_specs=pl.BlockSpec((None, a, b), lambda i: (i, 0, 0)),
        out_shape=jax.ShapeDtypeStruct((e, a, b), BF16),
        compiler_params=_cp("parallel"),
        name="cast_bf16",
    )(w_stack)


def _moe_kernel(te_ref, nv_ref, x_ref, wg_ref, wu_ref, wd_ref, o_ref):
    del te_ref
    valid = pl.program_id(0) < nv_ref[0]

    @pl.when(valid)
    def _():
        x = x_ref[...].astype(BF16)
        g = jnp.dot(x, wg_ref[...], preferred_element_type=F32)
        u = jnp.dot(x, wu_ref[...], preferred_element_type=F32)
        hid = (g * _sigmoid(g) * u).astype(BF16)
        o_ref[...] = jnp.dot(hid, wd_ref[...], preferred_element_type=F32)

    @pl.when(jnp.logical_not(valid))
    def _():
        o_ref[...] = jnp.zeros_like(o_ref)


def _moe_experts(xs, tile_expert, n_valid, w_gate, w_up, w_down, tm):
    n_rows, d = xs.shape
    f = w_gate.shape[2]
    xi = lambda i, te, nv: (jnp.maximum(jnp.minimum(i, nv[0] - 1), 0), 0)
    grid_spec = pltpu.PrefetchScalarGridSpec(
        num_scalar_prefetch=2,
        grid=(n_rows // tm,),
        in_specs=[pl.BlockSpec((tm, d), xi),
                  pl.BlockSpec((None, d, f), lambda i, te, nv: (te[i], 0, 0)),
                  pl.BlockSpec((None, d, f), lambda i, te, nv: (te[i], 0, 0)),
                  pl.BlockSpec((None, f, d), lambda i, te, nv: (te[i], 0, 0))],
        out_specs=pl.BlockSpec((tm, d), lambda i, te, nv: (i, 0)),
    )
    return pl.pallas_call(
        _moe_kernel,
        grid_spec=grid_spec,
        out_shape=jax.ShapeDtypeStruct((n_rows, d), F32),
        compiler_params=_cp("arbitrary"),
        name="moe_experts",
    )(tile_expert, n_valid, xs, w_gate, w_up, w_down)


def _combine_kernel(p_hbm, ys_hbm, x_ref, slab_ref, ga_ref, g_ref, o_ref, idx, buf, sem_i, sem_x, *, final):
    i = pl.program_id(0)
    n = pl.num_programs(0)
    rows = idx.shape[0]
    tm = rows // TOP_K

    def fetch(tile, s):
        _stage_indices(p_hbm.at[tile], idx, sem_i)

        def issue(r, c):
            pltpu.make_async_copy(ys_hbm.at[pl.ds(idx[r], 1), :], buf.at[s, pl.ds(r, 1), :], sem_x.at[s]).start()
            return c

        lax.fori_loop(0, rows, issue, 0, unroll=ROW_DMA_UNROLL)

    @pl.when(i == 0)
    def _():
        fetch(0, 0)

    @pl.when(i + 1 < n)
    def _():
        fetch(i + 1, (i + 1) % 2)

    s = i % 2
    pltpu.make_async_copy(ys_hbm.at[pl.ds(0, rows), :], buf.at[s], sem_x.at[s]).wait()
    slab = slab_ref[...]
    y = slab[:, TOP_K:TOP_K + 1] * buf[s, 0:tm, :]
    for k in range(1, TOP_K):
        y = y + slab[:, TOP_K + k:TOP_K + k + 1] * buf[s, k * tm:(k + 1) * tm, :]
    x = x_ref[...] + ga_ref[...] * y
    if final:
        x = x * lax.rsqrt(jnp.mean(x * x, axis=-1, keepdims=True) + NORM_EPS) * g_ref[...]
    o_ref[...] = x


def _combine(ys, slot, slab, x_all, mod3, g_final, ntiles, nlt, tpb, nb, final):
    d = x_all.shape[1]
    tm = TM_COMB
    f = TM // tm
    n = ntiles * f
    p = jnp.transpose(slot.reshape(n, tm, TOP_K), (0, 2, 1)).reshape(n, TOP_K * tm)
    return pl.pallas_call(
        functools.partial(_combine_kernel, final=final),
        grid=(n,),
        in_specs=[pl.BlockSpec(memory_space=pl.ANY), pl.BlockSpec(memory_space=pl.ANY),
                  pl.BlockSpec((tm, d), lambda i: (i, 0)),
                  pl.BlockSpec((tm, 128), lambda i: (i, 0)),
                  pl.BlockSpec((None, 1, d), lambda i: (jnp.where(i < nlt * f, i // (tpb * f), nb), 0, 5)),
                  pl.BlockSpec((1, d), lambda i: (0, 0))],
        out_specs=pl.BlockSpec((tm, d), lambda i: (i, 0)),
        out_shape=jax.ShapeDtypeStruct((n * tm, d), F32),
        scratch_shapes=[pltpu.SMEM((TOP_K * tm,), jnp.int32),
                        pltpu.VMEM((2, TOP_K * tm, d), F32),
                        pltpu.SemaphoreType.DMA(()), pltpu.SemaphoreType.DMA((2,))],
        compiler_params=_cp("arbitrary"),
        name="combine",
    )(p, ys, x_all, slab, mod3, g_final.reshape(1, d))


def kernel(x, c, ctx, c_ctx, w_mod, b_mod, g_norm1, g_norm2, g_final, w_in, w_merge, w_branch, w_out, w_hconv,
           hy_bias, hf_w1, hf_b1, hf_w2, hf_b2, hf_w3, hf_freq, w_sconv, w_lconv, lru_wa, lru_ba, lru_wx, lru_bx,
           lru_lambda, w_rg, b_rg, w_re, b_re, w_e_gate, w_e_up, w_e_down):
    nb, seq, d = x.shape
    cl = ctx.shape[1]
    depth = w_in.shape[0]
    db = w_branch.shape[2]
    gdim = db // FNET_GROUPS
    ngroups = w_rg.shape[2]
    epg = w_re.shape[3]
    n_experts = ngroups * epg
    t_lat, t_ctx = nb * seq, nb * cl
    t_all = t_lat + t_ctx
    nlt, tpb = t_lat // TM, seq // TM
    nat = t_all // TM
    tl_lat, tl_ctx = min(TL, seq), min(TL, cl)

    x_all = jnp.concatenate([x.reshape(t_lat, d), ctx.reshape(t_ctx, d)], axis=0)
    nrows = -(-(nb + 1) // 8) * 8
    cvec = jnp.zeros((nrows, d), F32).at[:nb].set(c).at[nb].set(c_ctx)
    mods = _adaln(cvec, w_mod, b_mod)

    use_fft = nb % 2 == 0 and seq % (8 * FFT_N2) == 0
    if use_fft:
        hyfft = _fft_consts(seq)
    else:
        c_lat, sf_lat, si_lat = _dft_mats(seq)
        fwd_lat, inv_lat = (c_lat, sf_lat), (c_lat, si_lat)
    c_ctx, sf_ctx, si_ctx = _dft_mats(cl)
    fwd_ctx, inv_ctx = (c_ctx, sf_ctx), (c_ctx, si_ctx)
    fc_lat, fs_lat, w_pq = _fnet_mats(seq, gdim, FNET_GROUPS)
    fc_ctx, fs_ctx, _ = _fnet_mats(cl, gdim, FNET_GROUPS)

    out = None
    for l in range(depth):
        last = l == depth - 1
        mod3 = mods[l].reshape(nrows, 1, 6 * d)
        u_all = _inproj(x_all, mod3, g_norm1[l], w_in[l].astype(BF16), nlt, tpb, nb)

        wa = [_block_diag(lru_wa[l, dd]) for dd in range(2)]
        wx = [_block_diag(lru_wx[l, dd]) for dd in range(2)]
        ba = [lru_ba[l, dd].reshape(1, db) for dd in range(2)]
        bx = [lru_bx[l, dd].reshape(1, db) for dd in range(2)]
        lam = [lru_lambda[l, dd].reshape(1, db) for dd in range(2)]
        zeros_st = jnp.zeros((nb, 1, db), F32)
        rows_out = t_lat if last else t_all

        def lru(row0, length, tl, rw, h0f, h0b, combine, out_init):
            hf, stf = _lru_pass(u_all, w_lconv[l], wa[0], wx[0], ba[0], bx[0], lam[0], h0f,
                                row0, nb, length, tl, rw, db, False)
            if combine:
                y, stb = _lru_pass(u_all, w_lconv[l], wa[1], wx[1], ba[1], bx[1], lam[1], h0b,
                                   row0, nb, length, tl, rw, db, True, hf=hf, out_init=out_init, t_all=rows_out)
            else:
                y, stb = _lru_pass(u_all, w_lconv[l], wa[1], wx[1], ba[1], bx[1], lam[1], h0b,
                                   row0, nb, length, tl, rw, db, True)
            return y, stf, stb

        bias0, bias1 = hy_bias[l, 0].reshape(1, db), hy_bias[l, 1].reshape(1, db)

        def hyena_dense(row0, length, fwd, inv, v, x1, x2, v_bf, out_init):
            filt = _hyena_filters(length, hf_w1[l], hf_b1[l], hf_w2[l], hf_b2[l], hf_w3[l], hf_freq[l], db)
            pqr = _filter_spectra(fwd, filt, length, db)
            yhat = _hy_fwd(fwd, v_bf, nb, length, db, pqr[0])
            z2 = _hy_inv(inv, yhat, v, 0, x1, 0, bias0, nb, length, db, 0, nb * length, None)
            yhat = _hy_fwd(fwd, z2, nb, length, db, pqr[1])
            return _hy_inv(inv, yhat, z2, 0, x2, 0, bias1, nb, length, db, row0, rows_out, out_init)

        def hyena_fft(length, v, x1, x2, v_bf, out_init):
            filt = _hyena_filters(length, hf_w1[l], hf_b1[l], hf_w2[l], hf_b2[l], hf_w3[l], hf_freq[l], db)
            kspec = _fft_filter_spectra(hyfft, filt, length, db)
            z2 = _fft_conv(hyfft, v_bf, kspec[0], v, x1, bias0, nb, db, None, nb * length)
            return _fft_conv(hyfft, z2, kspec[1], z2, x2, bias1, nb, db, out_init, rows_out)

        zbuf = lambda: None if last else jnp.zeros((rows_out, db), BF16)
        if last:
            _, st_f, st_b = lru(t_lat, cl, tl_ctx, cl, zeros_st, zeros_st, False, None)
        else:
            y_lru_c, st_f, st_b = lru(t_lat, cl, tl_ctx, cl, zeros_st, zeros_st, True, zbuf())
        y_lru, _, _ = lru(0, seq, tl_lat, GRID_W, st_f, st_b, True, None if last else y_lru_c)

        pq = _fnet_pq(u_all, w_pq, db)
        v, x1, x2, v_bf, y_sc = _conv_stage(u_all, w_hconv[l], w_sconv[l], 0, t_lat, tl_lat, GRID_W, db,
                                            jnp.zeros((rows_out, db), BF16))
        y_fn = _fnet_dft(fc_lat, fs_lat, pq, 0, nb, seq, db, gdim, zbuf(), rows_out)
        if not last:
            vc, x1c, x2c, vc_bf, y_sc = _conv_stage(u_all, w_hconv[l], w_sconv[l], t_lat, t_ctx, tl_ctx, cl, db,
                                                    y_sc)
            y_fn = _fnet_dft(fc_ctx, fs_ctx, pq, t_lat, nb, cl, db, gdim, y_fn, rows_out)
            y_hy = hyena_dense(t_lat, cl, fwd_ctx, inv_ctx, vc, x1c, x2c, vc_bf, zbuf())
        else:
            y_hy = None
        if use_fft:
            y_hy = hyena_fft(seq, v, x1, x2, v_bf, y_hy)
        else:
            y_hy = hyena_dense(0, seq, fwd_lat, inv_lat, v, x1, x2, v_bf, y_hy)

        ntiles = nlt if last else nat
        x_mid = _merge((y_fn, y_hy, y_sc, y_lru), w_merge[l].astype(BF16), w_branch[l].astype(BF16),
                       w_out[l].astype(BF16), x_all, mod3, g_norm1[l], ntiles, nlt, tpb, nb)

        w_router = jnp.zeros((d, 128), F32).at[:, :ngroups].set(w_rg[l])
        w_router = w_router.at[:, ngroups:ngroups + n_experts].set(
            jnp.transpose(w_re[l], (1, 0, 2)).reshape(d, n_experts))
        b_router = jnp.zeros((1, 128), F32).at[0, :ngroups].set(b_rg[l])
        b_router = b_router.at[0, ngroups:ngroups + n_experts].set(b_re[l].reshape(-1))
        h2, slab, tile_counts = _router(x_mid, mod3, g_norm2[l], w_router, b_router, ntiles, nlt, tpb, nb,
                                        ngroups, epg)
        slot, tile_expert, n_valid, n_slot_tiles = _route_tables(slab, tile_counts, ngroups, n_experts, TM_MOE)
        xs = _dispatch(h2, slot, n_slot_tiles * TM_MOE)
        ys = _moe_experts(xs, tile_expert, n_valid, _layer_bf16(w_e_gate, l), _layer_bf16(w_e_up, l),
                          _layer_bf16(w_e_down, l), TM_MOE)
        x_all = _combine(ys, slot, slab, x_mid, mod3, g_final, ntiles, nlt, tpb, nb, last)
        if last:
            out = x_all.reshape(nb, seq, d)
    return out
```

```python
import functools
import math

import jax
import jax.numpy as jnp
from jax import lax
from jax.experimental import pallas as pl
from jax.experimental.pallas import tpu as pltpu

F32 = jnp.float32
BF16 = jnp.bfloat16
HIGHEST = lax.Precision.HIGHEST

NORM_EPS = 1e-6
GRID_W = 64
FNET_GROUPS = 4
LRU_C = 8.0
HY_DECAY_TARGET = 1e-2
HY_FAST_PCT = 0.3
HY_SLOW_PCT = 1.5
TOP_K = 2

TM = 512
TL = 512
TR = 512
TM_MOE = 512
TM_COMB = 256
TN_IN = 1536
TN_MERGE = 512
TF_MOE = 512
VMEM_LIMIT = 56 * 1024 * 1024


def _cp(*sem):
    return pltpu.CompilerParams(dimension_semantics=sem, vmem_limit_bytes=VMEM_LIMIT)


def _sigmoid(x):
    return 1.0 / (1.0 + jnp.exp(-x))


def _adaln_kernel(c_ref, w_ref, b_ref, o_ref):
    c = c_ref[...]
    s = c * _sigmoid(c)
    o_ref[...] = jnp.dot(s, w_ref[...], preferred_element_type=F32, precision=HIGHEST) + b_ref[...]


def _adaln(cvec, w_mod, b_mod):
    nl, d, n6 = w_mod.shape
    r = cvec.shape[0]
    tn = min(1024, n6)
    return pl.pallas_call(
        _adaln_kernel,
        grid=(nl, n6 // tn),
        in_specs=[pl.BlockSpec((r, d), lambda l, j: (0, 0)),
                  pl.BlockSpec((None, d, tn), lambda l, j: (l, 0, j)),
                  pl.BlockSpec((None, 1, tn), lambda l, j: (l, 0, j))],
        out_specs=pl.BlockSpec((None, r, tn), lambda l, j: (l, 0, j)),
        out_shape=jax.ShapeDtypeStruct((nl, r, n6), F32),
        compiler_params=_cp("parallel", "parallel"),
        name="adaln",
    )(cvec, w_mod, b_mod.reshape(nl, 1, n6))


def _norm_mod(x, g, shift, scale):
    y = x * lax.rsqrt(jnp.mean(x * x, axis=-1, keepdims=True) + NORM_EPS) * g
    return y * (1.0 + scale) + shift


def _norm_mod_store(dst, x_ref, g_ref, sh_ref, sc_ref, rows=128):
    rows = min(rows, dst.shape[0])

    def body(c, carry):
        sl = pl.ds(pl.multiple_of(c * rows, rows), rows)
        dst[sl, :] = _norm_mod(x_ref[sl, :], g_ref[...], sh_ref[...], sc_ref[...]).astype(dst.dtype)
        return carry

    lax.fori_loop(0, dst.shape[0] // rows, body, 0)


def _mod_spec(d, chunk, nlt, tpb, nb):
    return pl.BlockSpec((None, 1, d), lambda i, *_: (jnp.where(i < nlt, i // tpb, nb), 0, chunk))


def _inproj_kernel(x_ref, sh_ref, sc_ref, g_ref, w_ref, u_ref, hbf, *, tn):
    j = pl.program_id(1)

    @pl.when(j == 0)
    def _():
        _norm_mod_store(hbf, x_ref, g_ref, sh_ref, sc_ref)

    for jc in range(w_ref.shape[1] // tn):
        @pl.when(j == jc)
        def _(jc=jc):
            u_ref[...] = jnp.dot(hbf[...], w_ref[:, jc * tn:(jc + 1) * tn], preferred_element_type=F32)


def _inproj(x_all, mod3, g, w_bf, nlt, tpb, nb):
    t, d = x_all.shape
    n = w_bf.shape[1]
    tn = TN_IN if n % TN_IN == 0 else n
    return pl.pallas_call(
        functools.partial(_inproj_kernel, tn=tn),
        grid=(t // TM, n // tn),
        in_specs=[pl.BlockSpec((TM, d), lambda i, j: (i, 0)),
                  _mod_spec(d, 0, nlt, tpb, nb), _mod_spec(d, 1, nlt, tpb, nb),
                  pl.BlockSpec((1, d), lambda i, j: (0, 0)),
                  pl.BlockSpec((d, n), lambda i, j: (0, 0))],
        out_specs=pl.BlockSpec((TM, tn), lambda i, j: (i, j)),
        out_shape=jax.ShapeDtypeStruct((t, n), F32),
        scratch_shapes=[pltpu.VMEM((TM, d), BF16)],
        compiler_params=_cp("parallel", "arbitrary"),
        name="inproj",
    )(x_all, mod3, mod3, g.reshape(1, d), w_bf)


def _dwconv(x, w_ref, left, rw):
    tl = x.shape[0]
    pos = lax.broadcasted_iota(jnp.int32, x.shape, 0) % rw
    acc = None
    for k in range(w_ref.shape[0]):
        off = k - left
        if off == 0:
            term = x
        else:
            shifted = pltpu.roll(x, (-off) % tl, axis=0)
            ok = (pos + off >= 0) & (pos + off < rw)
            term = jnp.where(ok, shifted, 0.0)
        term = term * w_ref[k:k + 1, :]
        acc = term if acc is None else acc + term
    return acc


def _conv_kernel(hv_ref, h1_ref, h2_ref, sb_ref, sc_ref, sh_ref, wh_ref, ws_ref, ys_in,
                 v_ref, x1_ref, x2_ref, vbf_ref, ys_ref, *, rw, db):
    del ys_in
    for k, (ref, dst) in enumerate(((hv_ref, v_ref), (h1_ref, x1_ref), (h2_ref, x2_ref))):
        y = _dwconv(ref[...], wh_ref.at[:, k * db:(k + 1) * db], 1, rw)
        dst[...] = y
        if k == 0:
            vbf_ref[...] = y.astype(BF16)
    ys_ref[...] = (sb_ref[...] * _dwconv(sc_ref[...] * sh_ref[...], ws_ref, 1, rw)).astype(BF16)


def _conv_stage(u_all, w_hconv, w_sconv, row0, nrows, tl, rw, db, ys_init):
    ob = row0 // tl
    col = lambda c: pl.BlockSpec((tl, db), lambda i: (ob + i, c))
    own = pl.BlockSpec((tl, db), lambda i: (i, 0))
    return pl.pallas_call(
        functools.partial(_conv_kernel, rw=rw, db=db),
        grid=(nrows // tl,),
        in_specs=[col(1), col(2), col(3), col(4), col(5), col(6),
                  pl.BlockSpec(w_hconv.shape, lambda i: (0, 0)),
                  pl.BlockSpec(w_sconv.shape, lambda i: (0, 0)),
                  pl.BlockSpec(memory_space=pl.ANY)],
        out_specs=[own, own, own, own, pl.BlockSpec((tl, db), lambda i: (ob + i, 0))],
        out_shape=[jax.ShapeDtypeStruct((nrows, db), F32)] * 3
                  + [jax.ShapeDtypeStruct((nrows, db), BF16), jax.ShapeDtypeStruct(ys_init.shape, BF16)],
        input_output_aliases={8: 4},
        compiler_params=_cp("parallel"),
        name="conv",
    )(*([u_all] * 6), w_hconv, w_sconv, ys_init)


def _trig_outer(l, period):
    q = 1 << ((l.bit_length() - 1) // 2)
    n = lax.broadcasted_iota(jnp.int32, (1, l), 1)
    scale = 2.0 * math.pi / period

    def table(rows, step):
        r = lax.broadcasted_iota(jnp.int32, (rows, 1), 0) * step
        ang = ((r * n) % period).astype(F32) * scale
        return jnp.cos(ang), jnp.sin(ang)

    ac, as_ = table(l // q, q)
    bc, bs = table(q, 1)
    c = ac[:, None, :] * bc[None] - as_[:, None, :] * bs[None]
    s = as_[:, None, :] * bc[None] + ac[:, None, :] * bs[None]
    return c.reshape(l, l), s.reshape(l, l)


def _dft_mats(l):
    c, s = _trig_outer(l, 2 * l)
    k = lax.broadcasted_iota(jnp.int32, (l, l), 0)
    n = lax.broadcasted_iota(jnp.int32, (l, l), 1)
    alt_n = jnp.where(n % 2 == 0, 1.0, -1.0).astype(F32)
    alt_k = jnp.where(k % 2 == 0, 1.0, -1.0).astype(F32)
    return c.astype(BF16), jnp.where(k == 0, alt_n, -s).astype(BF16), jnp.where(n == 0, alt_k, -s).astype(BF16)


def _fnet_mats(l, gdim, groups):
    c, s = _trig_outer(l, l)
    cg, sg = _trig_outer(gdim, gdim)
    eye = jnp.eye(groups, dtype=F32)
    w_pq = jnp.concatenate([jnp.kron(eye, cg), jnp.kron(eye, sg)], axis=1).astype(BF16)
    return c.astype(BF16), (-s).astype(BF16), w_pq


def _fnet_pq_kernel(u_ref, w_ref, o_ref):
    o_ref[...] = jnp.dot(u_ref[...].astype(BF16), w_ref[...], preferred_element_type=F32).astype(BF16)


def _fnet_pq(u_all, w_pq, db):
    t = u_all.shape[0]
    return pl.pallas_call(
        _fnet_pq_kernel,
        grid=(t // TM,),
        in_specs=[pl.BlockSpec((TM, db), lambda i: (i, 0)),
                  pl.BlockSpec(w_pq.shape, lambda i: (0, 0))],
        out_specs=pl.BlockSpec((TM, 2 * db), lambda i: (i, 0)),
        out_shape=jax.ShapeDtypeStruct((t, 2 * db), BF16),
        compiler_params=_cp("parallel"),
        name="fnet_pq",
    )(u_all, w_pq)


def _fnet_dft_kernel(d0_ref, d1_ref, x_ref, *rest, db, scale):
    o_ref = rest[-1]
    acc = jnp.dot(d0_ref[...], x_ref[:, :db], preferred_element_type=F32)
    acc = acc + jnp.dot(d1_ref[...], x_ref[:, db:], preferred_element_type=F32)
    o_ref[...] = (acc * scale).astype(BF16)


def _mat_spec(tr, l):
    return pl.BlockSpec((tr, l), lambda i, b: (i, 0))


def _fnet_dft(d0, d1, pq, row0, nb, l, db, gdim, out_init, t_all):
    tr = min(TR, l)
    nrt = l // tr
    args = [d0, d1, pq]
    in_specs = [_mat_spec(tr, l), _mat_spec(tr, l),
                pl.BlockSpec((l, 2 * db), lambda i, b: (row0 // l + b, 0))]
    aliases = {}
    if out_init is not None:
        args.append(out_init)
        in_specs.append(pl.BlockSpec(memory_space=pl.ANY))
        aliases = {3: 0}
    return pl.pallas_call(
        functools.partial(_fnet_dft_kernel, db=db, scale=1.0 / math.sqrt(l * gdim)),
        grid=(nrt, nb),
        in_specs=in_specs,
        out_specs=pl.BlockSpec((tr, db), lambda i, b: (row0 // tr + b * nrt + i, 0)),
        out_shape=jax.ShapeDtypeStruct((t_all, db), BF16),
        input_output_aliases=aliases,
        compiler_params=_cp("parallel", "parallel"),
        name="fnet_dft",
    )(*args)


def _hy_fwd_kernel(f0_ref, f1_ref, x_ref, *rest, mult):
    o_ref = rest[-1]
    zre = jnp.dot(f0_ref[...], x_ref[...], preferred_element_type=F32)
    zim = jnp.dot(f1_ref[...], x_ref[...], preferred_element_type=F32)
    if mult:
        k_ref = rest[0]
        p, q, r = k_ref[0], k_ref[1], k_ref[2]
        o_ref[0] = (zre * p - zim * q).astype(o_ref.dtype)
        o_ref[1] = (zre * q + zim * r).astype(o_ref.dtype)
    else:
        o_ref[0] = zre
        o_ref[1] = zim


def _hy_fwd(fwd, x, nb, l, db, pqr):
    tr = min(TR, l)
    args = [fwd[0], fwd[1], x]
    in_specs = [_mat_spec(tr, l), _mat_spec(tr, l),
                pl.BlockSpec((l, db), lambda i, b: (b, 0))]
    if pqr is not None:
        args.append(pqr)
        in_specs.append(pl.BlockSpec((3, tr, db), lambda i, b: (0, i, 0)))
    return pl.pallas_call(
        functools.partial(_hy_fwd_kernel, mult=pqr is not None),
        grid=(l // tr, nb),
        in_specs=in_specs,
        out_specs=pl.BlockSpec((None, 2, tr, db), lambda i, b: (b, 0, i, 0)),
        out_shape=jax.ShapeDtypeStruct((nb, 2, l, db), F32 if pqr is None else BF16),
        compiler_params=_cp("parallel", "parallel"),
        name="hy_fwd",
    )(*args)


def _hy_inv_kernel(g0_ref, g1_ref, y_ref, zp_ref, gate_ref, bias_ref, *rest):
    o_ref = rest[-1]
    y = jnp.dot(g0_ref[...], y_ref[0], preferred_element_type=F32)
    y = y + jnp.dot(g1_ref[...], y_ref[1], preferred_element_type=F32)
    o_ref[...] = (gate_ref[...] * (y + bias_ref[...] * zp_ref[...].astype(F32))).astype(BF16)


def _hy_inv(inv, yhat, zprev, zcol, gate, gcol, bias, nb, l, db, out_row0, out_rows, out_init):
    tr = min(TR, l)
    nrt = l // tr
    args = [inv[0], inv[1], yhat, zprev, gate, bias]
    in_specs = [_mat_spec(tr, l), _mat_spec(tr, l),
                pl.BlockSpec((None, 2, l, db), lambda i, b: (b, 0, 0, 0)),
                pl.BlockSpec((tr, db), lambda i, b: (b * nrt + i, zcol)),
                pl.BlockSpec((tr, db), lambda i, b: (b * nrt + i, gcol)),
                pl.BlockSpec((1, db), lambda i, b: (0, 0))]
    aliases = {}
    if out_init is not None:
        args.append(out_init)
        in_specs.append(pl.BlockSpec(memory_space=pl.ANY))
        aliases = {6: 0}
    return pl.pallas_call(
        _hy_inv_kernel,
        grid=(nrt, nb),
        in_specs=in_specs,
        out_specs=pl.BlockSpec((tr, db), lambda i, b: (out_row0 // tr + b * nrt + i, 0)),
        out_shape=jax.ShapeDtypeStruct((out_rows, db), BF16),
        input_output_aliases=aliases,
        compiler_params=_cp("parallel", "parallel"),
        name="hy_inv",
    )(*args)


def _hyena_filters(length, hf_w1, hf_b1, hf_w2, hf_b2, hf_w3, hf_freq, db, reverse_bwd=False):
    emb = hf_w1.shape[0]
    nbands = (emb - 1) // 2
    bands = jnp.linspace(1e-4, nbands - 1, nbands, dtype=F32)[None, :]
    min_decay = math.log(HY_DECAY_TARGET) / HY_SLOW_PCT
    max_decay = math.log(HY_DECAY_TARGET) / HY_FAST_PCT
    deltas = jnp.abs(jnp.linspace(min_decay, max_decay, db, dtype=F32))
    w3 = hf_w3.reshape(hf_w3.shape[0], -1, 2, db)

    def direction(pos, d):
        t = (pos * (1.0 / (length - 1)))[:, None]
        w = (2.0 * math.pi / length) * pos[:, None]
        feats = jnp.concatenate([t, jnp.cos(bands * w), -jnp.sin(bands * w)], axis=-1)
        z = jnp.sin(hf_freq[0] * (jnp.dot(feats, hf_w1, precision=HIGHEST) + hf_b1))
        z = jnp.sin(hf_freq[1] * (jnp.dot(z, hf_w2, precision=HIGHEST) + hf_b2))
        f = jnp.einsum("lf,fod->lod", z, w3[:, :, d, :], precision=HIGHEST)
        return f * jnp.exp(-t * deltas)[:, None, :]

    pos = jnp.arange(length, dtype=F32)
    filt = jnp.stack([direction(pos, 0), direction(length - 1 - pos if reverse_bwd else pos, 1)], axis=2)
    return filt * lax.rsqrt(jnp.sum(filt * filt, axis=(0, 2), keepdims=True) + NORM_EPS)


def _filter_spectra(fwd, filt, l, db):
    n_ord = filt.shape[1]
    cols = jnp.transpose(filt, (1, 2, 0, 3))
    cols = cols.at[:, 1, 0, :].set(0.0)
    x = cols.reshape(n_ord * 2 * l, db).astype(BF16)
    spec = _hy_fwd(fwd, x, n_ord * 2, l, db, None).reshape(n_ord, 2, 2, l, db)
    hf, hb = spec[:, 0], spec[:, 1]
    k_re = hf[:, 0] + hb[:, 0]
    k_im = hf[:, 1] - hb[:, 1]
    k_nyq = hf[:, 1, 0] + hb[:, 1, 0]
    first = (jnp.arange(l) == 0)[None, :, None]
    scale = jnp.where(first, 1.0 / (2 * l), 2.0 / (2 * l)).astype(F32)
    p = k_re * scale
    q = jnp.where(first, 0.0, k_im) * scale
    r = jnp.where(first, k_nyq[:, None, :], k_re) * scale
    return jnp.stack([p, q, r], axis=1)


FFT_N2 = 128
FFT_NT = 16
FFT_KC = 8


def _fft_consts(l):
    n, n2 = 2 * l, FFT_N2
    n1 = n // n2
    h = n1 // 2
    cat = jnp.concatenate
    ia = jnp.arange(n1, dtype=jnp.int32)
    ang = ((ia[:, None] * ia[None, :]) % n1).astype(F32) * (2.0 * math.pi / n1)
    fr, fi = jnp.cos(ang), -jnp.sin(ang)
    mat_data = cat([cat([fr[:, :h], -fi[:, :h]], 1), cat([fi[:, :h], fr[:, :h]], 1)], 0)
    mat_filt = cat([fr, fi], 0)
    gr, gi = fr[:, :h].T, -fi[:, :h].T
    mat_inv = jnp.stack([cat([gr, -gi], 1), cat([gi, gr], 1)])
    k = ia[:, None, None] + n1 * jnp.arange(n2, dtype=jnp.int32)[None, :, None]
    nn = jnp.arange(n2, dtype=jnp.int32)[None, None, :]
    angb = ((k * nn) % n).astype(F32) * (2.0 * math.pi / n)
    er, ei = jnp.cos(angb), -jnp.sin(angb)
    mb = cat([cat([er, -ei], 2), cat([ei, er], 2)], 1)
    eye = jnp.eye(FFT_NT, dtype=F32)
    kron = lambda a: jnp.kron(a, eye).astype(BF16)
    return dict(n1=n1, h=h, mat_data=kron(mat_data), mat_filt=kron(mat_filt * (1.0 / n)),
                mat_inv=jnp.stack([kron(mat_inv[0]), kron(mat_inv[1])]),
                mb=mb.astype(BF16), mib=jnp.transpose(mb, (0, 2, 1)).astype(BF16))


def _fft_a_kernel(m_ref, a_ref, b_ref, o_ref):
    h, nt, db = a_ref.shape
    x = jnp.concatenate([a_ref[...].reshape(h * nt, db), b_ref[...].reshape(h * nt, db)], axis=0)
    res = jnp.dot(m_ref[...], x, preferred_element_type=F32)
    o_ref[...] = res.astype(o_ref.dtype).reshape(o_ref.shape)


def _fft_stage_a(mat, x4, npairs, stride, offset):
    _, h, n2, db = x4.shape
    nt = FFT_NT
    rows = mat.shape[0] // nt
    return pl.pallas_call(
        _fft_a_kernel,
        grid=(npairs, n2 // nt),
        in_specs=[pl.BlockSpec(mat.shape, lambda p, j: (0, 0)),
                  pl.BlockSpec((None, h, nt, db), lambda p, j: (p * stride, 0, j, 0)),
                  pl.BlockSpec((None, h, nt, db), lambda p, j: (p * stride + offset, 0, j, 0))],
        out_specs=pl.BlockSpec((None, rows, nt, db), lambda p, j: (p, 0, j, 0)),
        out_shape=jax.ShapeDtypeStruct((npairs, rows, n2, db), BF16),
        compiler_params=_cp("parallel", "parallel"),
        name="fft_a",
    )(mat, x4, x4)


def _fft_b_kernel(m_ref, ar_ref, ai_ref, o_ref):
    for kk in range(m_ref.shape[0]):
        a = jnp.concatenate([ar_ref[kk], ai_ref[kk]], axis=0)
        o_ref[kk] = jnp.dot(m_ref[kk], a, preferred_element_type=F32)


def _fft_bb_kernel(m_ref, mi_ref, ar_ref, ai_ref, k_ref, o_ref):
    n2 = ar_ref.shape[1]
    for kk in range(m_ref.shape[0]):
        a = jnp.concatenate([ar_ref[kk], ai_ref[kk]], axis=0)
        x = jnp.dot(m_ref[kk], a, preferred_element_type=F32)
        xr, xi = x[:n2], x[n2:]
        kr, ki = k_ref[kk, :n2], k_ref[kk, n2:]
        y = jnp.concatenate([(xr * kr - xi * ki).astype(BF16), (xr * ki + xi * kr).astype(BF16)], axis=0)
        b = jnp.dot(mi_ref[kk], y, preferred_element_type=F32)
        o_ref[0, kk] = b[:n2].astype(o_ref.dtype)
        o_ref[1, kk] = b[n2:].astype(o_ref.dtype)


def _fft_stage_b(fc, a, kspec, db):
    mb, mib = fc["mb"], fc["mib"]
    npairs = a.shape[0]
    n1, r2, _ = mb.shape
    n2 = r2 // 2
    kc = min(FFT_KC, n1)
    a5 = a.reshape(npairs, 2, n1, n2, db)
    mspec = pl.BlockSpec((kc, r2, r2), lambda c, p: (c, 0, 0))
    aspec = lambda part: pl.BlockSpec((None, None, kc, n2, db), lambda c, p: (p, part, c, 0, 0))
    if kspec is None:
        kern, args, in_specs = _fft_b_kernel, [mb, a5, a5], [mspec, aspec(0), aspec(1)]
        out_spec = pl.BlockSpec((None, kc, r2, db), lambda c, p: (p, c, 0, 0))
        out_shape = jax.ShapeDtypeStruct((npairs, n1, r2, db), F32)
    else:
        kern, args = _fft_bb_kernel, [mb, mib, a5, a5, kspec]
        in_specs = [mspec, mspec, aspec(0), aspec(1), pl.BlockSpec((kc, r2, db), lambda c, p: (c, 0, 0))]
        out_spec = pl.BlockSpec((None, 2, kc, n2, db), lambda c, p: (p, 0, c, 0, 0))
        out_shape = jax.ShapeDtypeStruct((npairs, 2, n1, n2, db), BF16)
    return pl.pallas_call(
        kern,
        grid=(n1 // kc, npairs),
        in_specs=in_specs,
        out_specs=out_spec,
        out_shape=out_shape,
        compiler_params=_cp("parallel", "parallel"),
        name="fft_b",
    )(*args)


def _ifft_a_kernel(m_ref, b_ref, zp_ref, gate_ref, bias_ref, *rest):
    o_ref = rest[-1]
    h, nt, db = o_ref.shape
    bm = b_ref[...].reshape(b_ref.shape[0] * nt, db)
    y = jnp.dot(m_ref[...], bm, preferred_element_type=F32)
    zp = zp_ref[...].reshape(h * nt, db).astype(F32)
    out = gate_ref[...].reshape(h * nt, db) * (y + bias_ref[...] * zp)
    o_ref[...] = out.astype(o_ref.dtype).reshape(o_ref.shape)


def _ifft_stage_a(mat_inv, b5, zprev, gate, bias, nb, db, out_init, out_rows):
    npairs, _, n1, n2, _ = b5.shape
    nt = FFT_NT
    h = mat_inv.shape[1] // nt
    b4 = b5.reshape(npairs, 2 * n1, n2, db)
    view = lambda a: a.reshape(a.shape[0] // n2, n2, db)
    blk = pl.BlockSpec((h, nt, db), lambda b, j: (b, j, 0))
    args = [mat_inv, b4, view(zprev), view(gate), bias]
    in_specs = [pl.BlockSpec((None,) + mat_inv.shape[1:], lambda b, j: (b // npairs, 0, 0)),
                pl.BlockSpec((None, 2 * n1, nt, db), lambda b, j: (b % npairs, 0, j, 0)),
                blk, blk, pl.BlockSpec((1, db), lambda b, j: (0, 0))]
    aliases = {}
    if out_init is not None:
        args.append(view(out_init))
        in_specs.append(pl.BlockSpec(memory_space=pl.ANY))
        aliases = {5: 0}
    out = pl.pallas_call(
        _ifft_a_kernel,
        grid=(nb, n2 // nt),
        in_specs=in_specs,
        out_specs=blk,
        out_shape=jax.ShapeDtypeStruct((out_rows // n2, n2, db), BF16),
        input_output_aliases=aliases,
        compiler_params=_cp("parallel", "parallel"),
        name="ifft_a",
    )(*args)
    return out.reshape(out_rows, db)


def _fft_filter_spectra(fc, filt, l, db):
    n_ord = filt.shape[1]
    h_fwd = jnp.transpose(filt[:, :, 0, :], (1, 0, 2))
    h_bwd = jnp.transpose(filt[:, :, 1, :], (1, 0, 2))
    ker = jnp.concatenate([h_fwd, jnp.zeros((n_ord, 1, db), F32), h_bwd[:, :l - 1, :]], axis=1)
    x4 = ker.astype(BF16).reshape(n_ord * 2, fc["h"], FFT_N2, db)
    a = _fft_stage_a(fc["mat_filt"], x4, n_ord, 2, 1)
    return _fft_stage_b(fc, a, None, db)


def _fft_conv(fc, z_bf, kspec, zprev, gate, bias, nb, db, out_init, out_rows):
    x4 = z_bf.reshape(nb, fc["h"], FFT_N2, db)
    a = _fft_stage_a(fc["mat_data"], x4, nb // 2, 1, nb // 2)
    b5 = _fft_stage_b(fc, a, kspec, db)
    return _ifft_stage_a(fc["mat_inv"], b5, zprev, gate, bias, nb, db, out_init, out_rows)


def _lru_kernel(*refs, rw, reverse, final):
    if final:
        (x_ref, w_ref, wa_ref, wx_ref, ba_ref, bx_ref, lam_ref, h0_ref, hf_ref, ug_ref,
         _, o_ref, st_ref, a_s, b_s, carry) = refs
    else:
        (x_ref, w_ref, wa_ref, wx_ref, ba_ref, bx_ref, lam_ref, h0_ref,
         o_ref, st_ref, a_s, b_s, carry) = refs
    tl = x_ref.shape[0]

    @pl.when(pl.program_id(1) == 0)
    def _():
        carry[...] = h0_ref[...]

    xc = _dwconv(x_ref[...], w_ref, w_ref.shape[0] // 2, rw)
    xb = xc.astype(BF16)
    r = _sigmoid(jnp.dot(xb, wa_ref[...], preferred_element_type=F32) + ba_ref[...])
    ig = _sigmoid(jnp.dot(xb, wx_ref[...], preferred_element_type=F32) + bx_ref[...])
    lam = lam_ref[...]
    softplus = jnp.maximum(-lam, 0.0) + jnp.log(1.0 + jnp.exp(-jnp.abs(lam)))
    log_a = -LRU_C * r * softplus
    a = jnp.exp(log_a)
    bv = jnp.sqrt(1.0 - jnp.exp(2.0 * log_a)) * (ig * xc)

    pos8 = lax.broadcasted_iota(jnp.int32, a.shape, 0) % 8
    for s in (1, 2, 4):
        if reverse:
            a_sh, b_sh, ok = pltpu.roll(a, tl - s, axis=0), pltpu.roll(bv, tl - s, axis=0), pos8 < 8 - s
        else:
            a_sh, b_sh, ok = pltpu.roll(a, s, axis=0), pltpu.roll(bv, s, axis=0), pos8 >= s
        bv = jnp.where(ok, a * b_sh + bv, bv)
        a = jnp.where(ok, a * a_sh, a)
    a_s[...] = a
    b_s[...] = bv

    ng = tl // 8

    def body(gi, c):
        g = ng - 1 - gi if reverse else gi
        sl = pl.ds(pl.multiple_of(g * 8, 8), 8)
        h8 = b_s[sl, :] + a_s[sl, :] * c
        b_s[sl, :] = h8
        return h8[0:1, :] if reverse else h8[7:8, :]

    c_out = lax.fori_loop(0, ng, body, carry[...])
    carry[...] = c_out
    st_ref[...] = c_out
    if final:
        ug = ug_ref[...]
        gelu = 0.5 * ug * (1.0 + jnp.tanh(math.sqrt(2.0 / math.pi) * (ug + 0.044715 * ug * ug * ug)))
        o_ref[...] = ((hf_ref[...] + b_s[...]) * gelu).astype(BF16)
    else:
        o_ref[...] = b_s[...]


def _lru_pass(u_all, w_lconv, wa, wx, ba, bx, lam, h0, row0, nb, l, tl, rw, db, reverse, hf=None, out_init=None,
              t_all=None):
    final = hf is not None
    nc = l // tl
    ob = row0 // tl
    cidx = (lambda c: nc - 1 - c) if reverse else (lambda c: c)
    full = lambda a: pl.BlockSpec(a.shape, lambda b, c: (0,) * a.ndim)
    ncolx, ncolg = 7, 8
    args = [u_all, w_lconv, wa, wx, ba, bx, lam, h0]
    in_specs = [pl.BlockSpec((tl, db), lambda b, c: (ob + b * nc + cidx(c), ncolx)),
                full(w_lconv), full(wa), full(wx), full(ba), full(bx), full(lam),
                pl.BlockSpec((None, 1, db), lambda b, c: (b, 0, 0))]
    aliases = {}
    if final:
        args += [hf, u_all]
        in_specs += [pl.BlockSpec((tl, db), lambda b, c: (b * nc + cidx(c), 0)),
                     pl.BlockSpec((tl, db), lambda b, c: (ob + b * nc + cidx(c), ncolg))]
        if out_init is None:
            out_init = jnp.zeros((8, 128), BF16)
        else:
            aliases = {10: 0}
        args.append(out_init)
        in_specs.append(pl.BlockSpec(memory_space=pl.ANY))
        out_spec = pl.BlockSpec((tl, db), lambda b, c: (ob + b * nc + cidx(c), 0))
        out_shape = jax.ShapeDtypeStruct((t_all, db), BF16)
    else:
        out_spec = pl.BlockSpec((tl, db), lambda b, c: (b * nc + cidx(c), 0))
        out_shape = jax.ShapeDtypeStruct((nb * l, db), F32)
    return pl.pallas_call(
        functools.partial(_lru_kernel, rw=rw, reverse=reverse, final=final),
        grid=(nb, nc),
        in_specs=in_specs,
        out_specs=[out_spec, pl.BlockSpec((None, 1, db), lambda b, c: (b, 0, 0))],
        out_shape=[out_shape, jax.ShapeDtypeStruct((nb, 1, db), F32)],
        scratch_shapes=[pltpu.VMEM((tl, db), F32), pltpu.VMEM((tl, db), F32), pltpu.VMEM((1, db), F32)],
        input_output_aliases=aliases,
        compiler_params=_cp("arbitrary", "arbitrary"),
        name="lru_bwd" if reverse else "lru_fwd",
    )(*args)


def _block_diag(w):
    h, hd, _ = w.shape
    eye = jnp.eye(h, dtype=w.dtype)
    return (eye[:, None, :, None] * w[:, :, None, :]).reshape(h * hd, h * hd).astype(BF16)


def _merge_kernel(x_ref, sh_ref, sc_ref, g_ref, b0_ref, b1_ref, b2_ref, b3_ref, wm_ref, wb_ref, wo_ref, ga_ref,
                  o_ref, hbf):
    j = pl.program_id(1)

    @pl.when(j == 0)
    def _():
        _norm_mod_store(hbf, x_ref, g_ref, sh_ref, sc_ref)

    h = hbf[...]
    m = None
    for k, br in enumerate((b0_ref, b1_ref, b2_ref, b3_ref)):
        g = jnp.dot(h, wm_ref[k], preferred_element_type=F32)
        p = jnp.dot(br[...], wb_ref[k], preferred_element_type=F32)
        term = _sigmoid(g) * p
        m = term if m is None else m + term
    part = jnp.dot(m.astype(BF16), wo_ref[...], preferred_element_type=F32)

    @pl.when(j == 0)
    def _():
        o_ref[...] = part

    @pl.when(j > 0)
    def _():
        o_ref[...] += part

    @pl.when(j == pl.num_programs(1) - 1)
    def _():
        o_ref[...] = x_ref[...] + ga_ref[...] * o_ref[...]


def _merge(branches, wm_bf, wb_bf, wo_bf, x_all, mod3, g, ntiles, nlt, tpb, nb):
    t, d = x_all.shape
    db = wb_bf.shape[1]
    tn = min(TN_MERGE, d)
    row = lambda w: pl.BlockSpec((TM, w), lambda i, j: (i, 0))
    return pl.pallas_call(
        _merge_kernel,
        grid=(ntiles, d // tn),
        in_specs=[row(d), _mod_spec(d, 0, nlt, tpb, nb), _mod_spec(d, 1, nlt, tpb, nb),
                  pl.BlockSpec((1, d), lambda i, j: (0, 0)),
                  row(db), row(db), row(db), row(db),
                  pl.BlockSpec((4, d, tn), lambda i, j: (0, 0, j)),
                  pl.BlockSpec((4, db, tn), lambda i, j: (0, 0, j)),
                  pl.BlockSpec((tn, d), lambda i, j: (j, 0)),
                  _mod_spec(d, 2, nlt, tpb, nb)],
        out_specs=row(d),
        out_shape=jax.ShapeDtypeStruct((ntiles * TM, d), F32),
        scratch_shapes=[pltpu.VMEM((TM, d), BF16)],
        compiler_params=_cp("parallel", "arbitrary"),
        name="merge",
    )(x_all, mod3, mod3, g.reshape(1, d), *branches, wm_bf, wb_bf, wo_bf, mod3)


def _router_kernel(x_ref, sh_ref, sc_ref, g_ref, whi_ref, wlo_ref, br_ref, h_ref, r_ref, c_ref, *, ngroups, epg):
    h = _norm_mod(x_ref[...], g_ref[...], sh_ref[...], sc_ref[...])
    h_ref[...] = h
    h_hi = h.astype(BF16)
    h_lo = (h - h_hi.astype(F32)).astype(BF16)
    logits = jnp.dot(h_hi, wlo_ref[...], preferred_element_type=F32)
    logits = logits + jnp.dot(h_lo, whi_ref[...], preferred_element_type=F32)
    logits = logits + jnp.dot(h_hi, whi_ref[...], preferred_element_type=F32) + br_ref[...]
    lane = lax.broadcasted_iota(jnp.int32, logits.shape, 1)
    lane_f = lane.astype(F32)
    neg = jnp.float32(-1e30)
    big = jnp.float32(1e6)
    is_g = lane < ngroups
    lg = jnp.where(is_g, logits, neg)
    mx = jnp.max(lg, axis=-1, keepdims=True)
    g_top = jnp.min(jnp.where(lg == mx, lane_f, big), axis=-1, keepdims=True)
    den = jnp.sum(jnp.where(is_g, jnp.exp(lg - mx), 0.0), axis=-1, keepdims=True)
    p_top = 1.0 / den
    el = lane_f - ngroups
    in_grp = (el >= g_top * epg) & (el < (g_top + 1.0) * epg)
    le = jnp.where(in_grp, logits, neg)
    v1 = jnp.max(le, axis=-1, keepdims=True)
    e1 = jnp.min(jnp.where(le == v1, el, big), axis=-1, keepdims=True)
    le2 = jnp.where(el == e1, neg, le)
    v2 = jnp.max(le2, axis=-1, keepdims=True)
    e2 = jnp.min(jnp.where(le2 == v2, el, big), axis=-1, keepdims=True)
    dlt = jnp.exp(v2 - v1)
    w1 = p_top / (1.0 + dlt)
    w2 = p_top * dlt / (1.0 + dlt)
    tm = logits.shape[0]
    pick1 = el == e1
    pick2 = el == e2
    onehot = jnp.where(pick1 | pick2, 1.0, 0.0)
    row = lax.broadcasted_iota(jnp.int32, (tm, tm), 0)
    col = lax.broadcasted_iota(jnp.int32, (tm, tm), 1)
    earlier = jnp.where(col < row, 1.0, 0.0).astype(BF16)
    before = jnp.dot(earlier, onehot.astype(BF16), preferred_element_type=F32)
    rank1 = jnp.sum(jnp.where(pick1, before, 0.0), axis=-1, keepdims=True)
    rank2 = jnp.sum(jnp.where(pick2, before, 0.0), axis=-1, keepdims=True)
    out = jnp.where(lane == 0, e1, 0.0)
    out = jnp.where(lane == 1, e2, out)
    out = jnp.where(lane == 2, w1, out)
    out = jnp.where(lane == 3, w2, out)
    out = jnp.where(lane == 4, rank1, out)
    out = jnp.where(lane == 5, rank2, out)
    r_ref[...] = out
    c_ref[...] = jnp.broadcast_to(jnp.sum(onehot, axis=0, keepdims=True), c_ref.shape)


def _router(x_all, mod3, g, w_router, b_router, ntiles, nlt, tpb, nb, ngroups, epg):
    t, d = x_all.shape
    w_hi = w_router.astype(BF16)
    w_hi_lo = (w_hi, (w_router - w_hi.astype(F32)).astype(BF16))
    return pl.pallas_call(
        functools.partial(_router_kernel, ngroups=ngroups, epg=epg),
        grid=(ntiles,),
        in_specs=[pl.BlockSpec((TM, d), lambda i: (i, 0)),
                  _mod_spec(d, 3, nlt, tpb, nb), _mod_spec(d, 4, nlt, tpb, nb),
                  pl.BlockSpec((1, d), lambda i: (0, 0)),
                  pl.BlockSpec((d, 128), lambda i: (0, 0)),
                  pl.BlockSpec((d, 128), lambda i: (0, 0)),
                  pl.BlockSpec((1, 128), lambda i: (0, 0))],
        out_specs=[pl.BlockSpec((TM, d), lambda i: (i, 0)), pl.BlockSpec((TM, 128), lambda i: (i, 0)),
                   pl.BlockSpec((None, 8, 128), lambda i: (i, 0, 0))],
        out_shape=[jax.ShapeDtypeStruct((ntiles * TM, d), F32), jax.ShapeDtypeStruct((ntiles * TM, 128), F32),
                   jax.ShapeDtypeStruct((ntiles, 8, 128), F32)],
        compiler_params=_cp("parallel"),
        name="router",
    )(x_all, mod3, mod3, g.reshape(1, d), w_hi, w_hi_lo[1], b_router)


def _route_tables(slab, tile_counts, ngroups, n_experts, tm):
    ntiles = tile_counts.shape[0]
    e = slab[:, 0:TOP_K].astype(jnp.int32).reshape(ntiles, TM, TOP_K)
    rank = slab[:, 4:4 + TOP_K].astype(jnp.int32).reshape(ntiles, TM, TOP_K)
    counts_t = tile_counts[:, 0, ngroups:ngroups + n_experts].astype(jnp.int32)
    base = jnp.cumsum(counts_t, axis=0) - counts_t
    counts = jnp.sum(counts_t, axis=0)
    tiles_per_e = (counts + tm - 1) // tm
    tile_end = jnp.cumsum(tiles_per_e)
    starts = (tile_end - tiles_per_e) * tm
    off = starts[None, :] + base
    sel = e[..., None] == jnp.arange(n_experts, dtype=jnp.int32)
    slot = jnp.sum(jnp.where(sel, off[:, None, None, :], 0), axis=-1) + rank
    n_slot_tiles = (ntiles * TM * TOP_K) // tm + n_experts
    n_valid = tile_end[-1]
    tile_ids = jnp.minimum(jnp.arange(n_slot_tiles, dtype=jnp.int32), n_valid - 1)
    tile_expert = jnp.sum((tile_end[None, :] <= tile_ids[:, None]).astype(jnp.int32), axis=1)
    return slot, tile_expert, n_valid.reshape(1).astype(jnp.int32), n_slot_tiles


ROW_DMA_UNROLL = 8


def _stage_indices(idx_hbm_row, idx_smem, sem_i):
    cp = pltpu.make_async_copy(idx_hbm_row, idx_smem, sem_i)
    cp.start()
    cp.wait()


def _dispatch_kernel(p_hbm, h_ref, xs_in, xs_hbm, idx, sem_i, sem_x):
    del xs_in
    i = pl.program_id(0)
    tm = h_ref.shape[0]
    _stage_indices(p_hbm.at[i], idx, sem_i)

    def issue(r, c):
        src = h_ref.at[pl.ds(r, 1), :]
        for k in range(TOP_K):
            pltpu.make_async_copy(src, xs_hbm.at[pl.ds(idx[k * tm + r], 1), :], sem_x).start()
        return c

    lax.fori_loop(0, tm, issue, 0, unroll=ROW_DMA_UNROLL)
    for k in range(TOP_K):
        pltpu.make_async_copy(h_ref, xs_hbm.at[pl.ds(0, tm), :], sem_x).wait()


def _dispatch(h_all, slot, n_slot_rows):
    ntiles = slot.shape[0]
    d = h_all.shape[1]
    p = jnp.transpose(slot, (0, 2, 1)).reshape(ntiles, TOP_K * TM)
    xs0 = jnp.zeros((n_slot_rows, d), F32)
    return pl.pallas_call(
        _dispatch_kernel,
        grid=(ntiles,),
        in_specs=[pl.BlockSpec(memory_space=pl.ANY), pl.BlockSpec((TM, d), lambda i: (i, 0)),
                  pl.BlockSpec(memory_space=pl.ANY)],
        out_specs=pl.BlockSpec(memory_space=pl.ANY),
        out_shape=jax.ShapeDtypeStruct((n_slot_rows, d), F32),
        scratch_shapes=[pltpu.SMEM((TOP_K * TM,), jnp.int32),
                        pltpu.SemaphoreType.DMA(()), pltpu.SemaphoreType.DMA(())],
        input_output_aliases={2: 0},
        compiler_params=_cp("arbitrary"),
        name="dispatch",
    )(p, h_all, xs0)


def _cast_kernel(w_ref, o_ref):
    o_ref[...] = w_ref[...].astype(o_ref.dtype)


def _layer_bf16(w_stack, l):
    _, e, a, b = w_stack.shape
    return pl.pallas_call(
        _cast_kernel,
        grid=(e,),
        in_specs=[pl.BlockSpec((None, None, a, b), lambda i: (l, i, 0, 0))],
        out_specs=pl.BlockSpec((None, a, b), lambda i: (i, 0, 0)),
        out_shape=jax.ShapeDtypeStruct((e, a, b), BF16),
        compiler_params=_cp("parallel"),
        name="cast_bf16",
    )(w_stack)


def _moe_kernel(te_ref, nv_ref, x_ref, wg_ref, wu_ref, wd_ref, o_ref):
    del te_ref
    valid = pl.program_id(0) < nv_ref[0]

    @pl.when(valid)
    def _():
        x = x_ref[...].astype(BF16)
        g = jnp.dot(x, wg_ref[...], preferred_element_type=F32)
        u = jnp.dot(x, wu_ref[...], preferred_element_type=F32)
        hid = (g * _sigmoid(g) * u).astype(BF16)
        o_ref[...] = jnp.dot(hid, wd_ref[...], preferred_element_type=F32)

    @pl.when(jnp.logical_not(valid))
    def _():
        o_ref[...] = jnp.zeros_like(o_ref)


def _moe_experts(xs, tile_expert, n_valid, w_gate, w_up, w_down, tm):
    n_rows, d = xs.shape
    f = w_gate.shape[2]
    xi = lambda i, te, nv: (jnp.maximum(jnp.minimum(i, nv[0] - 1), 0), 0)
    grid_spec = pltpu.PrefetchScalarGridSpec(
        num_scalar_prefetch=2,
        grid=(n_rows // tm,),
        in_specs=[pl.BlockSpec((tm, d), xi),
                  pl.BlockSpec((None, d, f), lambda i, te, nv: (te[i], 0, 0)),
                  pl.BlockSpec((None, d, f), lambda i, te, nv: (te[i], 0, 0)),
                  pl.BlockSpec((None, f, d), lambda i, te, nv: (te[i], 0, 0))],
        out_specs=pl.BlockSpec((tm, d), lambda i, te, nv: (i, 0)),
    )
    return pl.pallas_call(
        _moe_kernel,
        grid_spec=grid_spec,
        out_shape=jax.ShapeDtypeStruct((n_rows, d), F32),
        compiler_params=_cp("arbitrary"),
        name="moe_experts",
    )(tile_expert, n_valid, xs, w_gate, w_up, w_down)


def _combine_kernel(p_hbm, ys_hbm, x_ref, slab_ref, ga_ref, g_ref, o_ref, idx, buf, sem_i, sem_x, *, final):
    i = pl.program_id(0)
    n = pl.num_programs(0)
    rows = idx.shape[0]
    tm = rows // TOP_K

    def fetch(tile, s):
        _stage_indices(p_hbm.at[tile], idx, sem_i)

        def issue(r, c):
            pltpu.make_async_copy(ys_hbm.at[pl.ds(idx[r], 1), :], buf.at[s, pl.ds(r, 1), :], sem_x.at[s]).start()
            return c

        lax.fori_loop(0, rows, issue, 0, unroll=ROW_DMA_UNROLL)

    @pl.when(i == 0)
    def _():
        fetch(0, 0)

    @pl.when(i + 1 < n)
    def _():
        fetch(i + 1, (i + 1) % 2)

    s = i % 2
    pltpu.make_async_copy(ys_hbm.at[pl.ds(0, rows), :], buf.at[s], sem_x.at[s]).wait()
    slab = slab_ref[...]
    y = slab[:, TOP_K:TOP_K + 1] * buf[s, 0:tm, :]
    for k in range(1, TOP_K):
        y = y + slab[:, TOP_K + k:TOP_K + k + 1] * buf[s, k * tm:(k + 1) * tm, :]
    x = x_ref[...] + ga_ref[...] * y
    if final:
        x = x * lax.rsqrt(jnp.mean(x * x, axis=-1, keepdims=True) + NORM_EPS) * g_ref[...]
    o_ref[...] = x


def _combine(ys, slot, slab, x_all, mod3, g_final, ntiles, nlt, tpb, nb, final):
    d = x_all.shape[1]
    tm = TM_COMB
    f = TM // tm
    n = ntiles * f
    p = jnp.transpose(slot.reshape(n, tm, TOP_K), (0, 2, 1)).reshape(n, TOP_K * tm)
    return pl.pallas_call(
        functools.partial(_combine_kernel, final=final),
        grid=(n,),
        in_specs=[pl.BlockSpec(memory_space=pl.ANY), pl.BlockSpec(memory_space=pl.ANY),
                  pl.BlockSpec((tm, d), lambda i: (i, 0)),
                  pl.BlockSpec((tm, 128), lambda i: (i, 0)),
                  pl.BlockSpec((None, 1, d), lambda i: (jnp.where(i < nlt * f, i // (tpb * f), nb), 0, 5)),
                  pl.BlockSpec((1, d), lambda i: (0, 0))],
        out_specs=pl.BlockSpec((tm, d), lambda i: (i, 0)),
        out_shape=jax.ShapeDtypeStruct((n * tm, d), F32),
        scratch_shapes=[pltpu.SMEM((TOP_K * tm,), jnp.int32),
                        pltpu.VMEM((2, TOP_K * tm, d), F32),
                        pltpu.SemaphoreType.DMA(()), pltpu.SemaphoreType.DMA((2,))],
        compiler_params=_cp("arbitrary"),
        name="combine",
    )(p, ys, x_all, slab, mod3, g_final.reshape(1, d))


def kernel(x, c, ctx, c_ctx, w_mod, b_mod, g_norm1, g_norm2, g_final, w_in, w_merge, w_branch, w_out, w_hconv,
           hy_bias, hf_w1, hf_b1, hf_w2, hf_b2, hf_w3, hf_freq, w_sconv, w_lconv, lru_wa, lru_ba, lru_wx, lru_bx,
           lru_lambda, w_rg, b_rg, w_re, b_re, w_e_gate, w_e_up, w_e_down):
    nb, seq, d = x.shape
    cl = ctx.shape[1]
    depth = w_in.shape[0]
    db = w_branch.shape[2]
    gdim = db // FNET_GROUPS
    ngroups = w_rg.shape[2]
    epg = w_re.shape[3]
    n_experts = ngroups * epg
    t_lat, t_ctx = nb * seq, nb * cl
    t_all = t_lat + t_ctx
    nlt, tpb = t_lat // TM, seq // TM
    nat = t_all // TM
    tl_lat, tl_ctx = min(TL, seq), min(TL, cl)

    x_all = jnp.concatenate([x.reshape(t_lat, d), ctx.reshape(t_ctx, d)], axis=0)
    nrows = -(-(nb + 1) // 8) * 8
    cvec = jnp.zeros((nrows, d), F32).at[:nb].set(c).at[nb].set(c_ctx)
    mods = _adaln(cvec, w_mod, b_mod)

    use_fft = nb % 2 == 0 and seq % (8 * FFT_N2) == 0
    if use_fft:
        hyfft = _fft_consts(seq)
    else:
        c_lat, sf_lat, si_lat = _dft_mats(seq)
        fwd_lat, inv_lat = (c_lat, sf_lat), (c_lat, si_lat)
    c_ctx, sf_ctx, si_ctx = _dft_mats(cl)
    fwd_ctx, inv_ctx = (c_ctx, sf_ctx), (c_ctx, si_ctx)
    fc_lat, fs_lat, w_pq = _fnet_mats(seq, gdim, FNET_GROUPS)
    fc_ctx, fs_ctx, _ = _fnet_mats(cl, gdim, FNET_GROUPS)

    out = None
    for l in range(depth):
        last = l == depth - 1
        mod3 = mods[l].reshape(nrows, 1, 6 * d)
        u_all = _inproj(x_all, mod3, g_norm1[l], w_in[l].astype(BF16), nlt, tpb, nb)

        wa = [_block_diag(lru_wa[l, dd]) for dd in range(2)]
        wx = [_block_diag(lru_wx[l, dd]) for dd in range(2)]
        ba = [lru_ba[l, dd].reshape(1, db) for dd in range(2)]
        bx = [lru_bx[l, dd].reshape(1, db) for dd in range(2)]
        lam = [lru_lambda[l, dd].reshape(1, db) for dd in range(2)]
        zeros_st = jnp.zeros((nb, 1, db), F32)
        rows_out = t_lat if last else t_all

        def lru(row0, length, tl, rw, h0f, h0b, combine, out_init):
            hf, stf = _lru_pass(u_all, w_lconv[l], wa[0], wx[0], ba[0], bx[0], lam[0], h0f,
                                row0, nb, length, tl, rw, db, False)
            if combine:
                y, stb = _lru_pass(u_all, w_lconv[l], wa[1], wx[1], ba[1], bx[1], lam[1], h0b,
                                   row0, nb, length, tl, rw, db, True, hf=hf, out_init=out_init, t_all=rows_out)
            else:
                y, stb = _lru_pass(u_all, w_lconv[l], wa[1], wx[1], ba[1], bx[1], lam[1], h0b,
                                   row0, nb, length, tl, rw, db, True)
            return y, stf, stb

        bias0, bias1 = hy_bias[l, 0].reshape(1, db), hy_bias[l, 1].reshape(1, db)

        def hyena_dense(row0, length, fwd, inv, v, x1, x2, v_bf, out_init):
            filt = _hyena_filters(length, hf_w1[l], hf_b1[l], hf_w2[l], hf_b2[l], hf_w3[l], hf_freq[l], db)
            pqr = _filter_spectra(fwd, filt, length, db)
            yhat = _hy_fwd(fwd, v_bf, nb, length, db, pqr[0])
            z2 = _hy_inv(inv, yhat, v, 0, x1, 0, bias0, nb, length, db, 0, nb * length, None)
            yhat = _hy_fwd(fwd, z2, nb, length, db, pqr[1])
            return _hy_inv(inv, yhat, z2, 0, x2, 0, bias1, nb, length, db, row0, rows_out, out_init)

        def hyena_fft(length, v, x1, x2, v_bf, out_init):
            filt = _hyena_filters(length, hf_w1[l], hf_b1[l], hf_w2[l], hf_b2[l], hf_w3[l], hf_freq[l], db,
                                  reverse_bwd=True)
            kspec = _fft_filter_spectra(hyfft, filt, length, db)
            z2 = _fft_conv(hyfft, v_bf, kspec[0], v, x1, bias0, nb, db, None, nb * length)
            return _fft_conv(hyfft, z2, kspec[1], z2, x2, bias1, nb, db, out_init, rows_out)

        zbuf = lambda: None if last else jnp.zeros((rows_out, db), BF16)
        if last:
            _, st_f, st_b = lru(t_lat, cl, tl_ctx, cl, zeros_st, zeros_st, False, None)
        else:
            y_lru_c, st_f, st_b = lru(t_lat, cl, tl_ctx, cl, zeros_st, zeros_st, True, zbuf())
        y_lru, _, _ = lru(0, seq, tl_lat, GRID_W, st_f, st_b, True, None if last else y_lru_c)

        pq = _fnet_pq(u_all, w_pq, db)
        v, x1, x2, v_bf, y_sc = _conv_stage(u_all, w_hconv[l], w_sconv[l], 0, t_lat, tl_lat, GRID_W, db,
                                            jnp.zeros((rows_out, db), BF16))
        y_fn = _fnet_dft(fc_lat, fs_lat, pq, 0, nb, seq, db, gdim, zbuf(), rows_out)
        if not last:
            vc, x1c, x2c, vc_bf, y_sc = _conv_stage(u_all, w_hconv[l], w_sconv[l], t_lat, t_ctx, tl_ctx, cl, db,
                                                    y_sc)
            y_fn = _fnet_dft(fc_ctx, fs_ctx, pq, t_lat, nb, cl, db, gdim, y_fn, rows_out)
            y_hy = hyena_dense(t_lat, cl, fwd_ctx, inv_ctx, vc, x1c, x2c, vc_bf, zbuf())
        else:
            y_hy = None
        if use_fft:
            y_hy = hyena_fft(seq, v, x1, x2, v_bf, y_hy)
        else:
            y_hy = hyena_dense(0, seq, fwd_lat, inv_lat, v, x1, x2, v_bf, y_hy)

        ntiles = nlt if last else nat
        x_mid = _merge((y_fn, y_hy, y_sc, y_lru), w_merge[l].astype(BF16), w_branch[l].astype(BF16),
                       w_out[l].astype(BF16), x_all, mod3, g_norm1[l], ntiles, nlt, tpb, nb)

        w_router = jnp.zeros((d, 128), F32).at[:, :ngroups].set(w_rg[l])
        w_router = w_router.at[:, ngroups:ngroups + n_experts].set(
            jnp.transpose(w_re[l], (1, 0, 2)).reshape(d, n_experts))
        b_router = jnp.zeros((1, 128), F32).at[0, :ngroups].set(b_rg[l])
        b_router = b_router.at[0, ngroups:ngroups + n_experts].set(b_re[l].reshape(-1))
        h2, slab, tile_counts = _router(x_mid, mod3, g_norm2[l], w_router, b_router, ntiles, nlt, tpb, nb,
                                        ngroups, epg)
        slot, tile_expert, n_valid, n_slot_tiles = _route_tables(slab, tile_counts, ngroups, n_experts, TM_MOE)
        xs = _dispatch(h2, slot, n_slot_tiles * TM_MOE)
        ys = _moe_experts(xs, tile_expert, n_valid, _layer_bf16(w_e_gate, l), _layer_bf16(w_e_up, l),
                          _layer_bf16(w_e_down, l), TM_MOE)
        x_all = _combine(ys, slot, slab, x_mid, mod3, g_final, ntiles, nlt, tpb, nb, last)
        if last:
            out = x_all.reshape(nb, seq, d)
    return out
```

```python
import functools
import math

import jax
import jax.numpy as jnp
from jax import lax
from jax.experimental import pallas as pl
from jax.experimental.pallas import tpu as pltpu

F32 = jnp.float32
BF16 = jnp.bfloat16
HIGHEST = lax.Precision.HIGHEST

NORM_EPS = 1e-6
GRID_W = 64
FNET_GROUPS = 4
LRU_C = 8.0
HY_DECAY_TARGET = 1e-2
HY_FAST_PCT = 0.3
HY_SLOW_PCT = 1.5
TOP_K = 2

TM = 512
TL = 512
TR = 512
TM_MOE = 512
TM_COMB = 256
TN_IN = 1536
TN_MERGE = 512
TF_MOE = 512
VMEM_LIMIT = 56 * 1024 * 1024


def _cp(*sem):
    return pltpu.CompilerParams(dimension_semantics=sem, vmem_limit_bytes=VMEM_LIMIT)


def _sigmoid(x):
    return 0.5 * jnp.tanh(0.5 * x) + 0.5


def _adaln_kernel(c_ref, w_ref, b_ref, o_ref):
    c = c_ref[...]
    s = c * _sigmoid(c)
    o_ref[...] = jnp.dot(s, w_ref[...], preferred_element_type=F32, precision=HIGHEST) + b_ref[...]


def _adaln(cvec, w_mod, b_mod):
    nl, d, n6 = w_mod.shape
    r = cvec.shape[0]
    tn = min(1024, n6)
    return pl.pallas_call(
        _adaln_kernel,
        grid=(nl, n6 // tn),
        in_specs=[pl.BlockSpec((r, d), lambda l, j: (0, 0)),
                  pl.BlockSpec((None, d, tn), lambda l, j: (l, 0, j)),
                  pl.BlockSpec((None, 1, tn), lambda l, j: (l, 0, j))],
        out_specs=pl.BlockSpec((None, r, tn), lambda l, j: (l, 0, j)),
        out_shape=jax.ShapeDtypeStruct((nl, r, n6), F32),
        compiler_params=_cp("parallel", "parallel"),
        name="adaln",
    )(cvec, w_mod, b_mod.reshape(nl, 1, n6))


def _norm_mod(x, g, shift, scale):
    y = x * lax.rsqrt(jnp.mean(x * x, axis=-1, keepdims=True) + NORM_EPS) * g
    return y * (1.0 + scale) + shift


def _norm_mod_store(dst, x_ref, g_ref, sh_ref, sc_ref, rows=128):
    rows = min(rows, dst.shape[0])

    def body(c, carry):
        sl = pl.ds(pl.multiple_of(c * rows, rows), rows)
        dst[sl, :] = _norm_mod(x_ref[sl, :], g_ref[...], sh_ref[...], sc_ref[...]).astype(dst.dtype)
        return carry

    lax.fori_loop(0, dst.shape[0] // rows, body, 0)


def _mod_spec(d, chunk, nlt, tpb, nb):
    return pl.BlockSpec((None, 1, d), lambda i, *_: (jnp.where(i < nlt, i // tpb, nb), 0, chunk))


def _inproj_kernel(x_ref, sh_ref, sc_ref, g_ref, w_ref, u_ref, h_ref, *, tn):
    j = pl.program_id(1)

    @pl.when(j == 0)
    def _():
        _norm_mod_store(h_ref, x_ref, g_ref, sh_ref, sc_ref)

    for jc in range(w_ref.shape[1] // tn):
        @pl.when(j == jc)
        def _(jc=jc):
            u_ref[...] = jnp.dot(h_ref[...], w_ref[:, jc * tn:(jc + 1) * tn], preferred_element_type=F32)


def _inproj(x_all, mod3, g, w_bf, nlt, tpb, nb):
    t, d = x_all.shape
    n = w_bf.shape[1]
    tn = TN_IN if n % TN_IN == 0 else n
    return pl.pallas_call(
        functools.partial(_inproj_kernel, tn=tn),
        grid=(t // TM, n // tn),
        in_specs=[pl.BlockSpec((TM, d), lambda i, j: (i, 0)),
                  _mod_spec(d, 0, nlt, tpb, nb), _mod_spec(d, 1, nlt, tpb, nb),
                  pl.BlockSpec((1, d), lambda i, j: (0, 0)),
                  pl.BlockSpec((d, n), lambda i, j: (0, 0))],
        out_specs=[pl.BlockSpec((TM, tn), lambda i, j: (i, j)), pl.BlockSpec((TM, d), lambda i, j: (i, 0))],
        out_shape=[jax.ShapeDtypeStruct((t, n), F32), jax.ShapeDtypeStruct((t, d), BF16)],
        compiler_params=_cp("parallel", "arbitrary"),
        name="inproj",
    )(x_all, mod3, mod3, g.reshape(1, d), w_bf)


def _dwconv(x, w_ref, left, rw):
    tl = x.shape[0]
    pos = lax.broadcasted_iota(jnp.int32, x.shape, 0) % rw
    acc = None
    for k in range(w_ref.shape[0]):
        off = k - left
        if off == 0:
            term = x
        else:
            shifted = pltpu.roll(x, (-off) % tl, axis=0)
            ok = (pos + off >= 0) & (pos + off < rw)
            term = jnp.where(ok, shifted, 0.0)
        term = term * w_ref[k:k + 1, :]
        acc = term if acc is None else acc + term
    return acc


def _conv_kernel(hv_ref, h1_ref, h2_ref, sb_ref, sc_ref, sh_ref, wh_ref, ws_ref, ys_in,
                 v_ref, x1_ref, x2_ref, vbf_ref, ys_ref, *, rw, db):
    del ys_in
    for k, (ref, dst) in enumerate(((hv_ref, v_ref), (h1_ref, x1_ref), (h2_ref, x2_ref))):
        y = _dwconv(ref[...], wh_ref.at[:, k * db:(k + 1) * db], 1, rw)
        dst[...] = y
        if k == 0:
            vbf_ref[...] = y.astype(BF16)
    ys_ref[...] = (sb_ref[...] * _dwconv(sc_ref[...] * sh_ref[...], ws_ref, 1, rw)).astype(BF16)


def _conv_stage(u_all, w_hconv, w_sconv, row0, nrows, tl, rw, db, ys_init):
    ob = row0 // tl
    col = lambda c: pl.BlockSpec((tl, db), lambda i: (ob + i, c))
    own = pl.BlockSpec((tl, db), lambda i: (i, 0))
    return pl.pallas_call(
        functools.partial(_conv_kernel, rw=rw, db=db),
        grid=(nrows // tl,),
        in_specs=[col(1), col(2), col(3), col(4), col(5), col(6),
                  pl.BlockSpec(w_hconv.shape, lambda i: (0, 0)),
                  pl.BlockSpec(w_sconv.shape, lambda i: (0, 0)),
                  pl.BlockSpec(memory_space=pl.ANY)],
        out_specs=[own, own, own, own, pl.BlockSpec((tl, db), lambda i: (ob + i, 0))],
        out_shape=[jax.ShapeDtypeStruct((nrows, db), F32)] * 3
                  + [jax.ShapeDtypeStruct((nrows, db), BF16), jax.ShapeDtypeStruct(ys_init.shape, BF16)],
        input_output_aliases={8: 4},
        compiler_params=_cp("parallel"),
        name="conv",
    )(*([u_all] * 6), w_hconv, w_sconv, ys_init)


def _trig_outer(l, period):
    q = 1 << ((l.bit_length() - 1) // 2)
    n = lax.broadcasted_iota(jnp.int32, (1, l), 1)
    scale = 2.0 * math.pi / period

    def table(rows, step):
        r = lax.broadcasted_iota(jnp.int32, (rows, 1), 0) * step
        ang = ((r * n) % period).astype(F32) * scale
        return jnp.cos(ang), jnp.sin(ang)

    ac, as_ = table(l // q, q)
    bc, bs = table(q, 1)
    c = ac[:, None, :] * bc[None] - as_[:, None, :] * bs[None]
    s = as_[:, None, :] * bc[None] + ac[:, None, :] * bs[None]
    return c.reshape(l, l), s.reshape(l, l)


def _dft_mats(l):
    c, s = _trig_outer(l, 2 * l)
    k = lax.broadcasted_iota(jnp.int32, (l, l), 0)
    n = lax.broadcasted_iota(jnp.int32, (l, l), 1)
    alt_n = jnp.where(n % 2 == 0, 1.0, -1.0).astype(F32)
    alt_k = jnp.where(k % 2 == 0, 1.0, -1.0).astype(F32)
    return c.astype(BF16), jnp.where(k == 0, alt_n, -s).astype(BF16), jnp.where(n == 0, alt_k, -s).astype(BF16)


def _fnet_mats(l, gdim, groups):
    c, s = _trig_outer(l, l)
    cg, sg = _trig_outer(gdim, gdim)
    eye = jnp.eye(groups, dtype=F32)
    w_pq = jnp.concatenate([jnp.kron(eye, cg), jnp.kron(eye, sg)], axis=1).astype(BF16)
    return c.astype(BF16), (-s).astype(BF16), w_pq


def _fnet_pq_kernel(u_ref, w_ref, o_ref):
    o_ref[...] = jnp.dot(u_ref[...].astype(BF16), w_ref[...], preferred_element_type=F32).astype(BF16)


def _fnet_pq(u_all, w_pq, db):
    t = u_all.shape[0]
    return pl.pallas_call(
        _fnet_pq_kernel,
        grid=(t // TM,),
        in_specs=[pl.BlockSpec((TM, db), lambda i: (i, 0)),
                  pl.BlockSpec(w_pq.shape, lambda i: (0, 0))],
        out_specs=pl.BlockSpec((TM, 2 * db), lambda i: (i, 0)),
        out_shape=jax.ShapeDtypeStruct((t, 2 * db), BF16),
        compiler_params=_cp("parallel"),
        name="fnet_pq",
    )(u_all, w_pq)


def _fnet_dft_kernel(d0_ref, d1_ref, x_ref, *rest, db, scale):
    o_ref = rest[-1]
    acc = jnp.dot(d0_ref[...], x_ref[:, :db], preferred_element_type=F32)
    acc = acc + jnp.dot(d1_ref[...], x_ref[:, db:], preferred_element_type=F32)
    o_ref[...] = (acc * scale).astype(BF16)


def _mat_spec(tr, l):
    return pl.BlockSpec((tr, l), lambda i, b: (i, 0))


def _fnet_dft(d0, d1, pq, row0, nb, l, db, gdim, out_init, t_all):
    tr = min(TR, l)
    nrt = l // tr
    args = [d0, d1, pq]
    in_specs = [_mat_spec(tr, l), _mat_spec(tr, l),
                pl.BlockSpec((l, 2 * db), lambda i, b: (row0 // l + b, 0))]
    aliases = {}
    if out_init is not None:
        args.append(out_init)
        in_specs.append(pl.BlockSpec(memory_space=pl.ANY))
        aliases = {3: 0}
    return pl.pallas_call(
        functools.partial(_fnet_dft_kernel, db=db, scale=1.0 / math.sqrt(l * gdim)),
        grid=(nrt, nb),
        in_specs=in_specs,
        out_specs=pl.BlockSpec((tr, db), lambda i, b: (row0 // tr + b * nrt + i, 0)),
        out_shape=jax.ShapeDtypeStruct((t_all, db), BF16),
        input_output_aliases=aliases,
        compiler_params=_cp("parallel", "parallel"),
        name="fnet_dft",
    )(*args)


def _hy_fwd_kernel(f0_ref, f1_ref, x_ref, *rest, mult):
    o_ref = rest[-1]
    zre = jnp.dot(f0_ref[...], x_ref[...], preferred_element_type=F32)
    zim = jnp.dot(f1_ref[...], x_ref[...], preferred_element_type=F32)
    if mult:
        k_ref = rest[0]
        p, q, r = k_ref[0], k_ref[1], k_ref[2]
        o_ref[0] = (zre * p - zim * q).astype(o_ref.dtype)
        o_ref[1] = (zre * q + zim * r).astype(o_ref.dtype)
    else:
        o_ref[0] = zre
        o_ref[1] = zim


def _hy_fwd(fwd, x, nb, l, db, pqr):
    tr = min(TR, l)
    args = [fwd[0], fwd[1], x]
    in_specs = [_mat_spec(tr, l), _mat_spec(tr, l),
                pl.BlockSpec((l, db), lambda i, b: (b, 0))]
    if pqr is not None:
        args.append(pqr)
        in_specs.append(pl.BlockSpec((3, tr, db), lambda i, b: (0, i, 0)))
    return pl.pallas_call(
        functools.partial(_hy_fwd_kernel, mult=pqr is not None),
        grid=(l // tr, nb),
        in_specs=in_specs,
        out_specs=pl.BlockSpec((None, 2, tr, db), lambda i, b: (b, 0, i, 0)),
        out_shape=jax.ShapeDtypeStruct((nb, 2, l, db), F32 if pqr is None else BF16),
        compiler_params=_cp("parallel", "parallel"),
        name="hy_fwd",
    )(*args)


def _hy_inv_kernel(g0_ref, g1_ref, y_ref, zp_ref, gate_ref, bias_ref, *rest):
    o_ref = rest[-1]
    y = jnp.dot(g0_ref[...], y_ref[0], preferred_element_type=F32)
    y = y + jnp.dot(g1_ref[...], y_ref[1], preferred_element_type=F32)
    o_ref[...] = (gate_ref[...] * (y + bias_ref[...] * zp_ref[...].astype(F32))).astype(BF16)


def _hy_inv(inv, yhat, zprev, zcol, gate, gcol, bias, nb, l, db, out_row0, out_rows, out_init):
    tr = min(TR, l)
    nrt = l // tr
    args = [inv[0], inv[1], yhat, zprev, gate, bias]
    in_specs = [_mat_spec(tr, l), _mat_spec(tr, l),
                pl.BlockSpec((None, 2, l, db), lambda i, b: (b, 0, 0, 0)),
                pl.BlockSpec((tr, db), lambda i, b: (b * nrt + i, zcol)),
                pl.BlockSpec((tr, db), lambda i, b: (b * nrt + i, gcol)),
                pl.BlockSpec((1, db), lambda i, b: (0, 0))]
    aliases = {}
    if out_init is not None:
        args.append(out_init)
        in_specs.append(pl.BlockSpec(memory_space=pl.ANY))
        aliases = {6: 0}
    return pl.pallas_call(
        _hy_inv_kernel,
        grid=(nrt, nb),
        in_specs=in_specs,
        out_specs=pl.BlockSpec((tr, db), lambda i, b: (out_row0 // tr + b * nrt + i, 0)),
        out_shape=jax.ShapeDtypeStruct((out_rows, db), BF16),
        input_output_aliases=aliases,
        compiler_params=_cp("parallel", "parallel"),
        name="hy_inv",
    )(*args)


def _hyena_filters(length, hf_w1, hf_b1, hf_w2, hf_b2, hf_w3, hf_freq, db, reverse_bwd=False):
    emb = hf_w1.shape[0]
    nbands = (emb - 1) // 2
    bands = jnp.linspace(1e-4, nbands - 1, nbands, dtype=F32)[None, :]
    min_decay = math.log(HY_DECAY_TARGET) / HY_SLOW_PCT
    max_decay = math.log(HY_DECAY_TARGET) / HY_FAST_PCT
    deltas = jnp.abs(jnp.linspace(min_decay, max_decay, db, dtype=F32))
    w3 = hf_w3.reshape(hf_w3.shape[0], -1, 2, db)

    def direction(pos, d):
        t = (pos * (1.0 / (length - 1)))[:, None]
        w = (2.0 * math.pi / length) * pos[:, None]
        feats = jnp.concatenate([t, jnp.cos(bands * w), -jnp.sin(bands * w)], axis=-1)
        z = jnp.sin(hf_freq[0] * (jnp.dot(feats, hf_w1, precision=HIGHEST) + hf_b1))
        z = jnp.sin(hf_freq[1] * (jnp.dot(z, hf_w2, precision=HIGHEST) + hf_b2))
        f = jnp.einsum("lf,fod->lod", z, w3[:, :, d, :], precision=HIGHEST)
        return f * jnp.exp(-t * deltas)[:, None, :]

    pos = jnp.arange(length, dtype=F32)
    filt = jnp.stack([direction(pos, 0), direction(length - 1 - pos if reverse_bwd else pos, 1)], axis=2)
    return filt * lax.rsqrt(jnp.sum(filt * filt, axis=(0, 2), keepdims=True) + NORM_EPS)


def _filter_spectra(fwd, filt, l, db):
    n_ord = filt.shape[1]
    cols = jnp.transpose(filt, (1, 2, 0, 3))
    cols = cols.at[:, 1, 0, :].set(0.0)
    x = cols.reshape(n_ord * 2 * l, db).astype(BF16)
    spec = _hy_fwd(fwd, x, n_ord * 2, l, db, None).reshape(n_ord, 2, 2, l, db)
    hf, hb = spec[:, 0], spec[:, 1]
    k_re = hf[:, 0] + hb[:, 0]
    k_im = hf[:, 1] - hb[:, 1]
    k_nyq = hf[:, 1, 0] + hb[:, 1, 0]
    first = (jnp.arange(l) == 0)[None, :, None]
    scale = jnp.where(first, 1.0 / (2 * l), 2.0 / (2 * l)).astype(F32)
    p = k_re * scale
    q = jnp.where(first, 0.0, k_im) * scale
    r = jnp.where(first, k_nyq[:, None, :], k_re) * scale
    return jnp.stack([p, q, r], axis=1)


FFT_N2 = 128
FFT_NT = 16
FFT_KC = 8


def _fft_consts(l):
    n, n2 = 2 * l, FFT_N2
    n1 = n // n2
    h = n1 // 2
    cat = jnp.concatenate
    ia = jnp.arange(n1, dtype=jnp.int32)
    ang = ((ia[:, None] * ia[None, :]) % n1).astype(F32) * (2.0 * math.pi / n1)
    fr, fi = jnp.cos(ang), -jnp.sin(ang)
    mat_data = cat([cat([fr[:, :h], -fi[:, :h]], 1), cat([fi[:, :h], fr[:, :h]], 1)], 0)
    mat_filt = cat([fr, fi], 0)
    gr, gi = fr[:, :h].T, -fi[:, :h].T
    mat_inv = jnp.stack([cat([gr, -gi], 1), cat([gi, gr], 1)])
    k = ia[:, None, None] + n1 * jnp.arange(n2, dtype=jnp.int32)[None, :, None]
    nn = jnp.arange(n2, dtype=jnp.int32)[None, None, :]
    angb = ((k * nn) % n).astype(F32) * (2.0 * math.pi / n)
    er, ei = jnp.cos(angb), -jnp.sin(angb)
    mb = cat([cat([er, -ei], 2), cat([ei, er], 2)], 1)
    eye = jnp.eye(FFT_NT, dtype=F32)
    kron = lambda a: jnp.kron(a, eye).astype(BF16)
    return dict(n1=n1, h=h, mat_data=kron(mat_data), mat_filt=kron(mat_filt * (1.0 / n)),
                mat_inv=jnp.stack([kron(mat_inv[0]), kron(mat_inv[1])]),
                mb=mb.astype(BF16), mib=jnp.transpose(mb, (0, 2, 1)).astype(BF16))


def _fft_a_kernel(m_ref, a_ref, b_ref, o_ref):
    h, nt, db = a_ref.shape
    x = jnp.concatenate([a_ref[...].reshape(h * nt, db), b_ref[...].reshape(h * nt, db)], axis=0)
    res = jnp.dot(m_ref[...], x, preferred_element_type=F32)
    o_ref[...] = res.astype(o_ref.dtype).reshape(o_ref.shape)


def _fft_stage_a(mat, x4, npairs, stride, offset):
    _, h, n2, db = x4.shape
    nt = FFT_NT
    rows = mat.shape[0] // nt
    return pl.pallas_call(
        _fft_a_kernel,
        grid=(npairs, n2 // nt),
        in_specs=[pl.BlockSpec(mat.shape, lambda p, j: (0, 0)),
                  pl.BlockSpec((None, h, nt, db), lambda p, j: (p * stride, 0, j, 0)),
                  pl.BlockSpec((None, h, nt, db), lambda p, j: (p * stride + offset, 0, j, 0))],
        out_specs=pl.BlockSpec((None, rows, nt, db), lambda p, j: (p, 0, j, 0)),
        out_shape=jax.ShapeDtypeStruct((npairs, rows, n2, db), BF16),
        compiler_params=_cp("parallel", "parallel"),
        name="fft_a",
    )(mat, x4, x4)


def _fft_b_kernel(m_ref, ar_ref, ai_ref, o_ref):
    for kk in range(m_ref.shape[0]):
        a = jnp.concatenate([ar_ref[kk], ai_ref[kk]], axis=0)
        o_ref[kk] = jnp.dot(m_ref[kk], a, preferred_element_type=F32)


def _fft_bb_kernel(m_ref, mi_ref, ar_ref, ai_ref, k_ref, o_ref):
    n2 = ar_ref.shape[1]
    for kk in range(m_ref.shape[0]):
        a = jnp.concatenate([ar_ref[kk], ai_ref[kk]], axis=0)
        x = jnp.dot(m_ref[kk], a, preferred_element_type=F32)
        xr, xi = x[:n2], x[n2:]
        kr, ki = k_ref[kk, :n2], k_ref[kk, n2:]
        y = jnp.concatenate([(xr * kr - xi * ki).astype(BF16), (xr * ki + xi * kr).astype(BF16)], axis=0)
        b = jnp.dot(mi_ref[kk], y, preferred_element_type=F32)
        o_ref[0, kk] = b[:n2].astype(o_ref.dtype)
        o_ref[1, kk] = b[n2:].astype(o_ref.dtype)


def _fft_stage_b(fc, a, kspec, db):
    mb, mib = fc["mb"], fc["mib"]
    npairs = a.shape[0]
    n1, r2, _ = mb.shape
    n2 = r2 // 2
    kc = min(FFT_KC, n1)
    a5 = a.reshape(npairs, 2, n1, n2, db)
    mspec = pl.BlockSpec((kc, r2, r2), lambda c, p: (c, 0, 0))
    aspec = lambda part: pl.BlockSpec((None, None, kc, n2, db), lambda c, p: (p, part, c, 0, 0))
    if kspec is None:
        kern, args, in_specs = _fft_b_kernel, [mb, a5, a5], [mspec, aspec(0), aspec(1)]
        out_spec = pl.BlockSpec((None, kc, r2, db), lambda c, p: (p, c, 0, 0))
        out_shape = jax.ShapeDtypeStruct((npairs, n1, r2, db), F32)
    else:
        kern, args = _fft_bb_kernel, [mb, mib, a5, a5, kspec]
        in_specs = [mspec, mspec, aspec(0), aspec(1), pl.BlockSpec((kc, r2, db), lambda c, p: (c, 0, 0))]
        out_spec = pl.BlockSpec((None, 2, kc, n2, db), lambda c, p: (p, 0, c, 0, 0))
        out_shape = jax.ShapeDtypeStruct((npairs, 2, n1, n2, db), BF16)
    return pl.pallas_call(
        kern,
        grid=(n1 // kc, npairs),
        in_specs=in_specs,
        out_specs=out_spec,
        out_shape=out_shape,
        compiler_params=_cp("parallel", "parallel"),
        name="fft_b",
    )(*args)


def _ifft_a_kernel(m_ref, b_ref, zp_ref, gate_ref, bias_ref, *rest):
    o_ref = rest[-1]
    h, nt, db = o_ref.shape
    bm = b_ref[...].reshape(b_ref.shape[0] * nt, db)
    y = jnp.dot(m_ref[...], bm, preferred_element_type=F32)
    zp = zp_ref[...].reshape(h * nt, db).astype(F32)
    out = gate_ref[...].reshape(h * nt, db) * (y + bias_ref[...] * zp)
    o_ref[...] = out.astype(o_ref.dtype).reshape(o_ref.shape)


def _ifft_stage_a(mat_inv, b5, zprev, gate, bias, nb, db, out_init, out_rows):
    npairs, _, n1, n2, _ = b5.shape
    nt = FFT_NT
    h = mat_inv.shape[1] // nt
    b4 = b5.reshape(npairs, 2 * n1, n2, db)
    view = lambda a: a.reshape(a.shape[0] // n2, n2, db)
    blk = pl.BlockSpec((h, nt, db), lambda b, j: (b, j, 0))
    args = [mat_inv, b4, view(zprev), view(gate), bias]
    in_specs = [pl.BlockSpec((None,) + mat_inv.shape[1:], lambda b, j: (b // npairs, 0, 0)),
                pl.BlockSpec((None, 2 * n1, nt, db), lambda b, j: (b % npairs, 0, j, 0)),
                blk, blk, pl.BlockSpec((1, db), lambda b, j: (0, 0))]
    aliases = {}
    if out_init is not None:
        args.append(view(out_init))
        in_specs.append(pl.BlockSpec(memory_space=pl.ANY))
        aliases = {5: 0}
    out = pl.pallas_call(
        _ifft_a_kernel,
        grid=(nb, n2 // nt),
        in_specs=in_specs,
        out_specs=blk,
        out_shape=jax.ShapeDtypeStruct((out_rows // n2, n2, db), BF16),
        input_output_aliases=aliases,
        compiler_params=_cp("parallel", "parallel"),
        name="ifft_a",
    )(*args)
    return out.reshape(out_rows, db)


def _fft_filter_spectra(fc, filt, l, db):
    n_ord = filt.shape[1]
    h_fwd = jnp.transpose(filt[:, :, 0, :], (1, 0, 2))
    h_bwd = jnp.transpose(filt[:, :, 1, :], (1, 0, 2))
    ker = jnp.concatenate([h_fwd, jnp.zeros((n_ord, 1, db), F32), h_bwd[:, :l - 1, :]], axis=1)
    x4 = ker.astype(BF16).reshape(n_ord * 2, fc["h"], FFT_N2, db)
    a = _fft_stage_a(fc["mat_filt"], x4, n_ord, 2, 1)
    return _fft_stage_b(fc, a, None, db)


def _fft_conv(fc, z_bf, kspec, zprev, gate, bias, nb, db, out_init, out_rows):
    x4 = z_bf.reshape(nb, fc["h"], FFT_N2, db)
    a = _fft_stage_a(fc["mat_data"], x4, nb // 2, 1, nb // 2)
    b5 = _fft_stage_b(fc, a, kspec, db)
    return _ifft_stage_a(fc["mat_inv"], b5, zprev, gate, bias, nb, db, out_init, out_rows)


def _lru_kernel(*refs, rw, reverse, final):
    if final:
        (x_ref, w_ref, wa_ref, wx_ref, ba_ref, bx_ref, lam_ref, h0_ref, hf_ref, ug_ref,
         _, o_ref, st_ref, a_s, b_s, carry) = refs
    else:
        (x_ref, w_ref, wa_ref, wx_ref, ba_ref, bx_ref, lam_ref, h0_ref,
         o_ref, st_ref, a_s, b_s, carry) = refs
    tl = x_ref.shape[0]

    @pl.when(pl.program_id(1) == 0)
    def _():
        carry[...] = h0_ref[...]

    xc = _dwconv(x_ref[...], w_ref, w_ref.shape[0] // 2, rw)
    xb = xc.astype(BF16)
    r = _sigmoid(jnp.dot(xb, wa_ref[...], preferred_element_type=F32) + ba_ref[...])
    ig = _sigmoid(jnp.dot(xb, wx_ref[...], preferred_element_type=F32) + bx_ref[...])
    lam = lam_ref[...]
    softplus = jnp.maximum(-lam, 0.0) + jnp.log(1.0 + jnp.exp(-jnp.abs(lam)))
    log_a = -LRU_C * r * softplus
    a = jnp.exp(log_a)
    bv = jnp.sqrt(1.0 - jnp.exp(2.0 * log_a)) * (ig * xc)

    pos8 = lax.broadcasted_iota(jnp.int32, a.shape, 0) % 8
    for s in (1, 2, 4):
        if reverse:
            a_sh, b_sh, ok = pltpu.roll(a, tl - s, axis=0), pltpu.roll(bv, tl - s, axis=0), pos8 < 8 - s
        else:
            a_sh, b_sh, ok = pltpu.roll(a, s, axis=0), pltpu.roll(bv, s, axis=0), pos8 >= s
        bv = jnp.where(ok, a * b_sh + bv, bv)
        a = jnp.where(ok, a * a_sh, a)
    a_s[...] = a
    b_s[...] = bv

    ng = tl // 8

    def body(gi, c):
        g = ng - 1 - gi if reverse else gi
        sl = pl.ds(pl.multiple_of(g * 8, 8), 8)
        h8 = b_s[sl, :] + a_s[sl, :] * c
        b_s[sl, :] = h8
        return h8[0:1, :] if reverse else h8[7:8, :]

    c_out = lax.fori_loop(0, ng, body, carry[...])
    carry[...] = c_out
    st_ref[...] = c_out
    if final:
        ug = ug_ref[...]
        gelu = 0.5 * ug * (1.0 + jnp.tanh(math.sqrt(2.0 / math.pi) * (ug + 0.044715 * ug * ug * ug)))
        o_ref[...] = ((hf_ref[...] + b_s[...]) * gelu).astype(BF16)
    else:
        o_ref[...] = b_s[...]


def _lru_pass(u_all, w_lconv, wa, wx, ba, bx, lam, h0, row0, nb, l, tl, rw, db, reverse, hf=None, out_init=None,
              t_all=None):
    final = hf is not None
    nc = l // tl
    ob = row0 // tl
    cidx = (lambda c: nc - 1 - c) if reverse else (lambda c: c)
    full = lambda a: pl.BlockSpec(a.shape, lambda b, c: (0,) * a.ndim)
    ncolx, ncolg = 7, 8
    args = [u_all, w_lconv, wa, wx, ba, bx, lam, h0]
    in_specs = [pl.BlockSpec((tl, db), lambda b, c: (ob + b * nc + cidx(c), ncolx)),
                full(w_lconv), full(wa), full(wx), full(ba), full(bx), full(lam),
                pl.BlockSpec((None, 1, db), lambda b, c: (b, 0, 0))]
    aliases = {}
    if final:
        args += [hf, u_all]
        in_specs += [pl.BlockSpec((tl, db), lambda b, c: (b * nc + cidx(c), 0)),
                     pl.BlockSpec((tl, db), lambda b, c: (ob + b * nc + cidx(c), ncolg))]
        if out_init is None:
            out_init = jnp.zeros((8, 128), BF16)
        else:
            aliases = {10: 0}
        args.append(out_init)
        in_specs.append(pl.BlockSpec(memory_space=pl.ANY))
        out_spec = pl.BlockSpec((tl, db), lambda b, c: (ob + b * nc + cidx(c), 0))
        out_shape = jax.ShapeDtypeStruct((t_all, db), BF16)
    else:
        out_spec = pl.BlockSpec((tl, db), lambda b, c: (b * nc + cidx(c), 0))
        out_shape = jax.ShapeDtypeStruct((nb * l, db), F32)
    return pl.pallas_call(
        functools.partial(_lru_kernel, rw=rw, reverse=reverse, final=final),
        grid=(nb, nc),
        in_specs=in_specs,
        out_specs=[out_spec, pl.BlockSpec((None, 1, db), lambda b, c: (b, 0, 0))],
        out_shape=[out_shape, jax.ShapeDtypeStruct((nb, 1, db), F32)],
        scratch_shapes=[pltpu.VMEM((tl, db), F32), pltpu.VMEM((tl, db), F32), pltpu.VMEM((1, db), F32)],
        input_output_aliases=aliases,
        compiler_params=_cp("arbitrary", "arbitrary"),
        name="lru_bwd" if reverse else "lru_fwd",
    )(*args)


def _block_diag(w):
    h, hd, _ = w.shape
    eye = jnp.eye(h, dtype=w.dtype)
    return (eye[:, None, :, None] * w[:, :, None, :]).reshape(h * hd, h * hd).astype(BF16)


def _merge_kernel(x_ref, h_ref, b0_ref, b1_ref, b2_ref, b3_ref, wm_ref, wb_ref, wo_ref, ga_ref, o_ref):
    j = pl.program_id(1)
    h = h_ref[...]
    m = None
    for k, br in enumerate((b0_ref, b1_ref, b2_ref, b3_ref)):
        g = jnp.dot(h, wm_ref[k], preferred_element_type=F32)
        p = jnp.dot(br[...], wb_ref[k], preferred_element_type=F32)
        term = _sigmoid(g) * p
        m = term if m is None else m + term
    part = jnp.dot(m.astype(BF16), wo_ref[...], preferred_element_type=F32)

    @pl.when(j == 0)
    def _():
        o_ref[...] = part

    @pl.when(j > 0)
    def _():
        o_ref[...] += part

    @pl.when(j == pl.num_programs(1) - 1)
    def _():
        o_ref[...] = x_ref[...] + ga_ref[...] * o_ref[...]


def _merge(h_bf, branches, wm_bf, wb_bf, wo_bf, x_all, mod3, ntiles, nlt, tpb, nb):
    t, d = x_all.shape
    db = wb_bf.shape[1]
    tn = min(TN_MERGE, d)
    row = lambda w: pl.BlockSpec((TM, w), lambda i, j: (i, 0))
    return pl.pallas_call(
        _merge_kernel,
        grid=(ntiles, d // tn),
        in_specs=[row(d), row(d), row(db), row(db), row(db), row(db),
                  pl.BlockSpec((4, d, tn), lambda i, j: (0, 0, j)),
                  pl.BlockSpec((4, db, tn), lambda i, j: (0, 0, j)),
                  pl.BlockSpec((tn, d), lambda i, j: (j, 0)),
                  _mod_spec(d, 2, nlt, tpb, nb)],
        out_specs=row(d),
        out_shape=jax.ShapeDtypeStruct((ntiles * TM, d), F32),
        compiler_params=_cp("parallel", "arbitrary"),
        name="merge",
    )(x_all, h_bf, *branches, wm_bf, wb_bf, wo_bf, mod3)


def _router_kernel(x_ref, sh_ref, sc_ref, g_ref, whi_ref, wlo_ref, br_ref, tri_ref, h_ref, r_ref, c_ref, *,
                   ngroups, epg):
    h = _norm_mod(x_ref[...], g_ref[...], sh_ref[...], sc_ref[...])
    h_ref[...] = h
    h_hi = h.astype(BF16)
    h_lo = (h - h_hi.astype(F32)).astype(BF16)
    logits = jnp.dot(h_hi, wlo_ref[...], preferred_element_type=F32)
    logits = logits + jnp.dot(h_lo, whi_ref[...], preferred_element_type=F32)
    logits = logits + jnp.dot(h_hi, whi_ref[...], preferred_element_type=F32) + br_ref[...]
    lane = lax.broadcasted_iota(jnp.int32, logits.shape, 1)
    lane_f = lane.astype(F32)
    neg = jnp.float32(-1e30)
    big = jnp.float32(1e6)
    is_g = lane < ngroups
    lg = jnp.where(is_g, logits, neg)
    mx = jnp.max(lg, axis=-1, keepdims=True)
    g_top = jnp.min(jnp.where(lg == mx, lane_f, big), axis=-1, keepdims=True)
    den = jnp.sum(jnp.where(is_g, jnp.exp(lg - mx), 0.0), axis=-1, keepdims=True)
    p_top = 1.0 / den
    el = lane_f - ngroups
    in_grp = (el >= g_top * epg) & (el < (g_top + 1.0) * epg)
    le = jnp.where(in_grp, logits, neg)
    v1 = jnp.max(le, axis=-1, keepdims=True)
    e1 = jnp.min(jnp.where(le == v1, el, big), axis=-1, keepdims=True)
    le2 = jnp.where(el == e1, neg, le)
    v2 = jnp.max(le2, axis=-1, keepdims=True)
    e2 = jnp.min(jnp.where(le2 == v2, el, big), axis=-1, keepdims=True)
    dlt = jnp.exp(v2 - v1)
    w1 = p_top / (1.0 + dlt)
    w2 = p_top * dlt / (1.0 + dlt)
    tm = logits.shape[0]
    pick1 = el == e1
    pick2 = el == e2
    onehot = jnp.where(pick1 | pick2, 1.0, 0.0)
    before = jnp.dot(tri_ref[...], onehot.astype(BF16), preferred_element_type=F32)
    rank1 = jnp.sum(jnp.where(pick1, before, 0.0), axis=-1, keepdims=True)
    rank2 = jnp.sum(jnp.where(pick2, before, 0.0), axis=-1, keepdims=True)
    out = jnp.where(lane == 0, e1, 0.0)
    out = jnp.where(lane == 1, e2, out)
    out = jnp.where(lane == 2, w1, out)
    out = jnp.where(lane == 3, w2, out)
    out = jnp.where(lane == 4, rank1, out)
    out = jnp.where(lane == 5, rank2, out)
    r_ref[...] = out
    c_ref[...] = jnp.broadcast_to(jnp.sum(onehot, axis=0, keepdims=True), c_ref.shape)


def _router(x_all, mod3, g, w_router, b_router, ntiles, nlt, tpb, nb, ngroups, epg):
    t, d = x_all.shape
    w_hi = w_router.astype(BF16)
    w_hi_lo = (w_hi, (w_router - w_hi.astype(F32)).astype(BF16))
    earlier = jnp.tril(jnp.ones((TM, TM), F32), -1).astype(BF16)
    return pl.pallas_call(
        functools.partial(_router_kernel, ngroups=ngroups, epg=epg),
        grid=(ntiles,),
        in_specs=[pl.BlockSpec((TM, d), lambda i: (i, 0)),
                  _mod_spec(d, 3, nlt, tpb, nb), _mod_spec(d, 4, nlt, tpb, nb),
                  pl.BlockSpec((1, d), lambda i: (0, 0)),
                  pl.BlockSpec((d, 128), lambda i: (0, 0)),
                  pl.BlockSpec((d, 128), lambda i: (0, 0)),
                  pl.BlockSpec((1, 128), lambda i: (0, 0)),
                  pl.BlockSpec((TM, TM), lambda i: (0, 0))],
        out_specs=[pl.BlockSpec((TM, d), lambda i: (i, 0)), pl.BlockSpec((TM, 128), lambda i: (i, 0)),
                   pl.BlockSpec((None, 8, 128), lambda i: (i, 0, 0))],
        out_shape=[jax.ShapeDtypeStruct((ntiles * TM, d), F32), jax.ShapeDtypeStruct((ntiles * TM, 128), F32),
                   jax.ShapeDtypeStruct((ntiles, 8, 128), F32)],
        compiler_params=_cp("parallel"),
        name="router",
    )(x_all, mod3, mod3, g.reshape(1, d), w_hi, w_hi_lo[1], b_router, earlier)


def _route_tables(slab, tile_counts, ngroups, n_experts, tm):
    ntiles = tile_counts.shape[0]
    e = slab[:, 0:TOP_K].astype(jnp.int32).reshape(ntiles, TM, TOP_K)
    rank = slab[:, 4:4 + TOP_K].astype(jnp.int32).reshape(ntiles, TM, TOP_K)
    counts_t = tile_counts[:, 0, ngroups:ngroups + n_experts].astype(jnp.int32)
    base = jnp.cumsum(counts_t, axis=0) - counts_t
    counts = jnp.sum(counts_t, axis=0)
    tiles_per_e = (counts + tm - 1) // tm
    tile_end = jnp.cumsum(tiles_per_e)
    starts = (tile_end - tiles_per_e) * tm
    off = starts[None, :] + base
    sel = e[..., None] == jnp.arange(n_experts, dtype=jnp.int32)
    slot = jnp.sum(jnp.where(sel, off[:, None, None, :], 0), axis=-1) + rank
    n_slot_tiles = (ntiles * TM * TOP_K) // tm + n_experts
    n_valid = tile_end[-1]
    tile_ids = jnp.minimum(jnp.arange(n_slot_tiles, dtype=jnp.int32), n_valid - 1)
    tile_expert = jnp.sum((tile_end[None, :] <= tile_ids[:, None]).astype(jnp.int32), axis=1)
    return slot, tile_expert, n_valid.reshape(1).astype(jnp.int32), n_slot_tiles


ROW_DMA_UNROLL = 8


def _stage_indices(idx_hbm_row, idx_smem, sem_i):
    cp = pltpu.make_async_copy(idx_hbm_row, idx_smem, sem_i)
    cp.start()
    cp.wait()


def _dispatch_kernel(p_hbm, h_ref, xs_in, xs_hbm, idx, sem_i, sem_x):
    del xs_in
    i = pl.program_id(0)
    tm = h_ref.shape[0]
    _stage_indices(p_hbm.at[i], idx, sem_i)

    def issue(r, c):
        src = h_ref.at[pl.ds(r, 1), :]
        for k in range(TOP_K):
            pltpu.make_async_copy(src, xs_hbm.at[pl.ds(idx[k * tm + r], 1), :], sem_x).start()
        return c

    lax.fori_loop(0, tm, issue, 0, unroll=ROW_DMA_UNROLL)
    for k in range(TOP_K):
        pltpu.make_async_copy(h_ref, xs_hbm.at[pl.ds(0, tm), :], sem_x).wait()


def _dispatch(h_all, slot, n_slot_rows):
    ntiles = slot.shape[0]
    d = h_all.shape[1]
    p = jnp.transpose(slot, (0, 2, 1)).reshape(ntiles, TOP_K * TM)
    xs0 = jnp.zeros((n_slot_rows, d), F32)
    return pl.pallas_call(
        _dispatch_kernel,
        grid=(ntiles,),
        in_specs=[pl.BlockSpec(memory_space=pl.ANY), pl.BlockSpec((TM, d), lambda i: (i, 0)),
                  pl.BlockSpec(memory_space=pl.ANY)],
        out_specs=pl.BlockSpec(memory_space=pl.ANY),
        out_shape=jax.ShapeDtypeStruct((n_slot_rows, d), F32),
        scratch_shapes=[pltpu.SMEM((TOP_K * TM,), jnp.int32),
                        pltpu.SemaphoreType.DMA(()), pltpu.SemaphoreType.DMA(())],
        input_output_aliases={2: 0},
        compiler_params=_cp("arbitrary"),
        name="dispatch",
    )(p, h_all, xs0)


def _cast_kernel(w_ref, o_ref):
    o_ref[...] = w_ref[...].astype(o_ref.dtype)


def _layer_bf16(w_stack, l):
    _, e, a, b = w_stack.shape
    return pl.pallas_call(
        _cast_kernel,
        grid=(e,),
        in_specs=[pl.BlockSpec((None, None, a, b), lambda i: (l, i, 0, 0))],
        out_specs=pl.BlockSpec((None, a, b), lambda i: (i, 0, 0)),
        out_shape=jax.ShapeDtypeStruct((e, a, b), BF16),
        compiler_params=_cp("parallel"),
        name="cast_bf16",
    )(w_stack)


def _moe_kernel(te_ref, nv_ref, x_ref, wg_ref, wu_ref, wd_ref, o_ref):
    del te_ref
    valid = pl.program_id(0) < nv_ref[0]

    @pl.when(valid)
    def _():
        x = x_ref[...].astype(BF16)
        g = jnp.dot(x, wg_ref[...], preferred_element_type=F32)
        u = jnp.dot(x, wu_ref[...], preferred_element_type=F32)
        hid = (g * _sigmoid(g) * u).astype(BF16)
        o_ref[...] = jnp.dot(hid, wd_ref[...], preferred_element_type=F32)

    @pl.when(jnp.logical_not(valid))
    def _():
        o_ref[...] = jnp.zeros_like(o_ref)


def _moe_experts(xs, tile_expert, n_valid, w_gate, w_up, w_down, tm):
    n_rows, d = xs.shape
    f = w_gate.shape[2]
    xi = lambda i, te, nv: (jnp.maximum(jnp.minimum(i, nv[0] - 1), 0), 0)
    grid_spec = pltpu.PrefetchScalarGridSpec(
        num_scalar_prefetch=2,
        grid=(n_rows // tm,),
        in_specs=[pl.BlockSpec((tm, d), xi),
                  pl.BlockSpec((None, d, f), lambda i, te, nv: (te[i], 0, 0)),
                  pl.BlockSpec((None, d, f), lambda i, te, nv: (te[i], 0, 0)),
                  pl.BlockSpec((None, f, d), lambda i, te, nv: (te[i], 0, 0))],
        out_specs=pl.BlockSpec((tm, d), lambda i, te, nv: (i, 0)),
    )
    return pl.pallas_call(
        _moe_kernel,
        grid_spec=grid_spec,
        out_shape=jax.ShapeDtypeStruct((n_rows, d), F32),
        compiler_params=_cp("arbitrary"),
        name="moe_experts",
    )(tile_expert, n_valid, xs, w_gate, w_up, w_down)


def _combine_kernel(p_hbm, ys_hbm, x_ref, slab_ref, ga_ref, g_ref, o_ref, idx, buf, sem_i, sem_x, *, final):
    i = pl.program_id(0)
    n = pl.num_programs(0)
    rows = idx.shape[0]
    tm = rows // TOP_K

    def fetch(tile, s):
        _stage_indices(p_hbm.at[tile], idx, sem_i)

        def issue(r, c):
            pltpu.make_async_copy(ys_hbm.at[pl.ds(idx[r], 1), :], buf.at[s, pl.ds(r, 1), :], sem_x.at[s]).start()
            return c

        lax.fori_loop(0, rows, issue, 0, unroll=ROW_DMA_UNROLL)

    @pl.when(i == 0)
    def _():
        fetch(0, 0)

    @pl.when(i + 1 < n)
    def _():
        fetch(i + 1, (i + 1) % 2)

    s = i % 2
    pltpu.make_async_copy(ys_hbm.at[pl.ds(0, rows), :], buf.at[s], sem_x.at[s]).wait()
    slab = slab_ref[...]
    y = slab[:, TOP_K:TOP_K + 1] * buf[s, 0:tm, :]
    for k in range(1, TOP_K):
        y = y + slab[:, TOP_K + k:TOP_K + k + 1] * buf[s, k * tm:(k + 1) * tm, :]
    x = x_ref[...] + ga_ref[...] * y
    if final:
        x = x * lax.rsqrt(jnp.mean(x * x, axis=-1, keepdims=True) + NORM_EPS) * g_ref[...]
    o_ref[...] = x


def _combine(ys, slot, slab, x_all, mod3, g_final, ntiles, nlt, tpb, nb, final):
    d = x_all.shape[1]
    tm = TM_COMB
    f = TM // tm
    n = ntiles * f
    p = jnp.transpose(slot.reshape(n, tm, TOP_K), (0, 2, 1)).reshape(n, TOP_K * tm)
    return pl.pallas_call(
        functools.partial(_combine_kernel, final=final),
        grid=(n,),
        in_specs=[pl.BlockSpec(memory_space=pl.ANY), pl.BlockSpec(memory_space=pl.ANY),
                  pl.BlockSpec((tm, d), lambda i: (i, 0)),
                  pl.BlockSpec((tm, 128), lambda i: (i, 0)),
                  pl.BlockSpec((None, 1, d), lambda i: (jnp.where(i < nlt * f, i // (tpb * f), nb), 0, 5)),
                  pl.BlockSpec((1, d), lambda i: (0, 0))],
        out_specs=pl.BlockSpec((tm, d), lambda i: (i, 0)),
        out_shape=jax.ShapeDtypeStruct((n * tm, d), F32),
        scratch_shapes=[pltpu.SMEM((TOP_K * tm,), jnp.int32),
                        pltpu.VMEM((2, TOP_K * tm, d), F32),
                        pltpu.SemaphoreType.DMA(()), pltpu.SemaphoreType.DMA((2,))],
        compiler_params=_cp("arbitrary"),
        name="combine",
    )(p, ys, x_all, slab, mod3, g_final.reshape(1, d))


def kernel(x, c, ctx, c_ctx, w_mod, b_mod, g_norm1, g_norm2, g_final, w_in, w_merge, w_branch, w_out, w_hconv,
           hy_bias, hf_w1, hf_b1, hf_w2, hf_b2, hf_w3, hf_freq, w_sconv, w_lconv, lru_wa, lru_ba, lru_wx, lru_bx,
           lru_lambda, w_rg, b_rg, w_re, b_re, w_e_gate, w_e_up, w_e_down):
    nb, seq, d = x.shape
    cl = ctx.shape[1]
    depth = w_in.shape[0]
    db = w_branch.shape[2]
    gdim = db // FNET_GROUPS
    ngroups = w_rg.shape[2]
    epg = w_re.shape[3]
    n_experts = ngroups * epg
    t_lat, t_ctx = nb * seq, nb * cl
    t_all = t_lat + t_ctx
    nlt, tpb = t_lat // TM, seq // TM
    nat = t_all // TM
    tl_lat, tl_ctx = min(TL, seq), min(TL, cl)

    x_all = jnp.concatenate([x.reshape(t_lat, d), ctx.reshape(t_ctx, d)], axis=0)
    nrows = -(-(nb + 1) // 8) * 8
    cvec = jnp.zeros((nrows, d), F32).at[:nb].set(c).at[nb].set(c_ctx)
    mods = _adaln(cvec, w_mod, b_mod)

    use_fft = nb % 2 == 0 and seq % (8 * FFT_N2) == 0
    if use_fft:
        hyfft = _fft_consts(seq)
    else:
        c_lat, sf_lat, si_lat = _dft_mats(seq)
        fwd_lat, inv_lat = (c_lat, sf_lat), (c_lat, si_lat)
    c_ctx, sf_ctx, si_ctx = _dft_mats(cl)
    fwd_ctx, inv_ctx = (c_ctx, sf_ctx), (c_ctx, si_ctx)
    fc_lat, fs_lat, w_pq = _fnet_mats(seq, gdim, FNET_GROUPS)
    fc_ctx, fs_ctx, _ = _fnet_mats(cl, gdim, FNET_GROUPS)

    out = None
    for l in range(depth):
        last = l == depth - 1
        mod3 = mods[l].reshape(nrows, 1, 6 * d)
        u_all, h_bf = _inproj(x_all, mod3, g_norm1[l], w_in[l].astype(BF16), nlt, tpb, nb)

        wa = [_block_diag(lru_wa[l, dd]) for dd in range(2)]
        wx = [_block_diag(lru_wx[l, dd]) for dd in range(2)]
        ba = [lru_ba[l, dd].reshape(1, db) for dd in range(2)]
        bx = [lru_bx[l, dd].reshape(1, db) for dd in range(2)]
        lam = [lru_lambda[l, dd].reshape(1, db) for dd in range(2)]
        zeros_st = jnp.zeros((nb, 1, db), F32)
        rows_out = t_lat if last else t_all

        def lru(row0, length, tl, rw, h0f, h0b, combine, out_init):
            hf, stf = _lru_pass(u_all, w_lconv[l], wa[0], wx[0], ba[0], bx[0], lam[0], h0f,
                                row0, nb, length, tl, rw, db, False)
            if combine:
                y, stb = _lru_pass(u_all, w_lconv[l], wa[1], wx[1], ba[1], bx[1], lam[1], h0b,
                                   row0, nb, length, tl, rw, db, True, hf=hf, out_init=out_init, t_all=rows_out)
            else:
                y, stb = _lru_pass(u_all, w_lconv[l], wa[1], wx[1], ba[1], bx[1], lam[1], h0b,
                                   row0, nb, length, tl, rw, db, True)
            return y, stf, stb

        bias0, bias1 = hy_bias[l, 0].reshape(1, db), hy_bias[l, 1].reshape(1, db)

        def hyena_dense(row0, length, fwd, inv, v, x1, x2, v_bf, out_init):
            filt = _hyena_filters(length, hf_w1[l], hf_b1[l], hf_w2[l], hf_b2[l], hf_w3[l], hf_freq[l], db)
            pqr = _filter_spectra(fwd, filt, length, db)
            yhat = _hy_fwd(fwd, v_bf, nb, length, db, pqr[0])
            z2 = _hy_inv(inv, yhat, v, 0, x1, 0, bias0, nb, length, db, 0, nb * length, None)
            yhat = _hy_fwd(fwd, z2, nb, length, db, pqr[1])
            return _hy_inv(inv, yhat, z2, 0, x2, 0, bias1, nb, length, db, row0, rows_out, out_init)

        def hyena_fft(length, v, x1, x2, v_bf, out_init):
            filt = _hyena_filters(length, hf_w1[l], hf_b1[l], hf_w2[l], hf_b2[l], hf_w3[l], hf_freq[l], db,
                                  reverse_bwd=True)
            kspec = _fft_filter_spectra(hyfft, filt, length, db)
            z2 = _fft_conv(hyfft, v_bf, kspec[0], v, x1, bias0, nb, db, None, nb * length)
            return _fft_conv(hyfft, z2, kspec[1], z2, x2, bias1, nb, db, out_init, rows_out)

        zbuf = lambda: None if last else jnp.zeros((rows_out, db), BF16)
        if last:
            _, st_f, st_b = lru(t_lat, cl, tl_ctx, cl, zeros_st, zeros_st, False, None)
        else:
            y_lru_c, st_f, st_b = lru(t_lat, cl, tl_ctx, cl, zeros_st, zeros_st, True, zbuf())
        y_lru, _, _ = lru(0, seq, tl_lat, GRID_W, st_f, st_b, True, None if last else y_lru_c)

        pq = _fnet_pq(u_all, w_pq, db)
        v, x1, x2, v_bf, y_sc = _conv_stage(u_all, w_hconv[l], w_sconv[l], 0, t_lat, tl_lat, GRID_W, db,
                                            jnp.zeros((rows_out, db), BF16))
        y_fn = _fnet_dft(fc_lat, fs_lat, pq, 0, nb, seq, db, gdim, zbuf(), rows_out)
        if not last:
            vc, x1c, x2c, vc_bf, y_sc = _conv_stage(u_all, w_hconv[l], w_sconv[l], t_lat, t_ctx, tl_ctx, cl, db,
                                                    y_sc)
            y_fn = _fnet_dft(fc_ctx, fs_ctx, pq, t_lat, nb, cl, db, gdim, y_fn, rows_out)
            y_hy = hyena_dense(t_lat, cl, fwd_ctx, inv_ctx, vc, x1c, x2c, vc_bf, zbuf())
        else:
            y_hy = None
        if use_fft:
            y_hy = hyena_fft(seq, v, x1, x2, v_bf, y_hy)
        else:
            y_hy = hyena_dense(0, seq, fwd_lat, inv_lat, v, x1, x2, v_bf, y_hy)

        ntiles = nlt if last else nat
        x_mid = _merge(h_bf, (y_fn, y_hy, y_sc, y_lru), w_merge[l].astype(BF16), w_branch[l].astype(BF16),
                       w_out[l].astype(BF16), x_all, mod3, ntiles, nlt, tpb, nb)

        w_router = jnp.zeros((d, 128), F32).at[:, :ngroups].set(w_rg[l])
        w_router = w_router.at[:, ngroups:ngroups + n_experts].set(
            jnp.transpose(w_re[l], (1, 0, 2)).reshape(d, n_experts))
        b_router = jnp.zeros((1, 128), F32).at[0, :ngroups].set(b_rg[l])
        b_router = b_router.at[0, ngroups:ngroups + n_experts].set(b_re[l].reshape(-1))
        h2, slab, tile_counts = _router(x_mid, mod3, g_norm2[l], w_router, b_router, ntiles, nlt, tpb, nb,
                                        ngroups, epg)
        slot, tile_expert, n_valid, n_slot_tiles = _route_tables(slab, tile_counts, ngroups, n_experts, TM_MOE)
        xs = _dispatch(h2, slot, n_slot_tiles * TM_MOE)
        ys = _moe_experts(xs, tile_expert, n_valid, _layer_bf16(w_e_gate, l), _layer_bf16(w_e_up, l),
                          _layer_bf16(w_e_down, l), TM_MOE)
        x_all = _combine(ys, slot, slab, x_mid, mod3, g_final, ntiles, nlt, tpb, nb, last)
        if last:
            out = x_all.reshape(nb, seq, d)
    return out
```

```python
import functools
import math

import jax
import jax.numpy as jnp
from jax import lax
from jax.experimental import pallas as pl
from jax.experimental.pallas import tpu as pltpu

F32 = jnp.float32
BF16 = jnp.bfloat16
HIGHEST = lax.Precision.HIGHEST

NORM_EPS = 1e-6
GRID_W = 64
FNET_GROUPS = 4
LRU_C = 8.0
HY_DECAY_TARGET = 1e-2
HY_FAST_PCT = 0.3
HY_SLOW_PCT = 1.5
TOP_K = 2

TM = 512
TL = 512
TR = 512
TM_MOE = 512
TM_COMB = 512
TN_IN = 1536
TN_MERGE = 512
TF_MOE = 512
VMEM_LIMIT = 56 * 1024 * 1024


def _cp(*sem):
    return pltpu.CompilerParams(dimension_semantics=sem, vmem_limit_bytes=VMEM_LIMIT)


def _sigmoid(x):
    return 0.5 * jnp.tanh(0.5 * x) + 0.5


def _adaln_kernel(c_ref, w_ref, b_ref, o_ref):
    c = c_ref[...]
    s = c * _sigmoid(c)
    o_ref[...] = jnp.dot(s, w_ref[...], preferred_element_type=F32, precision=HIGHEST) + b_ref[...]


def _adaln(cvec, w_mod, b_mod):
    nl, d, n6 = w_mod.shape
    r = cvec.shape[0]
    tn = min(1024, n6)
    return pl.pallas_call(
        _adaln_kernel,
        grid=(nl, n6 // tn),
        in_specs=[pl.BlockSpec((r, d), lambda l, j: (0, 0)),
                  pl.BlockSpec((None, d, tn), lambda l, j: (l, 0, j)),
                  pl.BlockSpec((None, 1, tn), lambda l, j: (l, 0, j))],
        out_specs=pl.BlockSpec((None, r, tn), lambda l, j: (l, 0, j)),
        out_shape=jax.ShapeDtypeStruct((nl, r, n6), F32),
        compiler_params=_cp("parallel", "parallel"),
        name="adaln",
    )(cvec, w_mod, b_mod.reshape(nl, 1, n6))


def _norm_mod(x, g, shift, scale):
    y = x * lax.rsqrt(jnp.mean(x * x, axis=-1, keepdims=True) + NORM_EPS) * g
    return y * (1.0 + scale) + shift


def _norm_mod_store(dst, x_ref, g_ref, sh_ref, sc_ref, rows=128):
    rows = min(rows, dst.shape[0])

    def body(c, carry):
        sl = pl.ds(pl.multiple_of(c * rows, rows), rows)
        dst[sl, :] = _norm_mod(x_ref[sl, :], g_ref[...], sh_ref[...], sc_ref[...]).astype(dst.dtype)
        return carry

    lax.fori_loop(0, dst.shape[0] // rows, body, 0)


U32 = jnp.uint32


def _pack_rows(v):
    half = v.shape[1] // 2
    lo = pltpu.bitcast(v[:, :half].astype(BF16).astype(F32), U32) >> 16
    hi = pltpu.bitcast(v[:, half:].astype(BF16).astype(F32), U32) & jnp.uint32(0xFFFF0000)
    return hi | lo


def _unpack_rows(w):
    lo = pltpu.bitcast(w << 16, F32)
    hi = pltpu.bitcast(w & jnp.uint32(0xFFFF0000), F32)
    return lo, hi


def _mod_spec(d, chunk, nlt, tpb, nb):
    return pl.BlockSpec((None, 1, d), lambda i, *_: (jnp.where(i < nlt, i // tpb, nb), 0, chunk))


def _inproj_kernel(x_ref, sh_ref, sc_ref, g_ref, w_ref, u_ref, h_ref, *, tn):
    j = pl.program_id(1)

    @pl.when(j == 0)
    def _():
        _norm_mod_store(h_ref, x_ref, g_ref, sh_ref, sc_ref)

    for jc in range(w_ref.shape[1] // tn):
        @pl.when(j == jc)
        def _(jc=jc):
            u_ref[...] = jnp.dot(h_ref[...], w_ref[:, jc * tn:(jc + 1) * tn], preferred_element_type=F32)


def _inproj(x_all, mod3, g, w_bf, nlt, tpb, nb):
    t, d = x_all.shape
    n = w_bf.shape[1]
    tn = TN_IN if n % TN_IN == 0 else n
    return pl.pallas_call(
        functools.partial(_inproj_kernel, tn=tn),
        grid=(t // TM, n // tn),
        in_specs=[pl.BlockSpec((TM, d), lambda i, j: (i, 0)),
                  _mod_spec(d, 0, nlt, tpb, nb), _mod_spec(d, 1, nlt, tpb, nb),
                  pl.BlockSpec((1, d), lambda i, j: (0, 0)),
                  pl.BlockSpec((d, n), lambda i, j: (0, 0))],
        out_specs=[pl.BlockSpec((TM, tn), lambda i, j: (i, j)), pl.BlockSpec((TM, d), lambda i, j: (i, 0))],
        out_shape=[jax.ShapeDtypeStruct((t, n), F32), jax.ShapeDtypeStruct((t, d), BF16)],
        compiler_params=_cp("parallel", "arbitrary"),
        name="inproj",
    )(x_all, mod3, mod3, g.reshape(1, d), w_bf)


def _dwconv(x, w_ref, left, rw):
    tl = x.shape[0]
    pos = lax.broadcasted_iota(jnp.int32, x.shape, 0) % rw
    acc = None
    for k in range(w_ref.shape[0]):
        off = k - left
        if off == 0:
            term = x
        else:
            shifted = pltpu.roll(x, (-off) % tl, axis=0)
            ok = (pos + off >= 0) & (pos + off < rw)
            term = jnp.where(ok, shifted, 0.0)
        term = term * w_ref[k:k + 1, :]
        acc = term if acc is None else acc + term
    return acc


def _conv_kernel(hv_ref, h1_ref, h2_ref, sb_ref, sc_ref, sh_ref, wh_ref, ws_ref, ys_in,
                 v_ref, x1_ref, x2_ref, vbf_ref, ys_ref, *, rw, db):
    del ys_in
    for k, (ref, dst) in enumerate(((hv_ref, v_ref), (h1_ref, x1_ref), (h2_ref, x2_ref))):
        y = _dwconv(ref[...], wh_ref.at[:, k * db:(k + 1) * db], 1, rw)
        dst[...] = y
        if k == 0:
            vbf_ref[...] = y.astype(BF16)
    ys_ref[...] = (sb_ref[...] * _dwconv(sc_ref[...] * sh_ref[...], ws_ref, 1, rw)).astype(BF16)


def _conv_stage(u_all, w_hconv, w_sconv, row0, nrows, tl, rw, db, ys_init):
    ob = row0 // tl
    col = lambda c: pl.BlockSpec((tl, db), lambda i: (ob + i, c))
    own = pl.BlockSpec((tl, db), lambda i: (i, 0))
    return pl.pallas_call(
        functools.partial(_conv_kernel, rw=rw, db=db),
        grid=(nrows // tl,),
        in_specs=[col(1), col(2), col(3), col(4), col(5), col(6),
                  pl.BlockSpec(w_hconv.shape, lambda i: (0, 0)),
                  pl.BlockSpec(w_sconv.shape, lambda i: (0, 0)),
                  pl.BlockSpec(memory_space=pl.ANY)],
        out_specs=[own, own, own, own, pl.BlockSpec((tl, db), lambda i: (ob + i, 0))],
        out_shape=[jax.ShapeDtypeStruct((nrows, db), F32)] * 3
                  + [jax.ShapeDtypeStruct((nrows, db), BF16), jax.ShapeDtypeStruct(ys_init.shape, BF16)],
        input_output_aliases={8: 4},
        compiler_params=_cp("parallel"),
        name="conv",
    )(*([u_all] * 6), w_hconv, w_sconv, ys_init)


def _trig_outer(l, period):
    q = 1 << ((l.bit_length() - 1) // 2)
    n = lax.broadcasted_iota(jnp.int32, (1, l), 1)
    scale = 2.0 * math.pi / period

    def table(rows, step):
        r = lax.broadcasted_iota(jnp.int32, (rows, 1), 0) * step
        ang = ((r * n) % period).astype(F32) * scale
        return jnp.cos(ang), jnp.sin(ang)

    ac, as_ = table(l // q, q)
    bc, bs = table(q, 1)
    c = ac[:, None, :] * bc[None] - as_[:, None, :] * bs[None]
    s = as_[:, None, :] * bc[None] + ac[:, None, :] * bs[None]
    return c.reshape(l, l), s.reshape(l, l)


def _dft_mats(l):
    c, s = _trig_outer(l, 2 * l)
    k = lax.broadcasted_iota(jnp.int32, (l, l), 0)
    n = lax.broadcasted_iota(jnp.int32, (l, l), 1)
    alt_n = jnp.where(n % 2 == 0, 1.0, -1.0).astype(F32)
    alt_k = jnp.where(k % 2 == 0, 1.0, -1.0).astype(F32)
    return c.astype(BF16), jnp.where(k == 0, alt_n, -s).astype(BF16), jnp.where(n == 0, alt_k, -s).astype(BF16)


def _fnet_mats(l, gdim, groups):
    c, s = _trig_outer(l, l)
    cg, sg = _trig_outer(gdim, gdim)
    eye = jnp.eye(groups, dtype=F32)
    w_pq = jnp.concatenate([jnp.kron(eye, cg), jnp.kron(eye, sg)], axis=1).astype(BF16)
    return c.astype(BF16), (-s).astype(BF16), w_pq


def _fnet_pq_kernel(u_ref, w_ref, o_ref):
    o_ref[...] = jnp.dot(u_ref[...].astype(BF16), w_ref[...], preferred_element_type=F32).astype(BF16)


def _fnet_pq(u_all, w_pq, db):
    t = u_all.shape[0]
    return pl.pallas_call(
        _fnet_pq_kernel,
        grid=(t // TM,),
        in_specs=[pl.BlockSpec((TM, db), lambda i: (i, 0)),
                  pl.BlockSpec(w_pq.shape, lambda i: (0, 0))],
        out_specs=pl.BlockSpec((TM, 2 * db), lambda i: (i, 0)),
        out_shape=jax.ShapeDtypeStruct((t, 2 * db), BF16),
        compiler_params=_cp("parallel"),
        name="fnet_pq",
    )(u_all, w_pq)


def _fnet_dft_kernel(d0_ref, d1_ref, x_ref, *rest, db, scale):
    o_ref = rest[-1]
    acc = jnp.dot(d0_ref[...], x_ref[:, :db], preferred_element_type=F32)
    acc = acc + jnp.dot(d1_ref[...], x_ref[:, db:], preferred_element_type=F32)
    o_ref[...] = (acc * scale).astype(BF16)


def _mat_spec(tr, l):
    return pl.BlockSpec((tr, l), lambda i, b: (i, 0))


def _fnet_dft(d0, d1, pq, row0, nb, l, db, gdim, out_init, t_all):
    tr = min(TR, l)
    nrt = l // tr
    args = [d0, d1, pq]
    in_specs = [_mat_spec(tr, l), _mat_spec(tr, l),
                pl.BlockSpec((l, 2 * db), lambda i, b: (row0 // l + b, 0))]
    aliases = {}
    if out_init is not None:
        args.append(out_init)
        in_specs.append(pl.BlockSpec(memory_space=pl.ANY))
        aliases = {3: 0}
    return pl.pallas_call(
        functools.partial(_fnet_dft_kernel, db=db, scale=1.0 / math.sqrt(l * gdim)),
        grid=(nrt, nb),
        in_specs=in_specs,
        out_specs=pl.BlockSpec((tr, db), lambda i, b: (row0 // tr + b * nrt + i, 0)),
        out_shape=jax.ShapeDtypeStruct((t_all, db), BF16),
        input_output_aliases=aliases,
        compiler_params=_cp("parallel", "parallel"),
        name="fnet_dft",
    )(*args)


def _hy_fwd_kernel(f0_ref, f1_ref, x_ref, *rest, mult):
    o_ref = rest[-1]
    zre = jnp.dot(f0_ref[...], x_ref[...], preferred_element_type=F32)
    zim = jnp.dot(f1_ref[...], x_ref[...], preferred_element_type=F32)
    if mult:
        k_ref = rest[0]
        p, q, r = k_ref[0], k_ref[1], k_ref[2]
        o_ref[0] = (zre * p - zim * q).astype(o_ref.dtype)
        o_ref[1] = (zre * q + zim * r).astype(o_ref.dtype)
    else:
        o_ref[0] = zre
        o_ref[1] = zim


def _hy_fwd(fwd, x, nb, l, db, pqr):
    tr = min(TR, l)
    args = [fwd[0], fwd[1], x]
    in_specs = [_mat_spec(tr, l), _mat_spec(tr, l),
                pl.BlockSpec((l, db), lambda i, b: (b, 0))]
    if pqr is not None:
        args.append(pqr)
        in_specs.append(pl.BlockSpec((3, tr, db), lambda i, b: (0, i, 0)))
    return pl.pallas_call(
        functools.partial(_hy_fwd_kernel, mult=pqr is not None),
        grid=(l // tr, nb),
        in_specs=in_specs,
        out_specs=pl.BlockSpec((None, 2, tr, db), lambda i, b: (b, 0, i, 0)),
        out_shape=jax.ShapeDtypeStruct((nb, 2, l, db), F32 if pqr is None else BF16),
        compiler_params=_cp("parallel", "parallel"),
        name="hy_fwd",
    )(*args)


def _hy_inv_kernel(g0_ref, g1_ref, y_ref, zp_ref, gate_ref, bias_ref, *rest):
    o_ref = rest[-1]
    y = jnp.dot(g0_ref[...], y_ref[0], preferred_element_type=F32)
    y = y + jnp.dot(g1_ref[...], y_ref[1], preferred_element_type=F32)
    o_ref[...] = (gate_ref[...] * (y + bias_ref[...] * zp_ref[...].astype(F32))).astype(BF16)


def _hy_inv(inv, yhat, zprev, zcol, gate, gcol, bias, nb, l, db, out_row0, out_rows, out_init):
    tr = min(TR, l)
    nrt = l // tr
    args = [inv[0], inv[1], yhat, zprev, gate, bias]
    in_specs = [_mat_spec(tr, l), _mat_spec(tr, l),
                pl.BlockSpec((None, 2, l, db), lambda i, b: (b, 0, 0, 0)),
                pl.BlockSpec((tr, db), lambda i, b: (b * nrt + i, zcol)),
                pl.BlockSpec((tr, db), lambda i, b: (b * nrt + i, gcol)),
                pl.BlockSpec((1, db), lambda i, b: (0, 0))]
    aliases = {}
    if out_init is not None:
        args.append(out_init)
        in_specs.append(pl.BlockSpec(memory_space=pl.ANY))
        aliases = {6: 0}
    return pl.pallas_call(
        _hy_inv_kernel,
        grid=(nrt, nb),
        in_specs=in_specs,
        out_specs=pl.BlockSpec((tr, db), lambda i, b: (out_row0 // tr + b * nrt + i, 0)),
        out_shape=jax.ShapeDtypeStruct((out_rows, db), BF16),
        input_output_aliases=aliases,
        compiler_params=_cp("parallel", "parallel"),
        name="hy_inv",
    )(*args)


def _hyena_filters(length, hf_w1, hf_b1, hf_w2, hf_b2, hf_w3, hf_freq, db, reverse_bwd=False):
    emb = hf_w1.shape[0]
    nbands = (emb - 1) // 2
    bands = jnp.linspace(1e-4, nbands - 1, nbands, dtype=F32)[None, :]
    min_decay = math.log(HY_DECAY_TARGET) / HY_SLOW_PCT
    max_decay = math.log(HY_DECAY_TARGET) / HY_FAST_PCT
    deltas = jnp.abs(jnp.linspace(min_decay, max_decay, db, dtype=F32))
    w3 = hf_w3.reshape(hf_w3.shape[0], -1, 2, db)

    def direction(pos, d):
        t = (pos * (1.0 / (length - 1)))[:, None]
        w = (2.0 * math.pi / length) * pos[:, None]
        feats = jnp.concatenate([t, jnp.cos(bands * w), -jnp.sin(bands * w)], axis=-1)
        z = jnp.sin(hf_freq[0] * (jnp.dot(feats, hf_w1, precision=HIGHEST) + hf_b1))
        z = jnp.sin(hf_freq[1] * (jnp.dot(z, hf_w2, precision=HIGHEST) + hf_b2))
        f = jnp.einsum("lf,fod->lod", z, w3[:, :, d, :], precision=HIGHEST)
        return f * jnp.exp(-t * deltas)[:, None, :]

    pos = jnp.arange(length, dtype=F32)
    filt = jnp.stack([direction(pos, 0), direction(length - 1 - pos if reverse_bwd else pos, 1)], axis=2)
    return filt * lax.rsqrt(jnp.sum(filt * filt, axis=(0, 2), keepdims=True) + NORM_EPS)


def _filter_spectra(fwd, filt, l, db):
    n_ord = filt.shape[1]
    cols = jnp.transpose(filt, (1, 2, 0, 3))
    cols = cols.at[:, 1, 0, :].set(0.0)
    x = cols.reshape(n_ord * 2 * l, db).astype(BF16)
    spec = _hy_fwd(fwd, x, n_ord * 2, l, db, None).reshape(n_ord, 2, 2, l, db)
    hf, hb = spec[:, 0], spec[:, 1]
    k_re = hf[:, 0] + hb[:, 0]
    k_im = hf[:, 1] - hb[:, 1]
    k_nyq = hf[:, 1, 0] + hb[:, 1, 0]
    first = (jnp.arange(l) == 0)[None, :, None]
    scale = jnp.where(first, 1.0 / (2 * l), 2.0 / (2 * l)).astype(F32)
    p = k_re * scale
    q = jnp.where(first, 0.0, k_im) * scale
    r = jnp.where(first, k_nyq[:, None, :], k_re) * scale
    return jnp.stack([p, q, r], axis=1)


FFT_N2 = 128
FFT_NT = 16
FFT_KC = 8


def _fft_consts(l):
    n, n2 = 2 * l, FFT_N2
    n1 = n // n2
    h = n1 // 2
    cat = jnp.concatenate
    ia = jnp.arange(n1, dtype=jnp.int32)
    ang = ((ia[:, None] * ia[None, :]) % n1).astype(F32) * (2.0 * math.pi / n1)
    fr, fi = jnp.cos(ang), -jnp.sin(ang)
    mat_data = cat([cat([fr[:, :h], -fi[:, :h]], 1), cat([fi[:, :h], fr[:, :h]], 1)], 0)
    mat_filt = cat([fr, fi], 0)
    gr, gi = fr[:, :h].T, -fi[:, :h].T
    mat_inv = jnp.stack([cat([gr, -gi], 1), cat([gi, gr], 1)])
    k = ia[:, None, None] + n1 * jnp.arange(n2, dtype=jnp.int32)[None, :, None]
    nn = jnp.arange(n2, dtype=jnp.int32)[None, None, :]
    angb = ((k * nn) % n).astype(F32) * (2.0 * math.pi / n)
    er, ei = jnp.cos(angb), -jnp.sin(angb)
    mb = cat([cat([er, -ei], 2), cat([ei, er], 2)], 1)
    eye = jnp.eye(FFT_NT, dtype=F32)
    kron = lambda a: jnp.kron(a, eye).astype(BF16)
    return dict(n1=n1, h=h, mat_data=kron(mat_data), mat_filt=kron(mat_filt * (1.0 / n)),
                mat_inv=jnp.stack([kron(mat_inv[0]), kron(mat_inv[1])]),
                mb=mb.astype(BF16), mib=jnp.transpose(mb, (0, 2, 1)).astype(BF16))


def _fft_a_kernel(m_ref, a_ref, b_ref, o_ref):
    h, nt, db = a_ref.shape
    x = jnp.concatenate([a_ref[...].reshape(h * nt, db), b_ref[...].reshape(h * nt, db)], axis=0)
    res = jnp.dot(m_ref[...], x, preferred_element_type=F32)
    o_ref[...] = res.astype(o_ref.dtype).reshape(o_ref.shape)


def _fft_stage_a(mat, x4, npairs, stride, offset):
    _, h, n2, db = x4.shape
    nt = FFT_NT
    rows = mat.shape[0] // nt
    return pl.pallas_call(
        _fft_a_kernel,
        grid=(npairs, n2 // nt),
        in_specs=[pl.BlockSpec(mat.shape, lambda p, j: (0, 0)),
                  pl.BlockSpec((None, h, nt, db), lambda p, j: (p * stride, 0, j, 0)),
                  pl.BlockSpec((None, h, nt, db), lambda p, j: (p * stride + offset, 0, j, 0))],
        out_specs=pl.BlockSpec((None, rows, nt, db), lambda p, j: (p, 0, j, 0)),
        out_shape=jax.ShapeDtypeStruct((npairs, rows, n2, db), BF16),
        compiler_params=_cp("parallel", "parallel"),
        name="fft_a",
    )(mat, x4, x4)


def _fft_b_kernel(m_ref, ar_ref, ai_ref, o_ref):
    for kk in range(m_ref.shape[0]):
        a = jnp.concatenate([ar_ref[kk], ai_ref[kk]], axis=0)
        o_ref[kk] = jnp.dot(m_ref[kk], a, preferred_element_type=F32)


def _fft_bb_kernel(m_ref, mi_ref, ar_ref, ai_ref, k_ref, o_ref):
    n2 = ar_ref.shape[1]
    for kk in range(m_ref.shape[0]):
        a = jnp.concatenate([ar_ref[kk], ai_ref[kk]], axis=0)
        x = jnp.dot(m_ref[kk], a, preferred_element_type=F32)
        xr, xi = x[:n2], x[n2:]
        kr, ki = k_ref[kk, :n2], k_ref[kk, n2:]
        y = jnp.concatenate([(xr * kr - xi * ki).astype(BF16), (xr * ki + xi * kr).astype(BF16)], axis=0)
        b = jnp.dot(mi_ref[kk], y, preferred_element_type=F32)
        o_ref[0, kk] = b[:n2].astype(o_ref.dtype)
        o_ref[1, kk] = b[n2:].astype(o_ref.dtype)


def _fft_stage_b(fc, a, kspec, db):
    mb, mib = fc["mb"], fc["mib"]
    npairs = a.shape[0]
    n1, r2, _ = mb.shape
    n2 = r2 // 2
    kc = min(FFT_KC, n1)
    a5 = a.reshape(npairs, 2, n1, n2, db)
    mspec = pl.BlockSpec((kc, r2, r2), lambda c, p: (c, 0, 0))
    aspec = lambda part: pl.BlockSpec((None, None, kc, n2, db), lambda c, p: (p, part, c, 0, 0))
    if kspec is None:
        kern, args, in_specs = _fft_b_kernel, [mb, a5, a5], [mspec, aspec(0), aspec(1)]
        out_spec = pl.BlockSpec((None, kc, r2, db), lambda c, p: (p, c, 0, 0))
        out_shape = jax.ShapeDtypeStruct((npairs, n1, r2, db), F32)
    else:
        kern, args = _fft_bb_kernel, [mb, mib, a5, a5, kspec]
        in_specs = [mspec, mspec, aspec(0), aspec(1), pl.BlockSpec((kc, r2, db), lambda c, p: (c, 0, 0))]
        out_spec = pl.BlockSpec((None, 2, kc, n2, db), lambda c, p: (p, 0, c, 0, 0))
        out_shape = jax.ShapeDtypeStruct((npairs, 2, n1, n2, db), BF16)
    return pl.pallas_call(
        kern,
        grid=(n1 // kc, npairs),
        in_specs=in_specs,
        out_specs=out_spec,
        out_shape=out_shape,
        compiler_params=_cp("parallel", "parallel"),
        name="fft_b",
    )(*args)


def _ifft_a_kernel(m_ref, b_ref, zp_ref, gate_ref, bias_ref, *rest):
    o_ref = rest[-1]
    h, nt, db = o_ref.shape
    bm = b_ref[...].reshape(b_ref.shape[0] * nt, db)
    y = jnp.dot(m_ref[...], bm, preferred_element_type=F32)
    zp = zp_ref[...].reshape(h * nt, db).astype(F32)
    out = gate_ref[...].reshape(h * nt, db) * (y + bias_ref[...] * zp)
    o_ref[...] = out.astype(o_ref.dtype).reshape(o_ref.shape)


def _ifft_stage_a(mat_inv, b5, zprev, gate, bias, nb, db, out_init, out_rows):
    npairs, _, n1, n2, _ = b5.shape
    nt = FFT_NT
    h = mat_inv.shape[1] // nt
    b4 = b5.reshape(npairs, 2 * n1, n2, db)
    view = lambda a: a.reshape(a.shape[0] // n2, n2, db)
    blk = pl.BlockSpec((h, nt, db), lambda b, j: (b, j, 0))
    args = [mat_inv, b4, view(zprev), view(gate), bias]
    in_specs = [pl.BlockSpec((None,) + mat_inv.shape[1:], lambda b, j: (b // npairs, 0, 0)),
                pl.BlockSpec((None, 2 * n1, nt, db), lambda b, j: (b % npairs, 0, j, 0)),
                blk, blk, pl.BlockSpec((1, db), lambda b, j: (0, 0))]
    aliases = {}
    if out_init is not None:
        args.append(view(out_init))
        in_specs.append(pl.BlockSpec(memory_space=pl.ANY))
        aliases = {5: 0}
    out = pl.pallas_call(
        _ifft_a_kernel,
        grid=(nb, n2 // nt),
        in_specs=in_specs,
        out_specs=blk,
        out_shape=jax.ShapeDtypeStruct((out_rows // n2, n2, db), BF16),
        input_output_aliases=aliases,
        compiler_params=_cp("parallel", "parallel"),
        name="ifft_a",
    )(*args)
    return out.reshape(out_rows, db)


def _fft_filter_spectra(fc, filt, l, db):
    n_ord = filt.shape[1]
    h_fwd = jnp.transpose(filt[:, :, 0, :], (1, 0, 2))
    h_bwd = jnp.transpose(filt[:, :, 1, :], (1, 0, 2))
    ker = jnp.concatenate([h_fwd, jnp.zeros((n_ord, 1, db), F32), h_bwd[:, :l - 1, :]], axis=1)
    x4 = ker.astype(BF16).reshape(n_ord * 2, fc["h"], FFT_N2, db)
    a = _fft_stage_a(fc["mat_filt"], x4, n_ord, 2, 1)
    return _fft_stage_b(fc, a, None, db)


def _fft_conv(fc, z_bf, kspec, zprev, gate, bias, nb, db, out_init, out_rows):
    x4 = z_bf.reshape(nb, fc["h"], FFT_N2, db)
    a = _fft_stage_a(fc["mat_data"], x4, nb // 2, 1, nb // 2)
    b5 = _fft_stage_b(fc, a, kspec, db)
    return _ifft_stage_a(fc["mat_inv"], b5, zprev, gate, bias, nb, db, out_init, out_rows)


def _lru_kernel(*refs, rw, reverse, final):
    if final:
        (x_ref, w_ref, wa_ref, wx_ref, ba_ref, bx_ref, lam_ref, h0_ref, hf_ref, ug_ref,
         _, o_ref, st_ref, a_s, b_s, carry) = refs
    else:
        (x_ref, w_ref, wa_ref, wx_ref, ba_ref, bx_ref, lam_ref, h0_ref,
         o_ref, st_ref, a_s, b_s, carry) = refs
    tl = x_ref.shape[0]

    @pl.when(pl.program_id(1) == 0)
    def _():
        carry[...] = h0_ref[...]

    xc = _dwconv(x_ref[...], w_ref, w_ref.shape[0] // 2, rw)
    xb = xc.astype(BF16)
    r = _sigmoid(jnp.dot(xb, wa_ref[...], preferred_element_type=F32) + ba_ref[...])
    ig = _sigmoid(jnp.dot(xb, wx_ref[...], preferred_element_type=F32) + bx_ref[...])
    lam = lam_ref[...]
    softplus = jnp.maximum(-lam, 0.0) + jnp.log(1.0 + jnp.exp(-jnp.abs(lam)))
    log_a = -LRU_C * r * softplus
    a = jnp.exp(log_a)
    bv = jnp.sqrt(1.0 - jnp.exp(2.0 * log_a)) * (ig * xc)

    pos8 = lax.broadcasted_iota(jnp.int32, a.shape, 0) % 8
    for s in (1, 2, 4):
        if reverse:
            a_sh, b_sh, ok = pltpu.roll(a, tl - s, axis=0), pltpu.roll(bv, tl - s, axis=0), pos8 < 8 - s
        else:
            a_sh, b_sh, ok = pltpu.roll(a, s, axis=0), pltpu.roll(bv, s, axis=0), pos8 >= s
        bv = jnp.where(ok, a * b_sh + bv, bv)
        a = jnp.where(ok, a * a_sh, a)
    a_s[...] = a
    b_s[...] = bv

    ng = tl // 8

    def body(gi, c):
        g = ng - 1 - gi if reverse else gi
        sl = pl.ds(pl.multiple_of(g * 8, 8), 8)
        h8 = b_s[sl, :] + a_s[sl, :] * c
        b_s[sl, :] = h8
        return h8[0:1, :] if reverse else h8[7:8, :]

    c_out = lax.fori_loop(0, ng, body, carry[...])
    carry[...] = c_out
    st_ref[...] = c_out
    if final:
        ug = ug_ref[...]
        gelu = 0.5 * ug * (1.0 + jnp.tanh(math.sqrt(2.0 / math.pi) * (ug + 0.044715 * ug * ug * ug)))
        o_ref[...] = ((hf_ref[...] + b_s[...]) * gelu).astype(BF16)
    else:
        o_ref[...] = b_s[...]


def _lru_pass(u_all, w_lconv, wa, wx, ba, bx, lam, h0, row0, nb, l, tl, rw, db, reverse, hf=None, out_init=None,
              t_all=None):
    final = hf is not None
    nc = l // tl
    ob = row0 // tl
    cidx = (lambda c: nc - 1 - c) if reverse else (lambda c: c)
    full = lambda a: pl.BlockSpec(a.shape, lambda b, c: (0,) * a.ndim)
    ncolx, ncolg = 7, 8
    args = [u_all, w_lconv, wa, wx, ba, bx, lam, h0]
    in_specs = [pl.BlockSpec((tl, db), lambda b, c: (ob + b * nc + cidx(c), ncolx)),
                full(w_lconv), full(wa), full(wx), full(ba), full(bx), full(lam),
                pl.BlockSpec((None, 1, db), lambda b, c: (b, 0, 0))]
    aliases = {}
    if final:
        args += [hf, u_all]
        in_specs += [pl.BlockSpec((tl, db), lambda b, c: (b * nc + cidx(c), 0)),
                     pl.BlockSpec((tl, db), lambda b, c: (ob + b * nc + cidx(c), ncolg))]
        if out_init is None:
            out_init = jnp.zeros((8, 128), BF16)
        else:
            aliases = {10: 0}
        args.append(out_init)
        in_specs.append(pl.BlockSpec(memory_space=pl.ANY))
        out_spec = pl.BlockSpec((tl, db), lambda b, c: (ob + b * nc + cidx(c), 0))
        out_shape = jax.ShapeDtypeStruct((t_all, db), BF16)
    else:
        out_spec = pl.BlockSpec((tl, db), lambda b, c: (b * nc + cidx(c), 0))
        out_shape = jax.ShapeDtypeStruct((nb * l, db), F32)
    return pl.pallas_call(
        functools.partial(_lru_kernel, rw=rw, reverse=reverse, final=final),
        grid=(nb, nc),
        in_specs=in_specs,
        out_specs=[out_spec, pl.BlockSpec((None, 1, db), lambda b, c: (b, 0, 0))],
        out_shape=[out_shape, jax.ShapeDtypeStruct((nb, 1, db), F32)],
        scratch_shapes=[pltpu.VMEM((tl, db), F32), pltpu.VMEM((tl, db), F32), pltpu.VMEM((1, db), F32)],
        input_output_aliases=aliases,
        compiler_params=_cp("arbitrary", "arbitrary"),
        name="lru_bwd" if reverse else "lru_fwd",
    )(*args)


def _block_diag(w):
    h, hd, _ = w.shape
    eye = jnp.eye(h, dtype=w.dtype)
    return (eye[:, None, :, None] * w[:, :, None, :]).reshape(h * hd, h * hd).astype(BF16)


def _merge_kernel(x_ref, h_ref, b0_ref, b1_ref, b2_ref, b3_ref, wm_ref, wb_ref, wo_ref, ga_ref, o_ref):
    j = pl.program_id(1)
    h = h_ref[...]
    m = None
    for k, br in enumerate((b0_ref, b1_ref, b2_ref, b3_ref)):
        g = jnp.dot(h, wm_ref[k], preferred_element_type=F32)
        p = jnp.dot(br[...], wb_ref[k], preferred_element_type=F32)
        term = _sigmoid(g) * p
        m = term if m is None else m + term
    part = jnp.dot(m.astype(BF16), wo_ref[...], preferred_element_type=F32)

    @pl.when(j == 0)
    def _():
        o_ref[...] = part

    @pl.when(j > 0)
    def _():
        o_ref[...] += part

    @pl.when(j == pl.num_programs(1) - 1)
    def _():
        o_ref[...] = x_ref[...] + ga_ref[...] * o_ref[...]


def _merge(h_bf, branches, wm_bf, wb_bf, wo_bf, x_all, mod3, ntiles, nlt, tpb, nb):
    t, d = x_all.shape
    db = wb_bf.shape[1]
    tn = min(TN_MERGE, d)
    row = lambda w: pl.BlockSpec((TM, w), lambda i, j: (i, 0))
    return pl.pallas_call(
        _merge_kernel,
        grid=(ntiles, d // tn),
        in_specs=[row(d), row(d), row(db), row(db), row(db), row(db),
                  pl.BlockSpec((4, d, tn), lambda i, j: (0, 0, j)),
                  pl.BlockSpec((4, db, tn), lambda i, j: (0, 0, j)),
                  pl.BlockSpec((tn, d), lambda i, j: (j, 0)),
                  _mod_spec(d, 2, nlt, tpb, nb)],
        out_specs=row(d),
        out_shape=jax.ShapeDtypeStruct((ntiles * TM, d), F32),
        compiler_params=_cp("parallel", "arbitrary"),
        name="merge",
    )(x_all, h_bf, *branches, wm_bf, wb_bf, wo_bf, mod3)


def _router_kernel(x_ref, sh_ref, sc_ref, g_ref, whi_ref, wlo_ref, br_ref, tri_ref, h_ref, r_ref, c_ref, *,
                   ngroups, epg):
    h = _norm_mod(x_ref[...], g_ref[...], sh_ref[...], sc_ref[...])
    h_ref[...] = _pack_rows(h)
    h_hi = h.astype(BF16)
    h_lo = (h - h_hi.astype(F32)).astype(BF16)
    logits = jnp.dot(h_hi, wlo_ref[...], preferred_element_type=F32)
    logits = logits + jnp.dot(h_lo, whi_ref[...], preferred_element_type=F32)
    logits = logits + jnp.dot(h_hi, whi_ref[...], preferred_element_type=F32) + br_ref[...]
    lane = lax.broadcasted_iota(jnp.int32, logits.shape, 1)
    lane_f = lane.astype(F32)
    neg = jnp.float32(-1e30)
    big = jnp.float32(1e6)
    is_g = lane < ngroups
    lg = jnp.where(is_g, logits, neg)
    mx = jnp.max(lg, axis=-1, keepdims=True)
    g_top = jnp.min(jnp.where(lg == mx, lane_f, big), axis=-1, keepdims=True)
    den = jnp.sum(jnp.where(is_g, jnp.exp(lg - mx), 0.0), axis=-1, keepdims=True)
    p_top = 1.0 / den
    el = lane_f - ngroups
    in_grp = (el >= g_top * epg) & (el < (g_top + 1.0) * epg)
    le = jnp.where(in_grp, logits, neg)
    v1 = jnp.max(le, axis=-1, keepdims=True)
    e1 = jnp.min(jnp.where(le == v1, el, big), axis=-1, keepdims=True)
    le2 = jnp.where(el == e1, neg, le)
    v2 = jnp.max(le2, axis=-1, keepdims=True)
    e2 = jnp.min(jnp.where(le2 == v2, el, big), axis=-1, keepdims=True)
    dlt = jnp.exp(v2 - v1)
    w1 = p_top / (1.0 + dlt)
    w2 = p_top * dlt / (1.0 + dlt)
    tm = logits.shape[0]
    pick1 = el == e1
    pick2 = el == e2
    onehot = jnp.where(pick1 | pick2, 1.0, 0.0)
    before = jnp.dot(tri_ref[...], onehot.astype(BF16), preferred_element_type=F32)
    rank1 = jnp.sum(jnp.where(pick1, before, 0.0), axis=-1, keepdims=True)
    rank2 = jnp.sum(jnp.where(pick2, before, 0.0), axis=-1, keepdims=True)
    out = jnp.where(lane == 0, e1, 0.0)
    out = jnp.where(lane == 1, e2, out)
    out = jnp.where(lane == 2, w1, out)
    out = jnp.where(lane == 3, w2, out)
    out = jnp.where(lane == 4, rank1, out)
    out = jnp.where(lane == 5, rank2, out)
    r_ref[...] = out
    c_ref[...] = jnp.broadcast_to(jnp.sum(onehot, axis=0, keepdims=True), c_ref.shape)


def _router(x_all, mod3, g, w_router, b_router, ntiles, nlt, tpb, nb, ngroups, epg):
    t, d = x_all.shape
    w_hi = w_router.astype(BF16)
    w_hi_lo = (w_hi, (w_router - w_hi.astype(F32)).astype(BF16))
    earlier = jnp.tril(jnp.ones((TM, TM), F32), -1).astype(BF16)
    return pl.pallas_call(
        functools.partial(_router_kernel, ngroups=ngroups, epg=epg),
        grid=(ntiles,),
        in_specs=[pl.BlockSpec((TM, d), lambda i: (i, 0)),
                  _mod_spec(d, 3, nlt, tpb, nb), _mod_spec(d, 4, nlt, tpb, nb),
                  pl.BlockSpec((1, d), lambda i: (0, 0)),
                  pl.BlockSpec((d, 128), lambda i: (0, 0)),
                  pl.BlockSpec((d, 128), lambda i: (0, 0)),
                  pl.BlockSpec((1, 128), lambda i: (0, 0)),
                  pl.BlockSpec((TM, TM), lambda i: (0, 0))],
        out_specs=[pl.BlockSpec((TM, d // 2), lambda i: (i, 0)), pl.BlockSpec((TM, 128), lambda i: (i, 0)),
                   pl.BlockSpec((None, 8, 128), lambda i: (i, 0, 0))],
        out_shape=[jax.ShapeDtypeStruct((ntiles * TM, d // 2), U32),
                   jax.ShapeDtypeStruct((ntiles * TM, 128), F32),
                   jax.ShapeDtypeStruct((ntiles, 8, 128), F32)],
        compiler_params=_cp("parallel"),
        name="router",
    )(x_all, mod3, mod3, g.reshape(1, d), w_hi, w_hi_lo[1], b_router, earlier)


def _route_tables(slab, tile_counts, ngroups, n_experts, tm):
    ntiles = tile_counts.shape[0]
    e = slab[:, 0:TOP_K].astype(jnp.int32).reshape(ntiles, TM, TOP_K)
    rank = slab[:, 4:4 + TOP_K].astype(jnp.int32).reshape(ntiles, TM, TOP_K)
    counts_t = tile_counts[:, 0, ngroups:ngroups + n_experts].astype(jnp.int32)
    base = jnp.cumsum(counts_t, axis=0) - counts_t
    counts = jnp.sum(counts_t, axis=0)
    tiles_per_e = (counts + tm - 1) // tm
    tile_end = jnp.cumsum(tiles_per_e)
    starts = (tile_end - tiles_per_e) * tm
    off = starts[None, :] + base
    sel = e[..., None] == jnp.arange(n_experts, dtype=jnp.int32)
    slot = jnp.sum(jnp.where(sel, off[:, None, None, :], 0), axis=-1) + rank
    n_slot_tiles = (ntiles * TM * TOP_K) // tm + n_experts
    n_valid = tile_end[-1]
    tile_ids = jnp.minimum(jnp.arange(n_slot_tiles, dtype=jnp.int32), n_valid - 1)
    tile_expert = jnp.sum((tile_end[None, :] <= tile_ids[:, None]).astype(jnp.int32), axis=1)
    return slot, tile_expert, n_valid.reshape(1).astype(jnp.int32), n_slot_tiles


ROW_DMA_UNROLL = 8


def _stage_indices(idx_hbm_row, idx_smem, sem_i):
    cp = pltpu.make_async_copy(idx_hbm_row, idx_smem, sem_i)
    cp.start()
    cp.wait()


def _dispatch_kernel(p_hbm, h_ref, xs_in, xs_hbm, idx, sem_i, sem_x):
    del xs_in
    i = pl.program_id(0)
    tm = h_ref.shape[0]
    _stage_indices(p_hbm.at[i], idx, sem_i)

    def issue(r, c):
        src = h_ref.at[pl.ds(r, 1), :]
        for k in range(TOP_K):
            pltpu.make_async_copy(src, xs_hbm.at[pl.ds(idx[k * tm + r], 1), :], sem_x).start()
        return c

    lax.fori_loop(0, tm, issue, 0, unroll=ROW_DMA_UNROLL)
    for k in range(TOP_K):
        pltpu.make_async_copy(h_ref, xs_hbm.at[pl.ds(0, tm), :], sem_x).wait()


def _dispatch(h_all, slot, n_slot_rows):
    ntiles = slot.shape[0]
    d = h_all.shape[1]
    p = jnp.transpose(slot, (0, 2, 1)).reshape(ntiles, TOP_K * TM)
    xs0 = jnp.zeros((n_slot_rows, d), h_all.dtype)
    return pl.pallas_call(
        _dispatch_kernel,
        grid=(ntiles,),
        in_specs=[pl.BlockSpec(memory_space=pl.ANY), pl.BlockSpec((TM, d), lambda i: (i, 0)),
                  pl.BlockSpec(memory_space=pl.ANY)],
        out_specs=pl.BlockSpec(memory_space=pl.ANY),
        out_shape=jax.ShapeDtypeStruct((n_slot_rows, d), h_all.dtype),
        scratch_shapes=[pltpu.SMEM((TOP_K * TM,), jnp.int32),
                        pltpu.SemaphoreType.DMA(()), pltpu.SemaphoreType.DMA(())],
        input_output_aliases={2: 0},
        compiler_params=_cp("arbitrary"),
        name="dispatch",
    )(p, h_all, xs0)


def _cast_kernel(w_ref, o_ref):
    o_ref[...] = w_ref[...].astype(o_ref.dtype)


def _layer_bf16(w_stack, l):
    _, e, a, b = w_stack.shape
    return pl.pallas_call(
        _cast_kernel,
        grid=(e,),
        in_specs=[pl.BlockSpec((None, None, a, b), lambda i: (l, i, 0, 0))],
        out_specs=pl.BlockSpec((None, a, b), lambda i: (i, 0, 0)),
        out_shape=jax.ShapeDtypeStruct((e, a, b), BF16),
        compiler_params=_cp("parallel"),
        name="cast_bf16",
    )(w_stack)


def _moe_kernel(te_ref, nv_ref, x_ref, wg_ref, wu_ref, wd_ref, o_ref):
    del te_ref
    valid = pl.program_id(0) < nv_ref[0]

    @pl.when(valid)
    def _():
        half = x_ref.shape[1]
        x_lo, x_hi = _unpack_rows(x_ref[...])
        x_lo, x_hi = x_lo.astype(BF16), x_hi.astype(BF16)

        def proj(w_ref):
            return (jnp.dot(x_lo, w_ref[:half, :], preferred_element_type=F32)
                    + jnp.dot(x_hi, w_ref[half:, :], preferred_element_type=F32))

        g = proj(wg_ref)
        u = proj(wu_ref)
        hid = (g * _sigmoid(g) * u).astype(BF16)
        o_ref[...] = _pack_rows(jnp.dot(hid, wd_ref[...], preferred_element_type=F32))

    @pl.when(jnp.logical_not(valid))
    def _():
        o_ref[...] = jnp.zeros_like(o_ref)


def _moe_experts(xs, tile_expert, n_valid, w_gate, w_up, w_down, tm):
    n_rows, half = xs.shape
    d, f = w_gate.shape[1:]
    xi = lambda i, te, nv: (jnp.maximum(jnp.minimum(i, nv[0] - 1), 0), 0)
    grid_spec = pltpu.PrefetchScalarGridSpec(
        num_scalar_prefetch=2,
        grid=(n_rows // tm,),
        in_specs=[pl.BlockSpec((tm, half), xi),
                  pl.BlockSpec((None, d, f), lambda i, te, nv: (te[i], 0, 0)),
                  pl.BlockSpec((None, d, f), lambda i, te, nv: (te[i], 0, 0)),
                  pl.BlockSpec((None, f, d), lambda i, te, nv: (te[i], 0, 0))],
        out_specs=pl.BlockSpec((tm, half), lambda i, te, nv: (i, 0)),
    )
    return pl.pallas_call(
        _moe_kernel,
        grid_spec=grid_spec,
        out_shape=jax.ShapeDtypeStruct((n_rows, half), U32),
        compiler_params=_cp("arbitrary"),
        name="moe_experts",
    )(tile_expert, n_valid, xs, w_gate, w_up, w_down)


def _combine_kernel(p_hbm, ys_hbm, x_ref, slab_ref, ga_ref, g_ref, o_ref, idx, buf, sem_i, sem_x, *, final):
    i = pl.program_id(0)
    n = pl.num_programs(0)
    rows = idx.shape[0]
    tm = rows // TOP_K

    def fetch(tile, s):
        _stage_indices(p_hbm.at[tile], idx, sem_i)

        def issue(r, c):
            pltpu.make_async_copy(ys_hbm.at[pl.ds(idx[r], 1), :], buf.at[s, pl.ds(r, 1), :], sem_x.at[s]).start()
            return c

        lax.fori_loop(0, rows, issue, 0, unroll=ROW_DMA_UNROLL)

    @pl.when(i == 0)
    def _():
        fetch(0, 0)

    @pl.when(i + 1 < n)
    def _():
        fetch(i + 1, (i + 1) % 2)

    s = i % 2
    pltpu.make_async_copy(ys_hbm.at[pl.ds(0, rows), :], buf.at[s], sem_x.at[s]).wait()
    slab = slab_ref[...]
    y_lo = y_hi = None
    for k in range(TOP_K):
        w_k = slab[:, TOP_K + k:TOP_K + k + 1]
        lo, hi = _unpack_rows(buf[s, k * tm:(k + 1) * tm, :])
        y_lo = w_k * lo if y_lo is None else y_lo + w_k * lo
        y_hi = w_k * hi if y_hi is None else y_hi + w_k * hi
    x = x_ref[...] + ga_ref[...] * jnp.concatenate([y_lo, y_hi], axis=1)
    if final:
        x = x * lax.rsqrt(jnp.mean(x * x, axis=-1, keepdims=True) + NORM_EPS) * g_ref[...]
    o_ref[...] = x


def _combine(ys, slot, slab, x_all, mod3, g_final, ntiles, nlt, tpb, nb, final):
    d = x_all.shape[1]
    tm = TM_COMB
    f = TM // tm
    n = ntiles * f
    p = jnp.transpose(slot.reshape(n, tm, TOP_K), (0, 2, 1)).reshape(n, TOP_K * tm)
    return pl.pallas_call(
        functools.partial(_combine_kernel, final=final),
        grid=(n,),
        in_specs=[pl.BlockSpec(memory_space=pl.ANY), pl.BlockSpec(memory_space=pl.ANY),
                  pl.BlockSpec((tm, d), lambda i: (i, 0)),
                  pl.BlockSpec((tm, 128), lambda i: (i, 0)),
                  pl.BlockSpec((None, 1, d), lambda i: (jnp.where(i < nlt * f, i // (tpb * f), nb), 0, 5)),
                  pl.BlockSpec((1, d), lambda i: (0, 0))],
        out_specs=pl.BlockSpec((tm, d), lambda i: (i, 0)),
        out_shape=jax.ShapeDtypeStruct((n * tm, d), F32),
        scratch_shapes=[pltpu.SMEM((TOP_K * tm,), jnp.int32),
                        pltpu.VMEM((2, TOP_K * tm, ys.shape[1]), ys.dtype),
                        pltpu.SemaphoreType.DMA(()), pltpu.SemaphoreType.DMA((2,))],
        compiler_params=_cp("arbitrary"),
        name="combine",
    )(p, ys, x_all, slab, mod3, g_final.reshape(1, d))


def kernel(x, c, ctx, c_ctx, w_mod, b_mod, g_norm1, g_norm2, g_final, w_in, w_merge, w_branch, w_out, w_hconv,
           hy_bias, hf_w1, hf_b1, hf_w2, hf_b2, hf_w3, hf_freq, w_sconv, w_lconv, lru_wa, lru_ba, lru_wx, lru_bx,
           lru_lambda, w_rg, b_rg, w_re, b_re, w_e_gate, w_e_up, w_e_down):
    nb, seq, d = x.shape
    cl = ctx.shape[1]
    depth = w_in.shape[0]
    db = w_branch.shape[2]
    gdim = db // FNET_GROUPS
    ngroups = w_rg.shape[2]
    epg = w_re.shape[3]
    n_experts = ngroups * epg
    t_lat, t_ctx = nb * seq, nb * cl
    t_all = t_lat + t_ctx
    nlt, tpb = t_lat // TM, seq // TM
    nat = t_all // TM
    tl_lat, tl_ctx = min(TL, seq), min(TL, cl)

    x_all = jnp.concatenate([x.reshape(t_lat, d), ctx.reshape(t_ctx, d)], axis=0)
    nrows = -(-(nb + 1) // 8) * 8
    cvec = jnp.zeros((nrows, d), F32).at[:nb].set(c).at[nb].set(c_ctx)
    mods = _adaln(cvec, w_mod, b_mod)

    use_fft = nb % 2 == 0 and seq % (8 * FFT_N2) == 0
    if use_fft:
        hyfft = _fft_consts(seq)
    else:
        c_lat, sf_lat, si_lat = _dft_mats(seq)
        fwd_lat, inv_lat = (c_lat, sf_lat), (c_lat, si_lat)
    c_ctx, sf_ctx, si_ctx = _dft_mats(cl)
    fwd_ctx, inv_ctx = (c_ctx, sf_ctx), (c_ctx, si_ctx)
    fc_lat, fs_lat, w_pq = _fnet_mats(seq, gdim, FNET_GROUPS)
    fc_ctx, fs_ctx, _ = _fnet_mats(cl, gdim, FNET_GROUPS)

    out = None
    for l in range(depth):
        last = l == depth - 1
        mod3 = mods[l].reshape(nrows, 1, 6 * d)
        u_all, h_bf = _inproj(x_all, mod3, g_norm1[l], w_in[l].astype(BF16), nlt, tpb, nb)

        wa = [_block_diag(lru_wa[l, dd]) for dd in range(2)]
        wx = [_block_diag(lru_wx[l, dd]) for dd in range(2)]
        ba = [lru_ba[l, dd].reshape(1, db) for dd in range(2)]
        bx = [lru_bx[l, dd].reshape(1, db) for dd in range(2)]
        lam = [lru_lambda[l, dd].reshape(1, db) for dd in range(2)]
        zeros_st = jnp.zeros((nb, 1, db), F32)
        rows_out = t_lat if last else t_all

        def lru(row0, length, tl, rw, h0f, h0b, combine, out_init):
            hf, stf = _lru_pass(u_all, w_lconv[l], wa[0], wx[0], ba[0], bx[0], lam[0], h0f,
                                row0, nb, length, tl, rw, db, False)
            if combine:
                y, stb = _lru_pass(u_all, w_lconv[l], wa[1], wx[1], ba[1], bx[1], lam[1], h0b,
                                   row0, nb, length, tl, rw, db, True, hf=hf, out_init=out_init, t_all=rows_out)
            else:
                y, stb = _lru_pass(u_all, w_lconv[l], wa[1], wx[1], ba[1], bx[1], lam[1], h0b,
                                   row0, nb, length, tl, rw, db, True)
            return y, stf, stb

        bias0, bias1 = hy_bias[l, 0].reshape(1, db), hy_bias[l, 1].reshape(1, db)

        def hyena_dense(row0, length, fwd, inv, v, x1, x2, v_bf, out_init):
            filt = _hyena_filters(length, hf_w1[l], hf_b1[l], hf_w2[l], hf_b2[l], hf_w3[l], hf_freq[l], db)
            pqr = _filter_spectra(fwd, filt, length, db)
            yhat = _hy_fwd(fwd, v_bf, nb, length, db, pqr[0])
            z2 = _hy_inv(inv, yhat, v, 0, x1, 0, bias0, nb, length, db, 0, nb * length, None)
            yhat = _hy_fwd(fwd, z2, nb, length, db, pqr[1])
            return _hy_inv(inv, yhat, z2, 0, x2, 0, bias1, nb, length, db, row0, rows_out, out_init)

        def hyena_fft(length, v, x1, x2, v_bf, out_init):
            filt = _hyena_filters(length, hf_w1[l], hf_b1[l], hf_w2[l], hf_b2[l], hf_w3[l], hf_freq[l], db,
                                  reverse_bwd=True)
            kspec = _fft_filter_spectra(hyfft, filt, length, db)
            z2 = _fft_conv(hyfft, v_bf, kspec[0], v, x1, bias0, nb, db, None, nb * length)
            return _fft_conv(hyfft, z2, kspec[1], z2, x2, bias1, nb, db, out_init, rows_out)

        zbuf = lambda: None if last else jnp.zeros((rows_out, db), BF16)
        if last:
            _, st_f, st_b = lru(t_lat, cl, tl_ctx, cl, zeros_st, zeros_st, False, None)
        else:
            y_lru_c, st_f, st_b = lru(t_lat, cl, tl_ctx, cl, zeros_st, zeros_st, True, zbuf())
        y_lru, _, _ = lru(0, seq, tl_lat, GRID_W, st_f, st_b, True, None if last else y_lru_c)

        pq = _fnet_pq(u_all, w_pq, db)
        v, x1, x2, v_bf, y_sc = _conv_stage(u_all, w_hconv[l], w_sconv[l], 0, t_lat, tl_lat, GRID_W, db,
                                            jnp.zeros((rows_out, db), BF16))
        y_fn = _fnet_dft(fc_lat, fs_lat, pq, 0, nb, seq, db, gdim, zbuf(), rows_out)
        if not last:
            vc, x1c, x2c, vc_bf, y_sc = _conv_stage(u_all, w_hconv[l], w_sconv[l], t_lat, t_ctx, tl_ctx, cl, db,
                                                    y_sc)
            y_fn = _fnet_dft(fc_ctx, fs_ctx, pq, t_lat, nb, cl, db, gdim, y_fn, rows_out)
            y_hy = hyena_dense(t_lat, cl, fwd_ctx, inv_ctx, vc, x1c, x2c, vc_bf, zbuf())
        else:
            y_hy = None
        if use_fft:
            y_hy = hyena_fft(seq, v, x1, x2, v_bf, y_hy)
        else:
            y_hy = hyena_dense(0, seq, fwd_lat, inv_lat, v, x1, x2, v_bf, y_hy)

        ntiles = nlt if last else nat
        x_mid = _merge(h_bf, (y_fn, y_hy, y_sc, y_lru), w_merge[l].astype(BF16), w_branch[l].astype(BF16),
                       w_out[l].astype(BF16), x_all, mod3, ntiles, nlt, tpb, nb)

        w_router = jnp.zeros((d, 128), F32).at[:, :ngroups].set(w_rg[l])
        w_router = w_router.at[:, ngroups:ngroups + n_experts].set(
            jnp.transpose(w_re[l], (1, 0, 2)).reshape(d, n_experts))
        b_router = jnp.zeros((1, 128), F32).at[0, :ngroups].set(b_rg[l])
        b_router = b_router.at[0, ngroups:ngroups + n_experts].set(b_re[l].reshape(-1))
        h2, slab, tile_counts = _router(x_mid, mod3, g_norm2[l], w_router, b_router, ntiles, nlt, tpb, nb,
                                        ngroups, epg)
        slot, tile_expert, n_valid, n_slot_tiles = _route_tables(slab, tile_counts, ngroups, n_experts, TM_MOE)
        xs = _dispatch(h2, slot, n_slot_tiles * TM_MOE)
        ys = _moe_experts(xs, tile_expert, n_valid, _layer_bf16(w_e_gate, l), _layer_bf16(w_e_up, l),
                          _layer_bf16(w_e_down, l), TM_MOE)
        x_all = _combine(ys, slot, slab, x_mid, mod3, g_final, ntiles, nlt, tpb, nb, last)
        if last:
            out = x_all.reshape(nb, seq, d)
    return out
```

```python
import functools
import math

import jax
import jax.numpy as jnp
from jax import lax
from jax.experimental import pallas as pl
from jax.experimental.pallas import tpu as pltpu

F32 = jnp.float32
BF16 = jnp.bfloat16
HIGHEST = lax.Precision.HIGHEST

NORM_EPS = 1e-6
GRID_W = 64
FNET_GROUPS = 4
LRU_C = 8.0
HY_DECAY_TARGET = 1e-2
HY_FAST_PCT = 0.3
HY_SLOW_PCT = 1.5
TOP_K = 2

TM = 512
TL = 1024
TR = 512
TM_MOE = 512
TM_COMB = 512
TN_IN = 1536
TN_MERGE = 512
TF_MOE = 512
VMEM_LIMIT = 56 * 1024 * 1024


def _cp(*sem):
    return pltpu.CompilerParams(dimension_semantics=sem, vmem_limit_bytes=VMEM_LIMIT)


def _sigmoid(x):
    return 0.5 * jnp.tanh(0.5 * x) + 0.5


def _adaln_kernel(c_ref, w_ref, b_ref, o_ref):
    c = c_ref[...]
    s = c * _sigmoid(c)
    o_ref[...] = jnp.dot(s, w_ref[...], preferred_element_type=F32, precision=HIGHEST) + b_ref[...]


def _adaln(cvec, w_mod, b_mod):
    nl, d, n6 = w_mod.shape
    r = cvec.shape[0]
    tn = min(1024, n6)
    return pl.pallas_call(
        _adaln_kernel,
        grid=(nl, n6 // tn),
        in_specs=[pl.BlockSpec((r, d), lambda l, j: (0, 0)),
                  pl.BlockSpec((None, d, tn), lambda l, j: (l, 0, j)),
                  pl.BlockSpec((None, 1, tn), lambda l, j: (l, 0, j))],
        out_specs=pl.BlockSpec((None, r, tn), lambda l, j: (l, 0, j)),
        out_shape=jax.ShapeDtypeStruct((nl, r, n6), F32),
        compiler_params=_cp("parallel", "parallel"),
        name="adaln",
    )(cvec, w_mod, b_mod.reshape(nl, 1, n6))


def _norm_mod(x, g, shift, scale):
    y = x * lax.rsqrt(jnp.mean(x * x, axis=-1, keepdims=True) + NORM_EPS) * g
    return y * (1.0 + scale) + shift


def _norm_mod_store(dst, x_ref, g_ref, sh_ref, sc_ref, rows=128):
    rows = min(rows, dst.shape[0])

    def body(c, carry):
        sl = pl.ds(pl.multiple_of(c * rows, rows), rows)
        dst[sl, :] = _norm_mod(x_ref[sl, :], g_ref[...], sh_ref[...], sc_ref[...]).astype(dst.dtype)
        return carry

    lax.fori_loop(0, dst.shape[0] // rows, body, 0)


U32 = jnp.uint32


def _pack_rows(v):
    half = v.shape[1] // 2
    lo = pltpu.bitcast(v[:, :half].astype(BF16).astype(F32), U32) >> 16
    hi = pltpu.bitcast(v[:, half:].astype(BF16).astype(F32), U32) & jnp.uint32(0xFFFF0000)
    return hi | lo


def _unpack_rows(w):
    lo = pltpu.bitcast(w << 16, F32)
    hi = pltpu.bitcast(w & jnp.uint32(0xFFFF0000), F32)
    return lo, hi


def _mod_spec(d, chunk, nlt, tpb, nb):
    return pl.BlockSpec((None, 1, d), lambda i, *_: (jnp.where(i < nlt, i // tpb, nb), 0, chunk))


def _inproj_kernel(x_ref, sh_ref, sc_ref, g_ref, w_ref, u_ref, h_ref, *, tn):
    j = pl.program_id(1)

    @pl.when(j == 0)
    def _():
        _norm_mod_store(h_ref, x_ref, g_ref, sh_ref, sc_ref)

    for jc in range(w_ref.shape[1] // tn):
        @pl.when(j == jc)
        def _(jc=jc):
            u_ref[...] = jnp.dot(h_ref[...], w_ref[:, jc * tn:(jc + 1) * tn], preferred_element_type=F32)


def _inproj(x_all, mod3, g, w_bf, nlt, tpb, nb):
    t, d = x_all.shape
    n = w_bf.shape[1]
    tn = TN_IN if n % TN_IN == 0 else n
    return pl.pallas_call(
        functools.partial(_inproj_kernel, tn=tn),
        grid=(t // TM, n // tn),
        in_specs=[pl.BlockSpec((TM, d), lambda i, j: (i, 0)),
                  _mod_spec(d, 0, nlt, tpb, nb), _mod_spec(d, 1, nlt, tpb, nb),
                  pl.BlockSpec((1, d), lambda i, j: (0, 0)),
                  pl.BlockSpec((d, n), lambda i, j: (0, 0))],
        out_specs=[pl.BlockSpec((TM, tn), lambda i, j: (i, j)), pl.BlockSpec((TM, d), lambda i, j: (i, 0))],
        out_shape=[jax.ShapeDtypeStruct((t, n), F32), jax.ShapeDtypeStruct((t, d), BF16)],
        compiler_params=_cp("parallel", "arbitrary"),
        name="inproj",
    )(x_all, mod3, mod3, g.reshape(1, d), w_bf)


def _dwconv(x, w_ref, left, rw):
    tl = x.shape[0]
    pos = lax.broadcasted_iota(jnp.int32, x.shape, 0) % rw
    acc = None
    for k in range(w_ref.shape[0]):
        off = k - left
        if off == 0:
            term = x
        else:
            shifted = pltpu.roll(x, (-off) % tl, axis=0)
            ok = (pos + off >= 0) & (pos + off < rw)
            term = jnp.where(ok, shifted, 0.0)
        term = term * w_ref[k:k + 1, :]
        acc = term if acc is None else acc + term
    return acc


def _conv_kernel(hv_ref, h1_ref, h2_ref, sb_ref, sc_ref, sh_ref, wh_ref, ws_ref, ys_in,
                 v_ref, x1_ref, x2_ref, vbf_ref, ys_ref, *, rw, db):
    del ys_in
    for k, (ref, dst) in enumerate(((hv_ref, v_ref), (h1_ref, x1_ref), (h2_ref, x2_ref))):
        y = _dwconv(ref[...], wh_ref.at[:, k * db:(k + 1) * db], 1, rw)
        dst[...] = y
        if k == 0:
            vbf_ref[...] = y.astype(BF16)
    ys_ref[...] = (sb_ref[...] * _dwconv(sc_ref[...] * sh_ref[...], ws_ref, 1, rw)).astype(BF16)


def _conv_stage(u_all, w_hconv, w_sconv, row0, nrows, tl, rw, db, ys_init):
    ob = row0 // tl
    col = lambda c: pl.BlockSpec((tl, db), lambda i: (ob + i, c))
    own = pl.BlockSpec((tl, db), lambda i: (i, 0))
    return pl.pallas_call(
        functools.partial(_conv_kernel, rw=rw, db=db),
        grid=(nrows // tl,),
        in_specs=[col(1), col(2), col(3), col(4), col(5), col(6),
                  pl.BlockSpec(w_hconv.shape, lambda i: (0, 0)),
                  pl.BlockSpec(w_sconv.shape, lambda i: (0, 0)),
                  pl.BlockSpec(memory_space=pl.ANY)],
        out_specs=[own, own, own, own, pl.BlockSpec((tl, db), lambda i: (ob + i, 0))],
        out_shape=[jax.ShapeDtypeStruct((nrows, db), F32)] * 3
                  + [jax.ShapeDtypeStruct((nrows, db), BF16), jax.ShapeDtypeStruct(ys_init.shape, BF16)],
        input_output_aliases={8: 4},
        compiler_params=_cp("parallel"),
        name="conv",
    )(*([u_all] * 6), w_hconv, w_sconv, ys_init)


def _trig_outer(l, period):
    q = 1 << ((l.bit_length() - 1) // 2)
    n = lax.broadcasted_iota(jnp.int32, (1, l), 1)
    scale = 2.0 * math.pi / period

    def table(rows, step):
        r = lax.broadcasted_iota(jnp.int32, (rows, 1), 0) * step
        ang = ((r * n) % period).astype(F32) * scale
        return jnp.cos(ang), jnp.sin(ang)

    ac, as_ = table(l // q, q)
    bc, bs = table(q, 1)
    c = ac[:, None, :] * bc[None] - as_[:, None, :] * bs[None]
    s = as_[:, None, :] * bc[None] + ac[:, None, :] * bs[None]
    return c.reshape(l, l), s.reshape(l, l)


def _dft_mats(l):
    c, s = _trig_outer(l, 2 * l)
    k = lax.broadcasted_iota(jnp.int32, (l, l), 0)
    n = lax.broadcasted_iota(jnp.int32, (l, l), 1)
    alt_n = jnp.where(n % 2 == 0, 1.0, -1.0).astype(F32)
    alt_k = jnp.where(k % 2 == 0, 1.0, -1.0).astype(F32)
    return c.astype(BF16), jnp.where(k == 0, alt_n, -s).astype(BF16), jnp.where(n == 0, alt_k, -s).astype(BF16)


def _fnet_mats(l, gdim, groups):
    c, s = _trig_outer(l, l)
    cg, sg = _trig_outer(gdim, gdim)
    eye = jnp.eye(groups, dtype=F32)
    w_pq = jnp.concatenate([jnp.kron(eye, cg), jnp.kron(eye, sg)], axis=1).astype(BF16)
    return c.astype(BF16), (-s).astype(BF16), w_pq


def _fnet_pq_kernel(u_ref, w_ref, o_ref):
    o_ref[...] = jnp.dot(u_ref[...].astype(BF16), w_ref[...], preferred_element_type=F32).astype(BF16)


def _fnet_pq(u_all, w_pq, db):
    t = u_all.shape[0]
    return pl.pallas_call(
        _fnet_pq_kernel,
        grid=(t // TM,),
        in_specs=[pl.BlockSpec((TM, db), lambda i: (i, 0)),
                  pl.BlockSpec(w_pq.shape, lambda i: (0, 0))],
        out_specs=pl.BlockSpec((TM, 2 * db), lambda i: (i, 0)),
        out_shape=jax.ShapeDtypeStruct((t, 2 * db), BF16),
        compiler_params=_cp("parallel"),
        name="fnet_pq",
    )(u_all, w_pq)


def _fnet_dft_kernel(d0_ref, d1_ref, x_ref, *rest, db, scale):
    o_ref = rest[-1]
    acc = jnp.dot(d0_ref[...], x_ref[:, :db], preferred_element_type=F32)
    acc = acc + jnp.dot(d1_ref[...], x_ref[:, db:], preferred_element_type=F32)
    o_ref[...] = (acc * scale).astype(BF16)


def _mat_spec(tr, l):
    return pl.BlockSpec((tr, l), lambda i, b: (i, 0))


def _fnet_dft(d0, d1, pq, row0, nb, l, db, gdim, out_init, t_all):
    tr = min(TR, l)
    nrt = l // tr
    args = [d0, d1, pq]
    in_specs = [_mat_spec(tr, l), _mat_spec(tr, l),
                pl.BlockSpec((l, 2 * db), lambda i, b: (row0 // l + b, 0))]
    aliases = {}
    if out_init is not None:
        args.append(out_init)
        in_specs.append(pl.BlockSpec(memory_space=pl.ANY))
        aliases = {3: 0}
    return pl.pallas_call(
        functools.partial(_fnet_dft_kernel, db=db, scale=1.0 / math.sqrt(l * gdim)),
        grid=(nrt, nb),
        in_specs=in_specs,
        out_specs=pl.BlockSpec((tr, db), lambda i, b: (row0 // tr + b * nrt + i, 0)),
        out_shape=jax.ShapeDtypeStruct((t_all, db), BF16),
        input_output_aliases=aliases,
        compiler_params=_cp("parallel", "parallel"),
        name="fnet_dft",
    )(*args)


def _hy_fwd_kernel(f0_ref, f1_ref, x_ref, *rest, mult):
    o_ref = rest[-1]
    zre = jnp.dot(f0_ref[...], x_ref[...], preferred_element_type=F32)
    zim = jnp.dot(f1_ref[...], x_ref[...], preferred_element_type=F32)
    if mult:
        k_ref = rest[0]
        p, q, r = k_ref[0], k_ref[1], k_ref[2]
        o_ref[0] = (zre * p - zim * q).astype(o_ref.dtype)
        o_ref[1] = (zre * q + zim * r).astype(o_ref.dtype)
    else:
        o_ref[0] = zre
        o_ref[1] = zim


def _hy_fwd(fwd, x, nb, l, db, pqr):
    tr = min(TR, l)
    args = [fwd[0], fwd[1], x]
    in_specs = [_mat_spec(tr, l), _mat_spec(tr, l),
                pl.BlockSpec((l, db), lambda i, b: (b, 0))]
    if pqr is not None:
        args.append(pqr)
        in_specs.append(pl.BlockSpec((3, tr, db), lambda i, b: (0, i, 0)))
    return pl.pallas_call(
        functools.partial(_hy_fwd_kernel, mult=pqr is not None),
        grid=(l // tr, nb),
        in_specs=in_specs,
        out_specs=pl.BlockSpec((None, 2, tr, db), lambda i, b: (b, 0, i, 0)),
        out_shape=jax.ShapeDtypeStruct((nb, 2, l, db), F32 if pqr is None else BF16),
        compiler_params=_cp("parallel", "parallel"),
        name="hy_fwd",
    )(*args)


def _hy_inv_kernel(g0_ref, g1_ref, y_ref, zp_ref, gate_ref, bias_ref, *rest):
    o_ref = rest[-1]
    y = jnp.dot(g0_ref[...], y_ref[0], preferred_element_type=F32)
    y = y + jnp.dot(g1_ref[...], y_ref[1], preferred_element_type=F32)
    o_ref[...] = (gate_ref[...] * (y + bias_ref[...] * zp_ref[...].astype(F32))).astype(BF16)


def _hy_inv(inv, yhat, zprev, zcol, gate, gcol, bias, nb, l, db, out_row0, out_rows, out_init):
    tr = min(TR, l)
    nrt = l // tr
    args = [inv[0], inv[1], yhat, zprev, gate, bias]
    in_specs = [_mat_spec(tr, l), _mat_spec(tr, l),
                pl.BlockSpec((None, 2, l, db), lambda i, b: (b, 0, 0, 0)),
                pl.BlockSpec((tr, db), lambda i, b: (b * nrt + i, zcol)),
                pl.BlockSpec((tr, db), lambda i, b: (b * nrt + i, gcol)),
                pl.BlockSpec((1, db), lambda i, b: (0, 0))]
    aliases = {}
    if out_init is not None:
        args.append(out_init)
        in_specs.append(pl.BlockSpec(memory_space=pl.ANY))
        aliases = {6: 0}
    return pl.pallas_call(
        _hy_inv_kernel,
        grid=(nrt, nb),
        in_specs=in_specs,
        out_specs=pl.BlockSpec((tr, db), lambda i, b: (out_row0 // tr + b * nrt + i, 0)),
        out_shape=jax.ShapeDtypeStruct((out_rows, db), BF16),
        input_output_aliases=aliases,
        compiler_params=_cp("parallel", "parallel"),
        name="hy_inv",
    )(*args)


def _hyena_filters(length, hf_w1, hf_b1, hf_w2, hf_b2, hf_w3, hf_freq, db, reverse_bwd=False):
    emb = hf_w1.shape[0]
    nbands = (emb - 1) // 2
    bands = jnp.linspace(1e-4, nbands - 1, nbands, dtype=F32)[None, :]
    min_decay = math.log(HY_DECAY_TARGET) / HY_SLOW_PCT
    max_decay = math.log(HY_DECAY_TARGET) / HY_FAST_PCT
    deltas = jnp.abs(jnp.linspace(min_decay, max_decay, db, dtype=F32))
    w3 = hf_w3.reshape(hf_w3.shape[0], -1, 2, db)

    def direction(pos, d):
        t = (pos * (1.0 / (length - 1)))[:, None]
        w = (2.0 * math.pi / length) * pos[:, None]
        feats = jnp.concatenate([t, jnp.cos(bands * w), -jnp.sin(bands * w)], axis=-1)
        z = jnp.sin(hf_freq[0] * (jnp.dot(feats, hf_w1, precision=HIGHEST) + hf_b1))
        z = jnp.sin(hf_freq[1] * (jnp.dot(z, hf_w2, precision=HIGHEST) + hf_b2))
        f = jnp.einsum("lf,fod->lod", z, w3[:, :, d, :], precision=HIGHEST)
        return f * jnp.exp(-t * deltas)[:, None, :]

    pos = jnp.arange(length, dtype=F32)
    filt = jnp.stack([direction(pos, 0), direction(length - 1 - pos if reverse_bwd else pos, 1)], axis=2)
    return filt * lax.rsqrt(jnp.sum(filt * filt, axis=(0, 2), keepdims=True) + NORM_EPS)


def _filter_spectra(fwd, filt, l, db):
    n_ord = filt.shape[1]
    cols = jnp.transpose(filt, (1, 2, 0, 3))
    cols = cols.at[:, 1, 0, :].set(0.0)
    x = cols.reshape(n_ord * 2 * l, db).astype(BF16)
    spec = _hy_fwd(fwd, x, n_ord * 2, l, db, None).reshape(n_ord, 2, 2, l, db)
    hf, hb = spec[:, 0], spec[:, 1]
    k_re = hf[:, 0] + hb[:, 0]
    k_im = hf[:, 1] - hb[:, 1]
    k_nyq = hf[:, 1, 0] + hb[:, 1, 0]
    first = (jnp.arange(l) == 0)[None, :, None]
    scale = jnp.where(first, 1.0 / (2 * l), 2.0 / (2 * l)).astype(F32)
    p = k_re * scale
    q = jnp.where(first, 0.0, k_im) * scale
    r = jnp.where(first, k_nyq[:, None, :], k_re) * scale
    return jnp.stack([p, q, r], axis=1)


FFT_N2 = 128
FFT_NT = 16
FFT_KC = 8


def _fft_consts(l):
    n, n2 = 2 * l, FFT_N2
    n1 = n // n2
    h = n1 // 2
    cat = jnp.concatenate
    ia = jnp.arange(n1, dtype=jnp.int32)
    ang = ((ia[:, None] * ia[None, :]) % n1).astype(F32) * (2.0 * math.pi / n1)
    fr, fi = jnp.cos(ang), -jnp.sin(ang)
    mat_data = cat([cat([fr[:, :h], -fi[:, :h]], 1), cat([fi[:, :h], fr[:, :h]], 1)], 0)
    mat_filt = cat([fr, fi], 0)
    gr, gi = fr[:, :h].T, -fi[:, :h].T
    mat_inv = jnp.stack([cat([gr, -gi], 1), cat([gi, gr], 1)])
    k = ia[:, None, None] + n1 * jnp.arange(n2, dtype=jnp.int32)[None, :, None]
    nn = jnp.arange(n2, dtype=jnp.int32)[None, None, :]
    angb = ((k * nn) % n).astype(F32) * (2.0 * math.pi / n)
    er, ei = jnp.cos(angb), -jnp.sin(angb)
    mb = cat([cat([er, -ei], 2), cat([ei, er], 2)], 1)
    eye = jnp.eye(FFT_NT, dtype=F32)
    kron = lambda a: jnp.kron(a, eye).astype(BF16)
    return dict(n1=n1, h=h, mat_data=kron(mat_data), mat_filt=kron(mat_filt * (1.0 / n)),
                mat_inv=jnp.stack([kron(mat_inv[0]), kron(mat_inv[1])]),
                mb=mb.astype(BF16), mib=jnp.transpose(mb, (0, 2, 1)).astype(BF16))


def _fft_a_kernel(m_ref, a_ref, b_ref, o_ref):
    h, nt, db = a_ref.shape
    x = jnp.concatenate([a_ref[...].reshape(h * nt, db), b_ref[...].reshape(h * nt, db)], axis=0)
    res = jnp.dot(m_ref[...], x, preferred_element_type=F32)
    o_ref[...] = res.astype(o_ref.dtype).reshape(o_ref.shape)


def _fft_stage_a(mat, x4, npairs, stride, offset):
    _, h, n2, db = x4.shape
    nt = FFT_NT
    rows = mat.shape[0] // nt
    return pl.pallas_call(
        _fft_a_kernel,
        grid=(npairs, n2 // nt),
        in_specs=[pl.BlockSpec(mat.shape, lambda p, j: (0, 0)),
                  pl.BlockSpec((None, h, nt, db), lambda p, j: (p * stride, 0, j, 0)),
                  pl.BlockSpec((None, h, nt, db), lambda p, j: (p * stride + offset, 0, j, 0))],
        out_specs=pl.BlockSpec((None, rows, nt, db), lambda p, j: (p, 0, j, 0)),
        out_shape=jax.ShapeDtypeStruct((npairs, rows, n2, db), BF16),
        compiler_params=_cp("parallel", "parallel"),
        name="fft_a",
    )(mat, x4, x4)


def _fft_b_kernel(m_ref, ar_ref, ai_ref, o_ref):
    for kk in range(m_ref.shape[0]):
        a = jnp.concatenate([ar_ref[kk], ai_ref[kk]], axis=0)
        o_ref[kk] = jnp.dot(m_ref[kk], a, preferred_element_type=F32)


def _fft_bb_kernel(m_ref, mi_ref, ar_ref, ai_ref, k_ref, o_ref):
    n2 = ar_ref.shape[1]
    for kk in range(m_ref.shape[0]):
        a = jnp.concatenate([ar_ref[kk], ai_ref[kk]], axis=0)
        x = jnp.dot(m_ref[kk], a, preferred_element_type=F32)
        xr, xi = x[:n2], x[n2:]
        kr, ki = k_ref[kk, :n2], k_ref[kk, n2:]
        y = jnp.concatenate([(xr * kr - xi * ki).astype(BF16), (xr * ki + xi * kr).astype(BF16)], axis=0)
        b = jnp.dot(mi_ref[kk], y, preferred_element_type=F32)
        o_ref[0, kk] = b[:n2].astype(o_ref.dtype)
        o_ref[1, kk] = b[n2:].astype(o_ref.dtype)


def _fft_stage_b(fc, a, kspec, db):
    mb, mib = fc["mb"], fc["mib"]
    npairs = a.shape[0]
    n1, r2, _ = mb.shape
    n2 = r2 // 2
    kc = min(FFT_KC, n1)
    a5 = a.reshape(npairs, 2, n1, n2, db)
    mspec = pl.BlockSpec((kc, r2, r2), lambda c, p: (c, 0, 0))
    aspec = lambda part: pl.BlockSpec((None, None, kc, n2, db), lambda c, p: (p, part, c, 0, 0))
    if kspec is None:
        kern, args, in_specs = _fft_b_kernel, [mb, a5, a5], [mspec, aspec(0), aspec(1)]
        out_spec = pl.BlockSpec((None, kc, r2, db), lambda c, p: (p, c, 0, 0))
        out_shape = jax.ShapeDtypeStruct((npairs, n1, r2, db), F32)
    else:
        kern, args = _fft_bb_kernel, [mb, mib, a5, a5, kspec]
        in_specs = [mspec, mspec, aspec(0), aspec(1), pl.BlockSpec((kc, r2, db), lambda c, p: (c, 0, 0))]
        out_spec = pl.BlockSpec((None, 2, kc, n2, db), lambda c, p: (p, 0, c, 0, 0))
        out_shape = jax.ShapeDtypeStruct((npairs, 2, n1, n2, db), BF16)
    return pl.pallas_call(
        kern,
        grid=(n1 // kc, npairs),
        in_specs=in_specs,
        out_specs=out_spec,
        out_shape=out_shape,
        compiler_params=_cp("parallel", "parallel"),
        name="fft_b",
    )(*args)


def _ifft_a_kernel(m_ref, b_ref, zp_ref, gate_ref, bias_ref, *rest):
    o_ref = rest[-1]
    h, nt, db = o_ref.shape
    bm = b_ref[...].reshape(b_ref.shape[0] * nt, db)
    y = jnp.dot(m_ref[...], bm, preferred_element_type=F32)
    zp = zp_ref[...].reshape(h * nt, db).astype(F32)
    out = gate_ref[...].reshape(h * nt, db) * (y + bias_ref[...] * zp)
    o_ref[...] = out.astype(o_ref.dtype).reshape(o_ref.shape)


def _ifft_stage_a(mat_inv, b5, zprev, gate, bias, nb, db, out_init, out_rows):
    npairs, _, n1, n2, _ = b5.shape
    nt = FFT_NT
    h = mat_inv.shape[1] // nt
    b4 = b5.reshape(npairs, 2 * n1, n2, db)
    view = lambda a: a.reshape(a.shape[0] // n2, n2, db)
    blk = pl.BlockSpec((h, nt, db), lambda b, j: (b, j, 0))
    args = [mat_inv, b4, view(zprev), view(gate), bias]
    in_specs = [pl.BlockSpec((None,) + mat_inv.shape[1:], lambda b, j: (b // npairs, 0, 0)),
                pl.BlockSpec((None, 2 * n1, nt, db), lambda b, j: (b % npairs, 0, j, 0)),
                blk, blk, pl.BlockSpec((1, db), lambda b, j: (0, 0))]
    aliases = {}
    if out_init is not None:
        args.append(view(out_init))
        in_specs.append(pl.BlockSpec(memory_space=pl.ANY))
        aliases = {5: 0}
    out = pl.pallas_call(
        _ifft_a_kernel,
        grid=(nb, n2 // nt),
        in_specs=in_specs,
        out_specs=blk,
        out_shape=jax.ShapeDtypeStruct((out_rows // n2, n2, db), BF16),
        input_output_aliases=aliases,
        compiler_params=_cp("parallel", "parallel"),
        name="ifft_a",
    )(*args)
    return out.reshape(out_rows, db)


def _fft_filter_spectra(fc, filt, l, db):
    n_ord = filt.shape[1]
    h_fwd = jnp.transpose(filt[:, :, 0, :], (1, 0, 2))
    h_bwd = jnp.transpose(filt[:, :, 1, :], (1, 0, 2))
    ker = jnp.concatenate([h_fwd, jnp.zeros((n_ord, 1, db), F32), h_bwd[:, :l - 1, :]], axis=1)
    x4 = ker.astype(BF16).reshape(n_ord * 2, fc["h"], FFT_N2, db)
    a = _fft_stage_a(fc["mat_filt"], x4, n_ord, 2, 1)
    return _fft_stage_b(fc, a, None, db)


def _fft_conv(fc, z_bf, kspec, zprev, gate, bias, nb, db, out_init, out_rows):
    x4 = z_bf.reshape(nb, fc["h"], FFT_N2, db)
    a = _fft_stage_a(fc["mat_data"], x4, nb // 2, 1, nb // 2)
    b5 = _fft_stage_b(fc, a, kspec, db)
    return _ifft_stage_a(fc["mat_inv"], b5, zprev, gate, bias, nb, db, out_init, out_rows)


def _lru_kernel(*refs, rw, reverse, final):
    if final:
        (x_ref, w_ref, wa_ref, wx_ref, ba_ref, bx_ref, lam_ref, h0_ref, hf_ref, ug_ref,
         _, o_ref, st_ref, a_s, b_s, carry) = refs
    else:
        (x_ref, w_ref, wa_ref, wx_ref, ba_ref, bx_ref, lam_ref, h0_ref,
         o_ref, st_ref, a_s, b_s, carry) = refs
    tl = x_ref.shape[0]

    @pl.when(pl.program_id(1) == 0)
    def _():
        carry[...] = h0_ref[...]

    xc = _dwconv(x_ref[...], w_ref, w_ref.shape[0] // 2, rw)
    xb = xc.astype(BF16)
    r = _sigmoid(jnp.dot(xb, wa_ref[...], preferred_element_type=F32) + ba_ref[...])
    ig = _sigmoid(jnp.dot(xb, wx_ref[...], preferred_element_type=F32) + bx_ref[...])
    lam = lam_ref[...]
    softplus = jnp.maximum(-lam, 0.0) + jnp.log(1.0 + jnp.exp(-jnp.abs(lam)))
    log_a = -LRU_C * r * softplus
    a = jnp.exp(log_a)
    bv = jnp.sqrt(1.0 - jnp.exp(2.0 * log_a)) * (ig * xc)

    pos8 = lax.broadcasted_iota(jnp.int32, a.shape, 0) % 8
    for s in (1, 2, 4):
        if reverse:
            a_sh, b_sh, ok = pltpu.roll(a, tl - s, axis=0), pltpu.roll(bv, tl - s, axis=0), pos8 < 8 - s
        else:
            a_sh, b_sh, ok = pltpu.roll(a, s, axis=0), pltpu.roll(bv, s, axis=0), pos8 >= s
        bv = jnp.where(ok, a * b_sh + bv, bv)
        a = jnp.where(ok, a * a_sh, a)
    a_s[...] = a
    b_s[...] = bv

    ng = tl // 8

    def body(gi, c):
        g = ng - 1 - gi if reverse else gi
        sl = pl.ds(pl.multiple_of(g * 8, 8), 8)
        h8 = b_s[sl, :] + a_s[sl, :] * c
        b_s[sl, :] = h8
        return h8[0:1, :] if reverse else h8[7:8, :]

    c_out = lax.fori_loop(0, ng, body, carry[...])
    carry[...] = c_out
    st_ref[...] = c_out
    if final:
        ug = ug_ref[...]
        gelu = 0.5 * ug * (1.0 + jnp.tanh(math.sqrt(2.0 / math.pi) * (ug + 0.044715 * ug * ug * ug)))
        o_ref[...] = ((hf_ref[...] + b_s[...]) * gelu).astype(BF16)
    else:
        o_ref[...] = b_s[...]


def _lru_pass(u_all, w_lconv, wa, wx, ba, bx, lam, h0, row0, nb, l, tl, rw, db, reverse, hf=None, out_init=None,
              t_all=None):
    final = hf is not None
    nc = l // tl
    ob = row0 // tl
    cidx = (lambda c: nc - 1 - c) if reverse else (lambda c: c)
    full = lambda a: pl.BlockSpec(a.shape, lambda b, c: (0,) * a.ndim)
    ncolx, ncolg = 7, 8
    args = [u_all, w_lconv, wa, wx, ba, bx, lam, h0]
    in_specs = [pl.BlockSpec((tl, db), lambda b, c: (ob + b * nc + cidx(c), ncolx)),
                full(w_lconv), full(wa), full(wx), full(ba), full(bx), full(lam),
                pl.BlockSpec((None, 1, db), lambda b, c: (b, 0, 0))]
    aliases = {}
    if final:
        args += [hf, u_all]
        in_specs += [pl.BlockSpec((tl, db), lambda b, c: (b * nc + cidx(c), 0)),
                     pl.BlockSpec((tl, db), lambda b, c: (ob + b * nc + cidx(c), ncolg))]
        if out_init is None:
            out_init = jnp.zeros((8, 128), BF16)
        else:
            aliases = {10: 0}
        args.append(out_init)
        in_specs.append(pl.BlockSpec(memory_space=pl.ANY))
        out_spec = pl.BlockSpec((tl, db), lambda b, c: (ob + b * nc + cidx(c), 0))
        out_shape = jax.ShapeDtypeStruct((t_all, db), BF16)
    else:
        out_spec = pl.BlockSpec((tl, db), lambda b, c: (b * nc + cidx(c), 0))
        out_shape = jax.ShapeDtypeStruct((nb * l, db), F32)
    return pl.pallas_call(
        functools.partial(_lru_kernel, rw=rw, reverse=reverse, final=final),
        grid=(nb, nc),
        in_specs=in_specs,
        out_specs=[out_spec, pl.BlockSpec((None, 1, db), lambda b, c: (b, 0, 0))],
        out_shape=[out_shape, jax.ShapeDtypeStruct((nb, 1, db), F32)],
        scratch_shapes=[pltpu.VMEM((tl, db), F32), pltpu.VMEM((tl, db), F32), pltpu.VMEM((1, db), F32)],
        input_output_aliases=aliases,
        compiler_params=_cp("arbitrary", "arbitrary"),
        name="lru_bwd" if reverse else "lru_fwd",
    )(*args)


def _block_diag(w):
    h, hd, _ = w.shape
    eye = jnp.eye(h, dtype=w.dtype)
    return (eye[:, None, :, None] * w[:, :, None, :]).reshape(h * hd, h * hd).astype(BF16)


def _merge_kernel(x_ref, h_ref, b0_ref, b1_ref, b2_ref, b3_ref, wm_ref, wb_ref, wo_ref, ga_ref, o_ref):
    j = pl.program_id(1)
    h = h_ref[...]
    m = None
    for k, br in enumerate((b0_ref, b1_ref, b2_ref, b3_ref)):
        g = jnp.dot(h, wm_ref[k], preferred_element_type=F32)
        p = jnp.dot(br[...], wb_ref[k], preferred_element_type=F32)
        term = _sigmoid(g) * p
        m = term if m is None else m + term
    part = jnp.dot(m.astype(BF16), wo_ref[...], preferred_element_type=F32)

    @pl.when(j == 0)
    def _():
        o_ref[...] = part

    @pl.when(j > 0)
    def _():
        o_ref[...] += part

    @pl.when(j == pl.num_programs(1) - 1)
    def _():
        o_ref[...] = x_ref[...] + ga_ref[...] * o_ref[...]


def _merge(h_bf, branches, wm_bf, wb_bf, wo_bf, x_all, mod3, ntiles, nlt, tpb, nb):
    t, d = x_all.shape
    db = wb_bf.shape[1]
    tn = min(TN_MERGE, d)
    row = lambda w: pl.BlockSpec((TM, w), lambda i, j: (i, 0))
    return pl.pallas_call(
        _merge_kernel,
        grid=(ntiles, d // tn),
        in_specs=[row(d), row(d), row(db), row(db), row(db), row(db),
                  pl.BlockSpec((4, d, tn), lambda i, j: (0, 0, j)),
                  pl.BlockSpec((4, db, tn), lambda i, j: (0, 0, j)),
                  pl.BlockSpec((tn, d), lambda i, j: (j, 0)),
                  _mod_spec(d, 2, nlt, tpb, nb)],
        out_specs=row(d),
        out_shape=jax.ShapeDtypeStruct((ntiles * TM, d), F32),
        compiler_params=_cp("parallel", "arbitrary"),
        name="merge",
    )(x_all, h_bf, *branches, wm_bf, wb_bf, wo_bf, mod3)


def _router_kernel(x_ref, sh_ref, sc_ref, g_ref, whi_ref, wlo_ref, br_ref, tri_ref, h_ref, r_ref, c_ref, *,
                   ngroups, epg):
    h = _norm_mod(x_ref[...], g_ref[...], sh_ref[...], sc_ref[...])
    h_ref[...] = _pack_rows(h)
    h_hi = h.astype(BF16)
    h_lo = (h - h_hi.astype(F32)).astype(BF16)
    logits = jnp.dot(h_hi, wlo_ref[...], preferred_element_type=F32)
    logits = logits + jnp.dot(h_lo, whi_ref[...], preferred_element_type=F32)
    logits = logits + jnp.dot(h_hi, whi_ref[...], preferred_element_type=F32) + br_ref[...]
    lane = lax.broadcasted_iota(jnp.int32, logits.shape, 1)
    lane_f = lane.astype(F32)
    neg = jnp.float32(-1e30)
    big = jnp.float32(1e6)
    is_g = lane < ngroups
    lg = jnp.where(is_g, logits, neg)
    mx = jnp.max(lg, axis=-1, keepdims=True)
    g_top = jnp.min(jnp.where(lg == mx, lane_f, big), axis=-1, keepdims=True)
    den = jnp.sum(jnp.where(is_g, jnp.exp(lg - mx), 0.0), axis=-1, keepdims=True)
    p_top = 1.0 / den
    el = lane_f - ngroups
    in_grp = (el >= g_top * epg) & (el < (g_top + 1.0) * epg)
    le = jnp.where(in_grp, logits, neg)
    v1 = jnp.max(le, axis=-1, keepdims=True)
    e1 = jnp.min(jnp.where(le == v1, el, big), axis=-1, keepdims=True)
    le2 = jnp.where(el == e1, neg, le)
    v2 = jnp.max(le2, axis=-1, keepdims=True)
    e2 = jnp.min(jnp.where(le2 == v2, el, big), axis=-1, keepdims=True)
    dlt = jnp.exp(v2 - v1)
    w1 = p_top / (1.0 + dlt)
    w2 = p_top * dlt / (1.0 + dlt)
    tm = logits.shape[0]
    pick1 = el == e1
    pick2 = el == e2
    onehot = jnp.where(pick1 | pick2, 1.0, 0.0)
    before = jnp.dot(tri_ref[...], onehot.astype(BF16), preferred_element_type=F32)
    rank1 = jnp.sum(jnp.where(pick1, before, 0.0), axis=-1, keepdims=True)
    rank2 = jnp.sum(jnp.where(pick2, before, 0.0), axis=-1, keepdims=True)
    out = jnp.where(lane == 0, e1, 0.0)
    out = jnp.where(lane == 1, e2, out)
    out = jnp.where(lane == 2, w1, out)
    out = jnp.where(lane == 3, w2, out)
    out = jnp.where(lane == 4, rank1, out)
    out = jnp.where(lane == 5, rank2, out)
    r_ref[...] = out
    c_ref[...] = jnp.broadcast_to(jnp.sum(onehot, axis=0, keepdims=True), c_ref.shape)


def _router(x_all, mod3, g, w_router, b_router, ntiles, nlt, tpb, nb, ngroups, epg):
    t, d = x_all.shape
    w_hi = w_router.astype(BF16)
    w_hi_lo = (w_hi, (w_router - w_hi.astype(F32)).astype(BF16))
    earlier = jnp.tril(jnp.ones((TM, TM), F32), -1).astype(BF16)
    return pl.pallas_call(
        functools.partial(_router_kernel, ngroups=ngroups, epg=epg),
        grid=(ntiles,),
        in_specs=[pl.BlockSpec((TM, d), lambda i: (i, 0)),
                  _mod_spec(d, 3, nlt, tpb, nb), _mod_spec(d, 4, nlt, tpb, nb),
                  pl.BlockSpec((1, d), lambda i: (0, 0)),
                  pl.BlockSpec((d, 128), lambda i: (0, 0)),
                  pl.BlockSpec((d, 128), lambda i: (0, 0)),
                  pl.BlockSpec((1, 128), lambda i: (0, 0)),
                  pl.BlockSpec((TM, TM), lambda i: (0, 0))],
        out_specs=[pl.BlockSpec((TM, d // 2), lambda i: (i, 0)), pl.BlockSpec((TM, 128), lambda i: (i, 0)),
                   pl.BlockSpec((None, 8, 128), lambda i: (i, 0, 0))],
        out_shape=[jax.ShapeDtypeStruct((ntiles * TM, d // 2), U32),
                   jax.ShapeDtypeStruct((ntiles * TM, 128), F32),
                   jax.ShapeDtypeStruct((ntiles, 8, 128), F32)],
        compiler_params=_cp("parallel"),
        name="router",
    )(x_all, mod3, mod3, g.reshape(1, d), w_hi, w_hi_lo[1], b_router, earlier)


def _route_tables(slab, tile_counts, ngroups, n_experts, tm):
    ntiles = tile_counts.shape[0]
    e = slab[:, 0:TOP_K].astype(jnp.int32).reshape(ntiles, TM, TOP_K)
    rank = slab[:, 4:4 + TOP_K].astype(jnp.int32).reshape(ntiles, TM, TOP_K)
    counts_t = tile_counts[:, 0, ngroups:ngroups + n_experts].astype(jnp.int32)
    base = jnp.cumsum(counts_t, axis=0) - counts_t
    counts = jnp.sum(counts_t, axis=0)
    tiles_per_e = (counts + tm - 1) // tm
    tile_end = jnp.cumsum(tiles_per_e)
    starts = (tile_end - tiles_per_e) * tm
    off = starts[None, :] + base
    sel = e[..., None] == jnp.arange(n_experts, dtype=jnp.int32)
    slot = jnp.sum(jnp.where(sel, off[:, None, None, :], 0), axis=-1) + rank
    n_slot_tiles = (ntiles * TM * TOP_K) // tm + n_experts
    n_valid = tile_end[-1]
    tile_ids = jnp.minimum(jnp.arange(n_slot_tiles, dtype=jnp.int32), n_valid - 1)
    tile_expert = jnp.sum((tile_end[None, :] <= tile_ids[:, None]).astype(jnp.int32), axis=1)
    return slot, tile_expert, n_valid.reshape(1).astype(jnp.int32), n_slot_tiles


ROW_DMA_UNROLL = 8


def _stage_indices(idx_hbm_row, idx_smem, sem_i):
    cp = pltpu.make_async_copy(idx_hbm_row, idx_smem, sem_i)
    cp.start()
    cp.wait()


def _dispatch_kernel(p_hbm, h_ref, xs_in, xs_hbm, idx, sem_i, sem_x):
    del xs_in
    i = pl.program_id(0)
    tm = h_ref.shape[0]
    _stage_indices(p_hbm.at[i], idx, sem_i)

    def issue(r, c):
        src = h_ref.at[pl.ds(r, 1), :]
        for k in range(TOP_K):
            pltpu.make_async_copy(src, xs_hbm.at[pl.ds(idx[k * tm + r], 1), :], sem_x).start(priority=k % 2)
        return c

    lax.fori_loop(0, tm, issue, 0, unroll=ROW_DMA_UNROLL)
    for k in range(TOP_K):
        pltpu.make_async_copy(h_ref, xs_hbm.at[pl.ds(0, tm), :], sem_x).wait()


def _dispatch(h_all, slot, n_slot_rows):
    ntiles = slot.shape[0]
    d = h_all.shape[1]
    p = jnp.transpose(slot, (0, 2, 1)).reshape(ntiles, TOP_K * TM)
    xs0 = jnp.zeros((n_slot_rows, d), h_all.dtype)
    return pl.pallas_call(
        _dispatch_kernel,
        grid=(ntiles,),
        in_specs=[pl.BlockSpec(memory_space=pl.ANY), pl.BlockSpec((TM, d), lambda i: (i, 0)),
                  pl.BlockSpec(memory_space=pl.ANY)],
        out_specs=pl.BlockSpec(memory_space=pl.ANY),
        out_shape=jax.ShapeDtypeStruct((n_slot_rows, d), h_all.dtype),
        scratch_shapes=[pltpu.SMEM((TOP_K * TM,), jnp.int32),
                        pltpu.SemaphoreType.DMA(()), pltpu.SemaphoreType.DMA(())],
        input_output_aliases={2: 0},
        compiler_params=_cp("arbitrary"),
        name="dispatch",
    )(p, h_all, xs0)


def _cast_kernel(w_ref, o_ref):
    o_ref[...] = w_ref[...].astype(o_ref.dtype)


def _layer_bf16(w_stack, l):
    _, e, a, b = w_stack.shape
    return pl.pallas_call(
        _cast_kernel,
        grid=(e,),
        in_specs=[pl.BlockSpec((None, None, a, b), lambda i: (l, i, 0, 0))],
        out_specs=pl.BlockSpec((None, a, b), lambda i: (i, 0, 0)),
        out_shape=jax.ShapeDtypeStruct((e, a, b), BF16),
        compiler_params=_cp("parallel"),
        name="cast_bf16",
    )(w_stack)


def _moe_kernel(te_ref, nv_ref, x_ref, wg_ref, wu_ref, wd_ref, o_ref):
    del te_ref
    valid = pl.program_id(0) < nv_ref[0]

    @pl.when(valid)
    def _():
        half = x_ref.shape[1]
        x_lo, x_hi = _unpack_rows(x_ref[...])
        x_lo, x_hi = x_lo.astype(BF16), x_hi.astype(BF16)

        def proj(w_ref):
            return (jnp.dot(x_lo, w_ref[:half, :], preferred_element_type=F32)
                    + jnp.dot(x_hi, w_ref[half:, :], preferred_element_type=F32))

        g = proj(wg_ref)
        u = proj(wu_ref)
        hid = (g * _sigmoid(g) * u).astype(BF16)
        o_ref[...] = _pack_rows(jnp.dot(hid, wd_ref[...], preferred_element_type=F32))

    @pl.when(jnp.logical_not(valid))
    def _():
        o_ref[...] = jnp.zeros_like(o_ref)


def _moe_experts(xs, tile_expert, n_valid, w_gate, w_up, w_down, tm):
    n_rows, half = xs.shape
    d, f = w_gate.shape[1:]
    xi = lambda i, te, nv: (jnp.maximum(jnp.minimum(i, nv[0] - 1), 0), 0)
    grid_spec = pltpu.PrefetchScalarGridSpec(
        num_scalar_prefetch=2,
        grid=(n_rows // tm,),
        in_specs=[pl.BlockSpec((tm, half), xi),
                  pl.BlockSpec((None, d, f), lambda i, te, nv: (te[i], 0, 0)),
                  pl.BlockSpec((None, d, f), lambda i, te, nv: (te[i], 0, 0)),
                  pl.BlockSpec((None, f, d), lambda i, te, nv: (te[i], 0, 0))],
        out_specs=pl.BlockSpec((tm, half), lambda i, te, nv: (i, 0)),
    )
    return pl.pallas_call(
        _moe_kernel,
        grid_spec=grid_spec,
        out_shape=jax.ShapeDtypeStruct((n_rows, half), U32),
        compiler_params=_cp("arbitrary"),
        name="moe_experts",
    )(tile_expert, n_valid, xs, w_gate, w_up, w_down)


def _combine_kernel(p_hbm, ys_hbm, x_ref, slab_ref, ga_ref, g_ref, o_ref, idx, buf, sem_i, sem_x, *, final):
    i = pl.program_id(0)
    n = pl.num_programs(0)
    rows = idx.shape[0]
    tm = rows // TOP_K

    def fetch(tile, s):
        _stage_indices(p_hbm.at[tile], idx, sem_i)

        def issue(t, c):
            for k in range(TOP_K):
                r = k * tm + t
                pltpu.make_async_copy(ys_hbm.at[pl.ds(idx[r], 1), :], buf.at[s, pl.ds(r, 1), :],
                                      sem_x.at[s]).start(priority=k % 2)
            return c

        lax.fori_loop(0, tm, issue, 0, unroll=ROW_DMA_UNROLL)

    @pl.when(i == 0)
    def _():
        fetch(0, 0)

    @pl.when(i + 1 < n)
    def _():
        fetch(i + 1, (i + 1) % 2)

    s = i % 2
    pltpu.make_async_copy(ys_hbm.at[pl.ds(0, rows), :], buf.at[s], sem_x.at[s]).wait()
    slab = slab_ref[...]
    y_lo = y_hi = None
    for k in range(TOP_K):
        w_k = slab[:, TOP_K + k:TOP_K + k + 1]
        lo, hi = _unpack_rows(buf[s, k * tm:(k + 1) * tm, :])
        y_lo = w_k * lo if y_lo is None else y_lo + w_k * lo
        y_hi = w_k * hi if y_hi is None else y_hi + w_k * hi
    x = x_ref[...] + ga_ref[...] * jnp.concatenate([y_lo, y_hi], axis=1)
    if final:
        x = x * lax.rsqrt(jnp.mean(x * x, axis=-1, keepdims=True) + NORM_EPS) * g_ref[...]
    o_ref[...] = x


def _combine(ys, slot, slab, x_all, mod3, g_final, ntiles, nlt, tpb, nb, final):
    d = x_all.shape[1]
    tm = TM_COMB
    f = TM // tm
    n = ntiles * f
    p = jnp.transpose(slot.reshape(n, tm, TOP_K), (0, 2, 1)).reshape(n, TOP_K * tm)
    return pl.pallas_call(
        functools.partial(_combine_kernel, final=final),
        grid=(n,),
        in_specs=[pl.BlockSpec(memory_space=pl.ANY), pl.BlockSpec(memory_space=pl.ANY),
                  pl.BlockSpec((tm, d), lambda i: (i, 0)),
                  pl.BlockSpec((tm, 128), lambda i: (i, 0)),
                  pl.BlockSpec((None, 1, d), lambda i: (jnp.where(i < nlt * f, i // (tpb * f), nb), 0, 5)),
                  pl.BlockSpec((1, d), lambda i: (0, 0))],
        out_specs=pl.BlockSpec((tm, d), lambda i: (i, 0)),
        out_shape=jax.ShapeDtypeStruct((n * tm, d), F32),
        scratch_shapes=[pltpu.SMEM((TOP_K * tm,), jnp.int32),
                        pltpu.VMEM((2, TOP_K * tm, ys.shape[1]), ys.dtype),
                        pltpu.SemaphoreType.DMA(()), pltpu.SemaphoreType.DMA((2,))],
        compiler_params=_cp("arbitrary"),
        name="combine",
    )(p, ys, x_all, slab, mod3, g_final.reshape(1, d))


def kernel(x, c, ctx, c_ctx, w_mod, b_mod, g_norm1, g_norm2, g_final, w_in, w_merge, w_branch, w_out, w_hconv,
           hy_bias, hf_w1, hf_b1, hf_w2, hf_b2, hf_w3, hf_freq, w_sconv, w_lconv, lru_wa, lru_ba, lru_wx, lru_bx,
           lru_lambda, w_rg, b_rg, w_re, b_re, w_e_gate, w_e_up, w_e_down):
    nb, seq, d = x.shape
    cl = ctx.shape[1]
    depth = w_in.shape[0]
    db = w_branch.shape[2]
    gdim = db // FNET_GROUPS
    ngroups = w_rg.shape[2]
    epg = w_re.shape[3]
    n_experts = ngroups * epg
    t_lat, t_ctx = nb * seq, nb * cl
    t_all = t_lat + t_ctx
    nlt, tpb = t_lat // TM, seq // TM
    nat = t_all // TM
    tl_lat, tl_ctx = min(TL, seq), min(TL, cl)

    x_all = jnp.concatenate([x.reshape(t_lat, d), ctx.reshape(t_ctx, d)], axis=0)
    nrows = -(-(nb + 1) // 8) * 8
    cvec = jnp.zeros((nrows, d), F32).at[:nb].set(c).at[nb].set(c_ctx)
    mods = _adaln(cvec, w_mod, b_mod)

    use_fft = nb % 2 == 0 and seq % (8 * FFT_N2) == 0
    if use_fft:
        hyfft = _fft_consts(seq)
    else:
        c_lat, sf_lat, si_lat = _dft_mats(seq)
        fwd_lat, inv_lat = (c_lat, sf_lat), (c_lat, si_lat)
    c_ctx, sf_ctx, si_ctx = _dft_mats(cl)
    fwd_ctx, inv_ctx = (c_ctx, sf_ctx), (c_ctx, si_ctx)
    fc_lat, fs_lat, w_pq = _fnet_mats(seq, gdim, FNET_GROUPS)
    fc_ctx, fs_ctx, _ = _fnet_mats(cl, gdim, FNET_GROUPS)

    out = None
    for l in range(depth):
        last = l == depth - 1
        mod3 = mods[l].reshape(nrows, 1, 6 * d)
        u_all, h_bf = _inproj(x_all, mod3, g_norm1[l], w_in[l].astype(BF16), nlt, tpb, nb)

        wa = [_block_diag(lru_wa[l, dd]) for dd in range(2)]
        wx = [_block_diag(lru_wx[l, dd]) for dd in range(2)]
        ba = [lru_ba[l, dd].reshape(1, db) for dd in range(2)]
        bx = [lru_bx[l, dd].reshape(1, db) for dd in range(2)]
        lam = [lru_lambda[l, dd].reshape(1, db) for dd in range(2)]
        zeros_st = jnp.zeros((nb, 1, db), F32)
        rows_out = t_lat if last else t_all

        def lru(row0, length, tl, rw, h0f, h0b, combine, out_init):
            hf, stf = _lru_pass(u_all, w_lconv[l], wa[0], wx[0], ba[0], bx[0], lam[0], h0f,
                                row0, nb, length, tl, rw, db, False)
            if combine:
                y, stb = _lru_pass(u_all, w_lconv[l], wa[1], wx[1], ba[1], bx[1], lam[1], h0b,
                                   row0, nb, length, tl, rw, db, True, hf=hf, out_init=out_init, t_all=rows_out)
            else:
                y, stb = _lru_pass(u_all, w_lconv[l], wa[1], wx[1], ba[1], bx[1], lam[1], h0b,
                                   row0, nb, length, tl, rw, db, True)
            return y, stf, stb

        bias0, bias1 = hy_bias[l, 0].reshape(1, db), hy_bias[l, 1].reshape(1, db)

        def hyena_dense(row0, length, fwd, inv, v, x1, x2, v_bf, out_init):
            filt = _hyena_filters(length, hf_w1[l], hf_b1[l], hf_w2[l], hf_b2[l], hf_w3[l], hf_freq[l], db)
            pqr = _filter_spectra(fwd, filt, length, db)
            yhat = _hy_fwd(fwd, v_bf, nb, length, db, pqr[0])
            z2 = _hy_inv(inv, yhat, v, 0, x1, 0, bias0, nb, length, db, 0, nb * length, None)
            yhat = _hy_fwd(fwd, z2, nb, length, db, pqr[1])
            return _hy_inv(inv, yhat, z2, 0, x2, 0, bias1, nb, length, db, row0, rows_out, out_init)

        def hyena_fft(length, v, x1, x2, v_bf, out_init):
            filt = _hyena_filters(length, hf_w1[l], hf_b1[l], hf_w2[l], hf_b2[l], hf_w3[l], hf_freq[l], db,
                                  reverse_bwd=True)
            kspec = _fft_filter_spectra(hyfft, filt, length, db)
            z2 = _fft_conv(hyfft, v_bf, kspec[0], v, x1, bias0, nb, db, None, nb * length)
            return _fft_conv(hyfft, z2, kspec[1], z2, x2, bias1, nb, db, out_init, rows_out)

        zbuf = lambda: None if last else jnp.zeros((rows_out, db), BF16)
        if last:
            _, st_f, st_b = lru(t_lat, cl, tl_ctx, cl, zeros_st, zeros_st, False, None)
        else:
            y_lru_c, st_f, st_b = lru(t_lat, cl, tl_ctx, cl, zeros_st, zeros_st, True, zbuf())
        y_lru, _, _ = lru(0, seq, tl_lat, GRID_W, st_f, st_b, True, None if last else y_lru_c)

        pq = _fnet_pq(u_all, w_pq, db)
        v, x1, x2, v_bf, y_sc = _conv_stage(u_all, w_hconv[l], w_sconv[l], 0, t_lat, tl_lat, GRID_W, db,
                                            jnp.zeros((rows_out, db), BF16))
        y_fn = _fnet_dft(fc_lat, fs_lat, pq, 0, nb, seq, db, gdim, zbuf(), rows_out)
        if not last:
            vc, x1c, x2c, vc_bf, y_sc = _conv_stage(u_all, w_hconv[l], w_sconv[l], t_lat, t_ctx, tl_ctx, cl, db,
                                                    y_sc)
            y_fn = _fnet_dft(fc_ctx, fs_ctx, pq, t_lat, nb, cl, db, gdim, y_fn, rows_out)
            y_hy = hyena_dense(t_lat, cl, fwd_ctx, inv_ctx, vc, x1c, x2c, vc_bf, zbuf())
        else:
            y_hy = None
        if use_fft:
            y_hy = hyena_fft(seq, v, x1, x2, v_bf, y_hy)
        else:
            y_hy = hyena_dense(0, seq, fwd_lat, inv_lat, v, x1, x2, v_bf, y_hy)

        ntiles = nlt if last else nat
        x_mid = _merge(h_bf, (y_fn, y_hy, y_sc, y_lru), w_merge[l].astype(BF16), w_branch[l].astype(BF16),
                       w_out[l].astype(BF16), x_all, mod3, ntiles, nlt, tpb, nb)

        w_router = jnp.zeros((d, 128), F32).at[:, :ngroups].set(w_rg[l])
        w_router = w_router.at[:, ngroups:ngroups + n_experts].set(
            jnp.transpose(w_re[l], (1, 0, 2)).reshape(d, n_experts))
        b_router = jnp.zeros((1, 128), F32).at[0, :ngroups].set(b_rg[l])
        b_router = b_router.at[0, ngroups:ngroups + n_experts].set(b_re[l].reshape(-1))
        h2, slab, tile_counts = _router(x_mid, mod3, g_norm2[l], w_router, b_router, ntiles, nlt, tpb, nb,
                                        ngroups, epg)
        slot, tile_expert, n_valid, n_slot_tiles = _route_tables(slab, tile_counts, ngroups, n_experts, TM_MOE)
        xs = _dispatch(h2, slot, n_slot_tiles * TM_MOE)
        ys = _moe_experts(xs, tile_expert, n_valid, _layer_bf16(w_e_gate, l), _layer_bf16(w_e_up, l),
                          _layer_bf16(w_e_down, l), TM_MOE)
        x_all = _combine(ys, slot, slab, x_mid, mod3, g_final, ntiles, nlt, tpb, nb, last)
        if last:
            out = x_all.reshape(nb, seq, d)
    return out
```

```python
import functools
import math

import jax
import jax.numpy as jnp
from jax import lax
from jax.experimental import pallas as pl
from jax.experimental.pallas import tpu as pltpu

F32 = jnp.float32
BF16 = jnp.bfloat16
HIGHEST = lax.Precision.HIGHEST

NORM_EPS = 1e-6
GRID_W = 64
FNET_GROUPS = 4
LRU_C = 8.0
HY_DECAY_TARGET = 1e-2
HY_FAST_PCT = 0.3
HY_SLOW_PCT = 1.5
TOP_K = 2

TM = 512
TL = 1024
TR = 512
TM_MOE = 512
TM_COMB = 512
TN_IN = 1536
TN_MERGE = 512
VMEM_LIMIT = 56 * 1024 * 1024


def _cp(*sem):
    return pltpu.CompilerParams(dimension_semantics=sem, vmem_limit_bytes=VMEM_LIMIT)


def _sigmoid(x):
    return 0.5 * jnp.tanh(0.5 * x) + 0.5


def _adaln_kernel(c_ref, w_ref, b_ref, o_ref):
    c = c_ref[...]
    s = c * _sigmoid(c)
    o_ref[...] = jnp.dot(s, w_ref[...], preferred_element_type=F32, precision=HIGHEST) + b_ref[...]


def _adaln(cvec, w_mod, b_mod):
    nl, d, n6 = w_mod.shape
    r = cvec.shape[0]
    tn = min(1024, n6)
    return pl.pallas_call(
        _adaln_kernel,
        grid=(nl, n6 // tn),
        in_specs=[pl.BlockSpec((r, d), lambda l, j: (0, 0)),
                  pl.BlockSpec((None, d, tn), lambda l, j: (l, 0, j)),
                  pl.BlockSpec((None, 1, tn), lambda l, j: (l, 0, j))],
        out_specs=pl.BlockSpec((None, r, tn), lambda l, j: (l, 0, j)),
        out_shape=jax.ShapeDtypeStruct((nl, r, n6), F32),
        compiler_params=_cp("parallel", "parallel"),
        name="adaln",
    )(cvec, w_mod, b_mod.reshape(nl, 1, n6))


def _norm_mod(x, g, shift, scale):
    y = x * lax.rsqrt(jnp.mean(x * x, axis=-1, keepdims=True) + NORM_EPS) * g
    return y * (1.0 + scale) + shift


def _norm_mod_store(dst, x_ref, g_ref, sh_ref, sc_ref, rows=128):
    rows = min(rows, dst.shape[0])

    def body(c, carry):
        sl = pl.ds(pl.multiple_of(c * rows, rows), rows)
        dst[sl, :] = _norm_mod(x_ref[sl, :], g_ref[...], sh_ref[...], sc_ref[...]).astype(dst.dtype)
        return carry

    lax.fori_loop(0, dst.shape[0] // rows, body, 0)


U32 = jnp.uint32


def _pack_rows(v):
    half = v.shape[1] // 2
    lo = pltpu.bitcast(v[:, :half].astype(BF16).astype(F32), U32) >> 16
    hi = pltpu.bitcast(v[:, half:].astype(BF16).astype(F32), U32) & jnp.uint32(0xFFFF0000)
    return hi | lo


def _unpack_rows(w):
    lo = pltpu.bitcast(w << 16, F32)
    hi = pltpu.bitcast(w & jnp.uint32(0xFFFF0000), F32)
    return lo, hi


def _mod_spec(d, chunk, nlt, tpb, nb):
    return pl.BlockSpec((None, 1, d), lambda i, *_: (jnp.where(i < nlt, i // tpb, nb), 0, chunk))


def _inproj_kernel(x_ref, sh_ref, sc_ref, g_ref, w_ref, u_ref, h_ref, *, tn):
    j = pl.program_id(1)

    @pl.when(j == 0)
    def _():
        _norm_mod_store(h_ref, x_ref, g_ref, sh_ref, sc_ref)

    for jc in range(w_ref.shape[1] // tn):
        @pl.when(j == jc)
        def _(jc=jc):
            u_ref[...] = jnp.dot(h_ref[...], w_ref[:, jc * tn:(jc + 1) * tn], preferred_element_type=F32)


def _inproj(x_all, mod3, g, w_bf, nlt, tpb, nb):
    t, d = x_all.shape
    n = w_bf.shape[1]
    tn = TN_IN if n % TN_IN == 0 else n
    return pl.pallas_call(
        functools.partial(_inproj_kernel, tn=tn),
        grid=(t // TM, n // tn),
        in_specs=[pl.BlockSpec((TM, d), lambda i, j: (i, 0)),
                  _mod_spec(d, 0, nlt, tpb, nb), _mod_spec(d, 1, nlt, tpb, nb),
                  pl.BlockSpec((1, d), lambda i, j: (0, 0)),
                  pl.BlockSpec((d, n), lambda i, j: (0, 0))],
        out_specs=[pl.BlockSpec((TM, tn), lambda i, j: (i, j)), pl.BlockSpec((TM, d), lambda i, j: (i, 0))],
        out_shape=[jax.ShapeDtypeStruct((t, n), F32), jax.ShapeDtypeStruct((t, d), BF16)],
        compiler_params=_cp("parallel", "arbitrary"),
        name="inproj",
    )(x_all, mod3, mod3, g.reshape(1, d), w_bf)


def _dwconv(x, w_ref, left, rw):
    tl = x.shape[0]
    pos = lax.broadcasted_iota(jnp.int32, x.shape, 0) % rw
    acc = None
    for k in range(w_ref.shape[0]):
        off = k - left
        if off == 0:
            term = x
        else:
            shifted = pltpu.roll(x, (-off) % tl, axis=0)
            ok = (pos + off >= 0) & (pos + off < rw)
            term = jnp.where(ok, shifted, 0.0)
        term = term * w_ref[k:k + 1, :]
        acc = term if acc is None else acc + term
    return acc


def _conv_kernel(hv_ref, h1_ref, h2_ref, sb_ref, sc_ref, sh_ref, wh_ref, ws_ref, ys_in,
                 v_ref, x1_ref, x2_ref, vbf_ref, ys_ref, *, rw, db):
    del ys_in
    for k, (ref, dst) in enumerate(((hv_ref, v_ref), (h1_ref, x1_ref), (h2_ref, x2_ref))):
        y = _dwconv(ref[...], wh_ref.at[:, k * db:(k + 1) * db], 1, rw)
        dst[...] = y
        if k == 0:
            vbf_ref[...] = y.astype(BF16)
    ys_ref[...] = (sb_ref[...] * _dwconv(sc_ref[...] * sh_ref[...], ws_ref, 1, rw)).astype(BF16)


def _conv_stage(u_all, w_hconv, w_sconv, row0, nrows, tl, rw, db, ys_init):
    ob = row0 // tl
    col = lambda c: pl.BlockSpec((tl, db), lambda i: (ob + i, c))
    own = pl.BlockSpec((tl, db), lambda i: (i, 0))
    return pl.pallas_call(
        functools.partial(_conv_kernel, rw=rw, db=db),
        grid=(nrows // tl,),
        in_specs=[col(1), col(2), col(3), col(4), col(5), col(6),
                  pl.BlockSpec(w_hconv.shape, lambda i: (0, 0)),
                  pl.BlockSpec(w_sconv.shape, lambda i: (0, 0)),
                  pl.BlockSpec(memory_space=pl.ANY)],
        out_specs=[own, own, own, own, pl.BlockSpec((tl, db), lambda i: (ob + i, 0))],
        out_shape=[jax.ShapeDtypeStruct((nrows, db), F32)] * 3
                  + [jax.ShapeDtypeStruct((nrows, db), BF16), jax.ShapeDtypeStruct(ys_init.shape, BF16)],
        input_output_aliases={8: 4},
        compiler_params=_cp("parallel"),
        name="conv",
    )(*([u_all] * 6), w_hconv, w_sconv, ys_init)


def _trig_outer(l, period):
    q = 1 << ((l.bit_length() - 1) // 2)
    n = lax.broadcasted_iota(jnp.int32, (1, l), 1)
    scale = 2.0 * math.pi / period

    def table(rows, step):
        r = lax.broadcasted_iota(jnp.int32, (rows, 1), 0) * step
        ang = ((r * n) % period).astype(F32) * scale
        return jnp.cos(ang), jnp.sin(ang)

    ac, as_ = table(l // q, q)
    bc, bs = table(q, 1)
    c = ac[:, None, :] * bc[None] - as_[:, None, :] * bs[None]
    s = as_[:, None, :] * bc[None] + ac[:, None, :] * bs[None]
    return c.reshape(l, l), s.reshape(l, l)


def _dft_mats(l):
    c, s = _trig_outer(l, 2 * l)
    k = lax.broadcasted_iota(jnp.int32, (l, l), 0)
    n = lax.broadcasted_iota(jnp.int32, (l, l), 1)
    alt_n = jnp.where(n % 2 == 0, 1.0, -1.0).astype(F32)
    alt_k = jnp.where(k % 2 == 0, 1.0, -1.0).astype(F32)
    return c.astype(BF16), jnp.where(k == 0, alt_n, -s).astype(BF16), jnp.where(n == 0, alt_k, -s).astype(BF16)


def _fnet_mats(l, gdim, groups):
    c, s = _trig_outer(l, l)
    cg, sg = _trig_outer(gdim, gdim)
    eye = jnp.eye(groups, dtype=F32)
    w_pq = jnp.concatenate([jnp.kron(eye, cg), jnp.kron(eye, sg)], axis=1).astype(BF16)
    return c.astype(BF16), (-s).astype(BF16), w_pq


def _fnet_pq_kernel(u_ref, w_ref, o_ref):
    o_ref[...] = jnp.dot(u_ref[...].astype(BF16), w_ref[...], preferred_element_type=F32).astype(BF16)


def _fnet_pq(u_all, w_pq, db):
    t = u_all.shape[0]
    return pl.pallas_call(
        _fnet_pq_kernel,
        grid=(t // TM,),
        in_specs=[pl.BlockSpec((TM, db), lambda i: (i, 0)),
                  pl.BlockSpec(w_pq.shape, lambda i: (0, 0))],
        out_specs=pl.BlockSpec((TM, 2 * db), lambda i: (i, 0)),
        out_shape=jax.ShapeDtypeStruct((t, 2 * db), BF16),
        compiler_params=_cp("parallel"),
        name="fnet_pq",
    )(u_all, w_pq)


def _fnet_dft_kernel(d0_ref, d1_ref, x_ref, *rest, db, scale):
    o_ref = rest[-1]
    acc = jnp.dot(d0_ref[...], x_ref[:, :db], preferred_element_type=F32)
    acc = acc + jnp.dot(d1_ref[...], x_ref[:, db:], preferred_element_type=F32)
    o_ref[...] = (acc * scale).astype(BF16)


def _mat_spec(tr, l):
    return pl.BlockSpec((tr, l), lambda i, b: (i, 0))


def _fnet_dft(d0, d1, pq, row0, nb, l, db, gdim, out_init, t_all):
    tr = min(TR, l)
    nrt = l // tr
    args = [d0, d1, pq]
    in_specs = [_mat_spec(tr, l), _mat_spec(tr, l),
                pl.BlockSpec((l, 2 * db), lambda i, b: (row0 // l + b, 0))]
    aliases = {}
    if out_init is not None:
        args.append(out_init)
        in_specs.append(pl.BlockSpec(memory_space=pl.ANY))
        aliases = {3: 0}
    return pl.pallas_call(
        functools.partial(_fnet_dft_kernel, db=db, scale=1.0 / math.sqrt(l * gdim)),
        grid=(nrt, nb),
        in_specs=in_specs,
        out_specs=pl.BlockSpec((tr, db), lambda i, b: (row0 // tr + b * nrt + i, 0)),
        out_shape=jax.ShapeDtypeStruct((t_all, db), BF16),
        input_output_aliases=aliases,
        compiler_params=_cp("parallel", "parallel"),
        name="fnet_dft",
    )(*args)


def _hy_fwd_kernel(f0_ref, f1_ref, x_ref, *rest, mult):
    o_ref = rest[-1]
    zre = jnp.dot(f0_ref[...], x_ref[...], preferred_element_type=F32)
    zim = jnp.dot(f1_ref[...], x_ref[...], preferred_element_type=F32)
    if mult:
        k_ref = rest[0]
        p, q, r = k_ref[0], k_ref[1], k_ref[2]
        o_ref[0] = (zre * p - zim * q).astype(o_ref.dtype)
        o_ref[1] = (zre * q + zim * r).astype(o_ref.dtype)
    else:
        o_ref[0] = zre
        o_ref[1] = zim


def _hy_fwd(fwd, x, nb, l, db, pqr):
    tr = min(TR, l)
    args = [fwd[0], fwd[1], x]
    in_specs = [_mat_spec(tr, l), _mat_spec(tr, l),
                pl.BlockSpec((l, db), lambda i, b: (b, 0))]
    if pqr is not None:
        args.append(pqr)
        in_specs.append(pl.BlockSpec((3, tr, db), lambda i, b: (0, i, 0)))
    return pl.pallas_call(
        functools.partial(_hy_fwd_kernel, mult=pqr is not None),
        grid=(l // tr, nb),
        in_specs=in_specs,
        out_specs=pl.BlockSpec((None, 2, tr, db), lambda i, b: (b, 0, i, 0)),
        out_shape=jax.ShapeDtypeStruct((nb, 2, l, db), F32 if pqr is None else BF16),
        compiler_params=_cp("parallel", "parallel"),
        name="hy_fwd",
    )(*args)


def _hy_inv_kernel(g0_ref, g1_ref, y_ref, zp_ref, gate_ref, bias_ref, *rest):
    o_ref = rest[-1]
    y = jnp.dot(g0_ref[...], y_ref[0], preferred_element_type=F32)
    y = y + jnp.dot(g1_ref[...], y_ref[1], preferred_element_type=F32)
    o_ref[...] = (gate_ref[...] * (y + bias_ref[...] * zp_ref[...].astype(F32))).astype(BF16)


def _hy_inv(inv, yhat, zprev, zcol, gate, gcol, bias, nb, l, db, out_row0, out_rows, out_init):
    tr = min(TR, l)
    nrt = l // tr
    args = [inv[0], inv[1], yhat, zprev, gate, bias]
    in_specs = [_mat_spec(tr, l), _mat_spec(tr, l),
                pl.BlockSpec((None, 2, l, db), lambda i, b: (b, 0, 0, 0)),
                pl.BlockSpec((tr, db), lambda i, b: (b * nrt + i, zcol)),
                pl.BlockSpec((tr, db), lambda i, b: (b * nrt + i, gcol)),
                pl.BlockSpec((1, db), lambda i, b: (0, 0))]
    aliases = {}
    if out_init is not None:
        args.append(out_init)
        in_specs.append(pl.BlockSpec(memory_space=pl.ANY))
        aliases = {6: 0}
    return pl.pallas_call(
        _hy_inv_kernel,
        grid=(nrt, nb),
        in_specs=in_specs,
        out_specs=pl.BlockSpec((tr, db), lambda i, b: (out_row0 // tr + b * nrt + i, 0)),
        out_shape=jax.ShapeDtypeStruct((out_rows, db), BF16),
        input_output_aliases=aliases,
        compiler_params=_cp("parallel", "parallel"),
        name="hy_inv",
    )(*args)


def _hyena_filters(length, hf_w1, hf_b1, hf_w2, hf_b2, hf_w3, hf_freq, db, circular=False):
    emb = hf_w1.shape[0]
    nbands = (emb - 1) // 2
    bands = jnp.linspace(1e-4, nbands - 1, nbands, dtype=F32)[None, :]
    min_decay = math.log(HY_DECAY_TARGET) / HY_SLOW_PCT
    max_decay = math.log(HY_DECAY_TARGET) / HY_FAST_PCT
    deltas = jnp.abs(jnp.linspace(min_decay, max_decay, db, dtype=F32))
    w3 = hf_w3.reshape(hf_w3.shape[0], -1, 2, db)

    def direction(pos, d, order_major):
        t = (pos * (1.0 / (length - 1)))[:, None]
        w = (2.0 * math.pi / length) * pos[:, None]
        feats = jnp.concatenate([t, jnp.cos(bands * w), -jnp.sin(bands * w)], axis=-1)
        z = jnp.sin(hf_freq[0] * (jnp.dot(feats, hf_w1, precision=HIGHEST) + hf_b1))
        z = jnp.sin(hf_freq[1] * (jnp.dot(z, hf_w2, precision=HIGHEST) + hf_b2))
        decay = jnp.exp(-t * deltas)
        if order_major:
            return jnp.einsum("lf,fod->old", z, w3[:, :, d, :], precision=HIGHEST) * decay[None]
        return jnp.einsum("lf,fod->lod", z, w3[:, :, d, :], precision=HIGHEST) * decay[:, None, :]

    pos = jnp.arange(length, dtype=F32)
    if circular:
        first = (pos == 0)[None, :, None]
        f0 = direction(pos, 0, True)
        f1 = direction(jnp.where(pos == 0, 0.0, length - pos), 1, True)
        scale = lax.rsqrt(jnp.sum(f0 * f0, axis=1, keepdims=True) + jnp.sum(f1 * f1, axis=1, keepdims=True)
                          + NORM_EPS)
        return jnp.stack([f0 * scale, jnp.where(first, 0.0, f1) * scale], axis=1)
    filt = jnp.stack([direction(pos, 0, False), direction(pos, 1, False)], axis=2)
    return filt * lax.rsqrt(jnp.sum(filt * filt, axis=(0, 2), keepdims=True) + NORM_EPS)


def _filter_spectra(fwd, filt, l, db):
    n_ord = filt.shape[1]
    cols = jnp.transpose(filt, (1, 2, 0, 3))
    cols = cols.at[:, 1, 0, :].set(0.0)
    x = cols.reshape(n_ord * 2 * l, db).astype(BF16)
    spec = _hy_fwd(fwd, x, n_ord * 2, l, db, None).reshape(n_ord, 2, 2, l, db)
    hf, hb = spec[:, 0], spec[:, 1]
    k_re = hf[:, 0] + hb[:, 0]
    k_im = hf[:, 1] - hb[:, 1]
    k_nyq = hf[:, 1, 0] + hb[:, 1, 0]
    first = (jnp.arange(l) == 0)[None, :, None]
    scale = jnp.where(first, 1.0 / (2 * l), 2.0 / (2 * l)).astype(F32)
    p = k_re * scale
    q = jnp.where(first, 0.0, k_im) * scale
    r = jnp.where(first, k_nyq[:, None, :], k_re) * scale
    return jnp.stack([p, q, r], axis=1)


FFT_N2 = 128
FFT_NT = 16
FFT_KC = 8


def _fft_consts(l):
    n, n2 = 2 * l, FFT_N2
    n1 = n // n2
    h = n1 // 2
    cat = jnp.concatenate
    ia = jnp.arange(n1, dtype=jnp.int32)
    ang = ((ia[:, None] * ia[None, :]) % n1).astype(F32) * (2.0 * math.pi / n1)
    fr, fi = jnp.cos(ang), -jnp.sin(ang)
    mat_data = cat([cat([fr[:, :h], -fi[:, :h]], 1), cat([fi[:, :h], fr[:, :h]], 1)], 0)
    mat_filt = cat([fr, fi], 0)
    gr, gi = fr[:, :h].T, -fi[:, :h].T
    mat_inv = jnp.stack([cat([gr, -gi], 1), cat([gi, gr], 1)])
    k = ia[:, None, None] + n1 * jnp.arange(n2, dtype=jnp.int32)[None, :, None]
    nn = jnp.arange(n2, dtype=jnp.int32)[None, None, :]
    angb = ((k * nn) % n).astype(F32) * (2.0 * math.pi / n)
    er, ei = jnp.cos(angb), -jnp.sin(angb)
    mb = cat([cat([er, -ei], 2), cat([ei, er], 2)], 1)
    eye = jnp.eye(FFT_NT, dtype=F32)
    kron = lambda a: jnp.kron(a, eye).astype(BF16)
    return dict(n1=n1, h=h, mat_data=kron(mat_data), mat_filt=kron(mat_filt * (1.0 / n)),
                mat_inv=jnp.stack([kron(mat_inv[0]), kron(mat_inv[1])]),
                mb=mb.astype(BF16), mib=jnp.transpose(mb, (0, 2, 1)).astype(BF16))


def _fft_a_kernel(m_ref, a_ref, b_ref, o_ref):
    h, nt, db = a_ref.shape
    x = jnp.concatenate([a_ref[...].reshape(h * nt, db), b_ref[...].reshape(h * nt, db)], axis=0)
    res = jnp.dot(m_ref[...], x, preferred_element_type=F32)
    o_ref[...] = res.astype(o_ref.dtype).reshape(o_ref.shape)


def _fft_stage_a(mat, x4, npairs, stride, offset):
    _, h, n2, db = x4.shape
    nt = FFT_NT
    rows = mat.shape[0] // nt
    return pl.pallas_call(
        _fft_a_kernel,
        grid=(npairs, n2 // nt),
        in_specs=[pl.BlockSpec(mat.shape, lambda p, j: (0, 0)),
                  pl.BlockSpec((None, h, nt, db), lambda p, j: (p * stride, 0, j, 0)),
                  pl.BlockSpec((None, h, nt, db), lambda p, j: (p * stride + offset, 0, j, 0))],
        out_specs=pl.BlockSpec((None, rows, nt, db), lambda p, j: (p, 0, j, 0)),
        out_shape=jax.ShapeDtypeStruct((npairs, rows, n2, db), BF16),
        compiler_params=_cp("parallel", "parallel"),
        name="fft_a",
    )(mat, x4, x4)


def _fft_b_kernel(m_ref, ar_ref, ai_ref, o_ref):
    for kk in range(m_ref.shape[0]):
        a = jnp.concatenate([ar_ref[kk], ai_ref[kk]], axis=0)
        o_ref[kk] = jnp.dot(m_ref[kk], a, preferred_element_type=F32)


def _fft_bb_kernel(m_ref, mi_ref, ar_ref, ai_ref, k_ref, o_ref):
    n2 = ar_ref.shape[1]
    for kk in range(m_ref.shape[0]):
        a = jnp.concatenate([ar_ref[kk], ai_ref[kk]], axis=0)
        x = jnp.dot(m_ref[kk], a, preferred_element_type=F32)
        xr, xi = x[:n2], x[n2:]
        kr, ki = k_ref[kk, :n2], k_ref[kk, n2:]
        y = jnp.concatenate([(xr * kr - xi * ki).astype(BF16), (xr * ki + xi * kr).astype(BF16)], axis=0)
        b = jnp.dot(mi_ref[kk], y, preferred_element_type=F32)
        o_ref[0, kk] = b[:n2].astype(o_ref.dtype)
        o_ref[1, kk] = b[n2:].astype(o_ref.dtype)


def _fft_stage_b(fc, a, kspec, db):
    mb, mib = fc["mb"], fc["mib"]
    npairs = a.shape[0]
    n1, r2, _ = mb.shape
    n2 = r2 // 2
    kc = min(FFT_KC, n1)
    a5 = a.reshape(npairs, 2, n1, n2, db)
    mspec = pl.BlockSpec((kc, r2, r2), lambda c, p: (c, 0, 0))
    aspec = lambda part: pl.BlockSpec((None, None, kc, n2, db), lambda c, p: (p, part, c, 0, 0))
    if kspec is None:
        kern, args, in_specs = _fft_b_kernel, [mb, a5, a5], [mspec, aspec(0), aspec(1)]
        out_spec = pl.BlockSpec((None, kc, r2, db), lambda c, p: (p, c, 0, 0))
        out_shape = jax.ShapeDtypeStruct((npairs, n1, r2, db), F32)
    else:
        kern, args = _fft_bb_kernel, [mb, mib, a5, a5, kspec]
        in_specs = [mspec, mspec, aspec(0), aspec(1), pl.BlockSpec((kc, r2, db), lambda c, p: (c, 0, 0))]
        out_spec = pl.BlockSpec((None, 2, kc, n2, db), lambda c, p: (p, 0, c, 0, 0))
        out_shape = jax.ShapeDtypeStruct((npairs, 2, n1, n2, db), BF16)
    return pl.pallas_call(
        kern,
        grid=(n1 // kc, npairs),
        in_specs=in_specs,
        out_specs=out_spec,
        out_shape=out_shape,
        compiler_params=_cp("parallel", "parallel"),
        name="fft_b",
    )(*args)


def _ifft_a_kernel(m_ref, b_ref, zp_ref, gate_ref, bias_ref, *rest):
    o_ref = rest[-1]
    h, nt, db = o_ref.shape
    bm = b_ref[...].reshape(b_ref.shape[0] * nt, db)
    y = jnp.dot(m_ref[...], bm, preferred_element_type=F32)
    zp = zp_ref[...].reshape(h * nt, db).astype(F32)
    out = gate_ref[...].reshape(h * nt, db) * (y + bias_ref[...] * zp)
    o_ref[...] = out.astype(o_ref.dtype).reshape(o_ref.shape)


def _ifft_stage_a(mat_inv, b5, zprev, gate, bias, nb, db, out_init, out_rows):
    npairs, _, n1, n2, _ = b5.shape
    nt = FFT_NT
    h = mat_inv.shape[1] // nt
    b4 = b5.reshape(npairs, 2 * n1, n2, db)
    view = lambda a: a.reshape(a.shape[0] // n2, n2, db)
    blk = pl.BlockSpec((h, nt, db), lambda b, j: (b, j, 0))
    args = [mat_inv, b4, view(zprev), view(gate), bias]
    in_specs = [pl.BlockSpec((None,) + mat_inv.shape[1:], lambda b, j: (b // npairs, 0, 0)),
                pl.BlockSpec((None, 2 * n1, nt, db), lambda b, j: (b % npairs, 0, j, 0)),
                blk, blk, pl.BlockSpec((1, db), lambda b, j: (0, 0))]
    aliases = {}
    if out_init is not None:
        args.append(view(out_init))
        in_specs.append(pl.BlockSpec(memory_space=pl.ANY))
        aliases = {5: 0}
    out = pl.pallas_call(
        _ifft_a_kernel,
        grid=(nb, n2 // nt),
        in_specs=in_specs,
        out_specs=blk,
        out_shape=jax.ShapeDtypeStruct((out_rows // n2, n2, db), BF16),
        input_output_aliases=aliases,
        compiler_params=_cp("parallel", "parallel"),
        name="ifft_a",
    )(*args)
    return out.reshape(out_rows, db)


def _fft_filter_spectra(fc, filt, l, db):
    n_ord = filt.shape[0]
    x4 = filt.astype(BF16).reshape(n_ord * 2, fc["h"], FFT_N2, db)
    a = _fft_stage_a(fc["mat_filt"], x4, n_ord, 2, 1)
    return _fft_stage_b(fc, a, None, db)


def _fft_conv(fc, z_bf, kspec, zprev, gate, bias, nb, db, out_init, out_rows):
    x4 = z_bf.reshape(nb, fc["h"], FFT_N2, db)
    a = _fft_stage_a(fc["mat_data"], x4, nb // 2, 1, nb // 2)
    b5 = _fft_stage_b(fc, a, kspec, db)
    return _ifft_stage_a(fc["mat_inv"], b5, zprev, gate, bias, nb, db, out_init, out_rows)


def _lru_kernel(*refs, rw, reverse, final):
    if final:
        (x_ref, w_ref, wa_ref, wx_ref, ba_ref, bx_ref, lam_ref, h0_ref, hf_ref, ug_ref,
         _, o_ref, st_ref, a_s, b_s, carry) = refs
    else:
        (x_ref, w_ref, wa_ref, wx_ref, ba_ref, bx_ref, lam_ref, h0_ref,
         o_ref, st_ref, a_s, b_s, carry) = refs
    tl = x_ref.shape[0]

    @pl.when(pl.program_id(1) == 0)
    def _():
        carry[...] = h0_ref[...]

    xc = _dwconv(x_ref[...], w_ref, w_ref.shape[0] // 2, rw)
    xb = xc.astype(BF16)
    r = _sigmoid(jnp.dot(xb, wa_ref[...], preferred_element_type=F32) + ba_ref[...])
    ig = _sigmoid(jnp.dot(xb, wx_ref[...], preferred_element_type=F32) + bx_ref[...])
    lam = lam_ref[...]
    softplus = jnp.maximum(-lam, 0.0) + jnp.log(1.0 + jnp.exp(-jnp.abs(lam)))
    log_a = -LRU_C * r * softplus
    a = jnp.exp(log_a)
    bv = jnp.sqrt(1.0 - jnp.exp(2.0 * log_a)) * (ig * xc)

    pos8 = lax.broadcasted_iota(jnp.int32, a.shape, 0) % 8
    for s in (1, 2, 4):
        if reverse:
            a_sh, b_sh, ok = pltpu.roll(a, tl - s, axis=0), pltpu.roll(bv, tl - s, axis=0), pos8 < 8 - s
        else:
            a_sh, b_sh, ok = pltpu.roll(a, s, axis=0), pltpu.roll(bv, s, axis=0), pos8 >= s
        bv = jnp.where(ok, a * b_sh + bv, bv)
        a = jnp.where(ok, a * a_sh, a)
    a_s[...] = a
    b_s[...] = bv

    ng = tl // 8

    def body(gi, c):
        g = ng - 1 - gi if reverse else gi
        sl = pl.ds(pl.multiple_of(g * 8, 8), 8)
        h8 = b_s[sl, :] + a_s[sl, :] * c
        b_s[sl, :] = h8
        return h8[0:1, :] if reverse else h8[7:8, :]

    c_out = lax.fori_loop(0, ng, body, carry[...])
    carry[...] = c_out
    st_ref[...] = c_out
    if final:
        ug = ug_ref[...]
        gelu = 0.5 * ug * (1.0 + jnp.tanh(math.sqrt(2.0 / math.pi) * (ug + 0.044715 * ug * ug * ug)))
        o_ref[...] = ((hf_ref[...] + b_s[...]) * gelu).astype(BF16)
    else:
        o_ref[...] = b_s[...]


def _lru_pass(u_all, w_lconv, wa, wx, ba, bx, lam, h0, row0, nb, l, tl, rw, db, reverse, hf=None, out_init=None,
              t_all=None):
    final = hf is not None
    nc = l // tl
    ob = row0 // tl
    cidx = (lambda c: nc - 1 - c) if reverse else (lambda c: c)
    full = lambda a: pl.BlockSpec(a.shape, lambda b, c: (0,) * a.ndim)
    ncolx, ncolg = 7, 8
    args = [u_all, w_lconv, wa, wx, ba, bx, lam, h0]
    in_specs = [pl.BlockSpec((tl, db), lambda b, c: (ob + b * nc + cidx(c), ncolx)),
                full(w_lconv), full(wa), full(wx), full(ba), full(bx), full(lam),
                pl.BlockSpec((None, 1, db), lambda b, c: (b, 0, 0))]
    aliases = {}
    if final:
        args += [hf, u_all]
        in_specs += [pl.BlockSpec((tl, db), lambda b, c: (b * nc + cidx(c), 0)),
                     pl.BlockSpec((tl, db), lambda b, c: (ob + b * nc + cidx(c), ncolg))]
        if out_init is None:
            out_init = jnp.zeros((8, 128), BF16)
        else:
            aliases = {10: 0}
        args.append(out_init)
        in_specs.append(pl.BlockSpec(memory_space=pl.ANY))
        out_spec = pl.BlockSpec((tl, db), lambda b, c: (ob + b * nc + cidx(c), 0))
        out_shape = jax.ShapeDtypeStruct((t_all, db), BF16)
    else:
        out_spec = pl.BlockSpec((tl, db), lambda b, c: (b * nc + cidx(c), 0))
        out_shape = jax.ShapeDtypeStruct((nb * l, db), F32)
    return pl.pallas_call(
        functools.partial(_lru_kernel, rw=rw, reverse=reverse, final=final),
        grid=(nb, nc),
        in_specs=in_specs,
        out_specs=[out_spec, pl.BlockSpec((None, 1, db), lambda b, c: (b, 0, 0))],
        out_shape=[out_shape, jax.ShapeDtypeStruct((nb, 1, db), F32)],
        scratch_shapes=[pltpu.VMEM((tl, db), F32), pltpu.VMEM((tl, db), F32), pltpu.VMEM((1, db), F32)],
        input_output_aliases=aliases,
        compiler_params=_cp("arbitrary", "arbitrary"),
        name="lru_bwd" if reverse else "lru_fwd",
    )(*args)


def _block_diag(w):
    h, hd, _ = w.shape
    eye = jnp.eye(h, dtype=w.dtype)
    return (eye[:, None, :, None] * w[:, :, None, :]).reshape(h * hd, h * hd).astype(BF16)


def _merge_kernel(x_ref, h_ref, b0_ref, b1_ref, b2_ref, b3_ref, wm_ref, wb_ref, wo_ref, ga_ref, o_ref):
    j = pl.program_id(1)
    h = h_ref[...]
    m = None
    for k, br in enumerate((b0_ref, b1_ref, b2_ref, b3_ref)):
        g = jnp.dot(h, wm_ref[k], preferred_element_type=F32)
        p = jnp.dot(br[...], wb_ref[k], preferred_element_type=F32)
        term = _sigmoid(g) * p
        m = term if m is None else m + term
    part = jnp.dot(m.astype(BF16), wo_ref[...], preferred_element_type=F32)

    @pl.when(j == 0)
    def _():
        o_ref[...] = part

    @pl.when(j > 0)
    def _():
        o_ref[...] += part

    @pl.when(j == pl.num_programs(1) - 1)
    def _():
        o_ref[...] = x_ref[...] + ga_ref[...] * o_ref[...]


def _merge(h_bf, branches, wm_bf, wb_bf, wo_bf, x_all, mod3, ntiles, nlt, tpb, nb):
    t, d = x_all.shape
    db = wb_bf.shape[1]
    tn = min(TN_MERGE, d)
    row = lambda w: pl.BlockSpec((TM, w), lambda i, j: (i, 0))
    return pl.pallas_call(
        _merge_kernel,
        grid=(ntiles, d // tn),
        in_specs=[row(d), row(d), row(db), row(db), row(db), row(db),
                  pl.BlockSpec((4, d, tn), lambda i, j: (0, 0, j)),
                  pl.BlockSpec((4, db, tn), lambda i, j: (0, 0, j)),
                  pl.BlockSpec((tn, d), lambda i, j: (j, 0)),
                  _mod_spec(d, 2, nlt, tpb, nb)],
        out_specs=row(d),
        out_shape=jax.ShapeDtypeStruct((ntiles * TM, d), F32),
        compiler_params=_cp("parallel", "arbitrary"),
        name="merge",
    )(x_all, h_bf, *branches, wm_bf, wb_bf, wo_bf, mod3)


def _router_kernel(x_ref, sh_ref, sc_ref, g_ref, whi_ref, wlo_ref, br_ref, tri_ref, h_ref, r_ref, c_ref, *,
                   ngroups, epg):
    h = _norm_mod(x_ref[...], g_ref[...], sh_ref[...], sc_ref[...])
    h_ref[...] = _pack_rows(h)
    h_hi = h.astype(BF16)
    h_lo = (h - h_hi.astype(F32)).astype(BF16)
    logits = jnp.dot(h_hi, wlo_ref[...], preferred_element_type=F32)
    logits = logits + jnp.dot(h_lo, whi_ref[...], preferred_element_type=F32)
    logits = logits + jnp.dot(h_hi, whi_ref[...], preferred_element_type=F32) + br_ref[...]
    lane = lax.broadcasted_iota(jnp.int32, logits.shape, 1)
    lane_f = lane.astype(F32)
    neg = jnp.float32(-1e30)
    big = jnp.float32(1e6)
    is_g = lane < ngroups
    lg = jnp.where(is_g, logits, neg)
    mx = jnp.max(lg, axis=-1, keepdims=True)
    g_top = jnp.min(jnp.where(lg == mx, lane_f, big), axis=-1, keepdims=True)
    den = jnp.sum(jnp.where(is_g, jnp.exp(lg - mx), 0.0), axis=-1, keepdims=True)
    p_top = 1.0 / den
    el = lane_f - ngroups
    in_grp = (el >= g_top * epg) & (el < (g_top + 1.0) * epg)
    le = jnp.where(in_grp, logits, neg)
    v1 = jnp.max(le, axis=-1, keepdims=True)
    e1 = jnp.min(jnp.where(le == v1, el, big), axis=-1, keepdims=True)
    le2 = jnp.where(el == e1, neg, le)
    v2 = jnp.max(le2, axis=-1, keepdims=True)
    e2 = jnp.min(jnp.where(le2 == v2, el, big), axis=-1, keepdims=True)
    dlt = jnp.exp(v2 - v1)
    w1 = p_top / (1.0 + dlt)
    w2 = p_top * dlt / (1.0 + dlt)
    pick1 = el == e1
    pick2 = el == e2
    onehot = jnp.where(pick1 | pick2, 1.0, 0.0)
    before = jnp.dot(tri_ref[...], onehot.astype(BF16), preferred_element_type=F32)
    rank1 = jnp.sum(jnp.where(pick1, before, 0.0), axis=-1, keepdims=True)
    rank2 = jnp.sum(jnp.where(pick2, before, 0.0), axis=-1, keepdims=True)
    out = jnp.where(lane == 0, e1, 0.0)
    out = jnp.where(lane == 1, e2, out)
    out = jnp.where(lane == 2, w1, out)
    out = jnp.where(lane == 3, w2, out)
    out = jnp.where(lane == 4, rank1, out)
    out = jnp.where(lane == 5, rank2, out)
    r_ref[...] = out
    c_ref[...] = jnp.broadcast_to(jnp.sum(onehot, axis=0, keepdims=True), c_ref.shape)


def _router(x_all, mod3, g, w_router, b_router, ntiles, nlt, tpb, nb, ngroups, epg):
    t, d = x_all.shape
    w_hi = w_router.astype(BF16)
    w_hi_lo = (w_hi, (w_router - w_hi.astype(F32)).astype(BF16))
    earlier = jnp.tril(jnp.ones((TM, TM), F32), -1).astype(BF16)
    return pl.pallas_call(
        functools.partial(_router_kernel, ngroups=ngroups, epg=epg),
        grid=(ntiles,),
        in_specs=[pl.BlockSpec((TM, d), lambda i: (i, 0)),
                  _mod_spec(d, 3, nlt, tpb, nb), _mod_spec(d, 4, nlt, tpb, nb),
                  pl.BlockSpec((1, d), lambda i: (0, 0)),
                  pl.BlockSpec((d, 128), lambda i: (0, 0)),
                  pl.BlockSpec((d, 128), lambda i: (0, 0)),
                  pl.BlockSpec((1, 128), lambda i: (0, 0)),
                  pl.BlockSpec((TM, TM), lambda i: (0, 0))],
        out_specs=[pl.BlockSpec((TM, d // 2), lambda i: (i, 0)), pl.BlockSpec((TM, 128), lambda i: (i, 0)),
                   pl.BlockSpec((None, 8, 128), lambda i: (i, 0, 0))],
        out_shape=[jax.ShapeDtypeStruct((ntiles * TM, d // 2), U32),
                   jax.ShapeDtypeStruct((ntiles * TM, 128), F32),
                   jax.ShapeDtypeStruct((ntiles, 8, 128), F32)],
        compiler_params=_cp("parallel"),
        name="router",
    )(x_all, mod3, mod3, g.reshape(1, d), w_hi, w_hi_lo[1], b_router, earlier)


def _route_tables(slab, tile_counts, ngroups, n_experts, tm):
    ntiles = tile_counts.shape[0]
    e = slab[:, 0:TOP_K].astype(jnp.int32).reshape(ntiles, TM, TOP_K)
    rank = slab[:, 4:4 + TOP_K].astype(jnp.int32).reshape(ntiles, TM, TOP_K)
    counts_t = tile_counts[:, 0, ngroups:ngroups + n_experts].astype(jnp.int32)
    base = jnp.cumsum(counts_t, axis=0) - counts_t
    counts = jnp.sum(counts_t, axis=0)
    tiles_per_e = (counts + tm - 1) // tm
    tile_end = jnp.cumsum(tiles_per_e)
    starts = (tile_end - tiles_per_e) * tm
    off = starts[None, :] + base
    sel = e[..., None] == jnp.arange(n_experts, dtype=jnp.int32)
    slot = jnp.sum(jnp.where(sel, off[:, None, None, :], 0), axis=-1) + rank
    n_slot_tiles = (ntiles * TM * TOP_K) // tm + n_experts
    n_valid = tile_end[-1]
    tile_ids = jnp.minimum(jnp.arange(n_slot_tiles, dtype=jnp.int32), n_valid - 1)
    tile_expert = jnp.sum((tile_end[None, :] <= tile_ids[:, None]).astype(jnp.int32), axis=1)
    return slot, tile_expert, n_valid.reshape(1).astype(jnp.int32), n_slot_tiles


ROW_DMA_UNROLL = 8


def _stage_indices(idx_hbm_row, idx_smem, sem_i):
    cp = pltpu.make_async_copy(idx_hbm_row, idx_smem, sem_i)
    cp.start()
    cp.wait()


def _dispatch_kernel(p_hbm, h_ref, xs_in, xs_hbm, idx, sem_i, sem_x):
    del xs_in
    i = pl.program_id(0)
    tm = h_ref.shape[0]
    _stage_indices(p_hbm.at[i], idx, sem_i)

    def issue(r, c):
        src = h_ref.at[pl.ds(r, 1), :]
        for k in range(TOP_K):
            pltpu.make_async_copy(src, xs_hbm.at[pl.ds(idx[k * tm + r], 1), :], sem_x).start(priority=k % 2)
        return c

    lax.fori_loop(0, tm, issue, 0, unroll=ROW_DMA_UNROLL)
    for k in range(TOP_K):
        pltpu.make_async_copy(h_ref, xs_hbm.at[pl.ds(0, tm), :], sem_x).wait()


def _dispatch(h_all, slot, n_slot_rows):
    ntiles = slot.shape[0]
    d = h_all.shape[1]
    p = jnp.transpose(slot, (0, 2, 1)).reshape(ntiles, TOP_K * TM)
    xs0 = jnp.zeros((n_slot_rows, d), h_all.dtype)
    return pl.pallas_call(
        _dispatch_kernel,
        grid=(ntiles,),
        in_specs=[pl.BlockSpec(memory_space=pl.ANY), pl.BlockSpec((TM, d), lambda i: (i, 0)),
                  pl.BlockSpec(memory_space=pl.ANY)],
        out_specs=pl.BlockSpec(memory_space=pl.ANY),
        out_shape=jax.ShapeDtypeStruct((n_slot_rows, d), h_all.dtype),
        scratch_shapes=[pltpu.SMEM((TOP_K * TM,), jnp.int32),
                        pltpu.SemaphoreType.DMA(()), pltpu.SemaphoreType.DMA(())],
        input_output_aliases={2: 0},
        compiler_params=_cp("arbitrary"),
        name="dispatch",
    )(p, h_all, xs0)


def _cast_kernel(w_ref, o_ref):
    o_ref[...] = w_ref[...].astype(o_ref.dtype)


def _layer_bf16(w_stack, l):
    _, e, a, b = w_stack.shape
    return pl.pallas_call(
        _cast_kernel,
        grid=(e,),
        in_specs=[pl.BlockSpec((None, None, a, b), lambda i: (l, i, 0, 0))],
        out_specs=pl.BlockSpec((None, a, b), lambda i: (i, 0, 0)),
        out_shape=jax.ShapeDtypeStruct((e, a, b), BF16),
        compiler_params=_cp("parallel"),
        name="cast_bf16",
    )(w_stack)


def _moe_kernel(te_ref, nv_ref, x_ref, wg_ref, wu_ref, wd_ref, o_ref):
    del te_ref
    valid = pl.program_id(0) < nv_ref[0]

    @pl.when(valid)
    def _():
        half = x_ref.shape[1]
        x_lo, x_hi = _unpack_rows(x_ref[...])
        x_lo, x_hi = x_lo.astype(BF16), x_hi.astype(BF16)

        def proj(w_ref):
            return (jnp.dot(x_lo, w_ref[:half, :], preferred_element_type=F32)
                    + jnp.dot(x_hi, w_ref[half:, :], preferred_element_type=F32))

        g = proj(wg_ref)
        u = proj(wu_ref)
        hid = (g * _sigmoid(g) * u).astype(BF16)
        o_ref[...] = _pack_rows(jnp.dot(hid, wd_ref[...], preferred_element_type=F32))

    @pl.when(jnp.logical_not(valid))
    def _():
        o_ref[...] = jnp.zeros_like(o_ref)


def _moe_experts(xs, tile_expert, n_valid, w_gate, w_up, w_down, tm):
    n_rows, half = xs.shape
    d, f = w_gate.shape[1:]
    xi = lambda i, te, nv: (jnp.maximum(jnp.minimum(i, nv[0] - 1), 0), 0)
    grid_spec = pltpu.PrefetchScalarGridSpec(
        num_scalar_prefetch=2,
        grid=(n_rows // tm,),
        in_specs=[pl.BlockSpec((tm, half), xi),
                  pl.BlockSpec((None, d, f), lambda i, te, nv: (te[i], 0, 0)),
                  pl.BlockSpec((None, d, f), lambda i, te, nv: (te[i], 0, 0)),
                  pl.BlockSpec((None, f, d), lambda i, te, nv: (te[i], 0, 0))],
        out_specs=pl.BlockSpec((tm, half), lambda i, te, nv: (i, 0)),
    )
    return pl.pallas_call(
        _moe_kernel,
        grid_spec=grid_spec,
        out_shape=jax.ShapeDtypeStruct((n_rows, half), U32),
        compiler_params=_cp("arbitrary"),
        name="moe_experts",
    )(tile_expert, n_valid, xs, w_gate, w_up, w_down)


def _combine_kernel(p_hbm, ys_hbm, x_ref, slab_ref, ga_ref, g_ref, o_ref, idx, buf, sem_i, sem_x, *, final):
    i = pl.program_id(0)
    n = pl.num_programs(0)
    rows = idx.shape[0]
    tm = rows // TOP_K

    def fetch(tile, s):
        _stage_indices(p_hbm.at[tile], idx, sem_i)

        def issue(t, c):
            for k in range(TOP_K):
                r = k * tm + t
                pltpu.make_async_copy(ys_hbm.at[pl.ds(idx[r], 1), :], buf.at[s, pl.ds(r, 1), :],
                                      sem_x.at[s]).start(priority=k % 2)
            return c

        lax.fori_loop(0, tm, issue, 0, unroll=ROW_DMA_UNROLL)

    @pl.when(i == 0)
    def _():
        fetch(0, 0)

    @pl.when(i + 1 < n)
    def _():
        fetch(i + 1, (i + 1) % 2)

    s = i % 2
    pltpu.make_async_copy(ys_hbm.at[pl.ds(0, rows), :], buf.at[s], sem_x.at[s]).wait()
    slab = slab_ref[...]
    y_lo = y_hi = None
    for k in range(TOP_K):
        w_k = slab[:, TOP_K + k:TOP_K + k + 1]
        lo, hi = _unpack_rows(buf[s, k * tm:(k + 1) * tm, :])
        y_lo = w_k * lo if y_lo is None else y_lo + w_k * lo
        y_hi = w_k * hi if y_hi is None else y_hi + w_k * hi
    x = x_ref[...] + ga_ref[...] * jnp.concatenate([y_lo, y_hi], axis=1)
    if final:
        x = x * lax.rsqrt(jnp.mean(x * x, axis=-1, keepdims=True) + NORM_EPS) * g_ref[...]
    o_ref[...] = x


def _combine(ys, slot, slab, x_all, mod3, g_final, ntiles, nlt, tpb, nb, final):
    d = x_all.shape[1]
    tm = TM_COMB
    f = TM // tm
    n = ntiles * f
    p = jnp.transpose(slot.reshape(n, tm, TOP_K), (0, 2, 1)).reshape(n, TOP_K * tm)
    return pl.pallas_call(
        functools.partial(_combine_kernel, final=final),
        grid=(n,),
        in_specs=[pl.BlockSpec(memory_space=pl.ANY), pl.BlockSpec(memory_space=pl.ANY),
                  pl.BlockSpec((tm, d), lambda i: (i, 0)),
                  pl.BlockSpec((tm, 128), lambda i: (i, 0)),
                  pl.BlockSpec((None, 1, d), lambda i: (jnp.where(i < nlt * f, i // (tpb * f), nb), 0, 5)),
                  pl.BlockSpec((1, d), lambda i: (0, 0))],
        out_specs=pl.BlockSpec((tm, d), lambda i: (i, 0)),
        out_shape=jax.ShapeDtypeStruct((n * tm, d), F32),
        scratch_shapes=[pltpu.SMEM((TOP_K * tm,), jnp.int32),
                        pltpu.VMEM((2, TOP_K * tm, ys.shape[1]), ys.dtype),
                        pltpu.SemaphoreType.DMA(()), pltpu.SemaphoreType.DMA((2,))],
        compiler_params=_cp("arbitrary"),
        name="combine",
    )(p, ys, x_all, slab, mod3, g_final.reshape(1, d))


def kernel(x, c, ctx, c_ctx, w_mod, b_mod, g_norm1, g_norm2, g_final, w_in, w_merge, w_branch, w_out, w_hconv,
           hy_bias, hf_w1, hf_b1, hf_w2, hf_b2, hf_w3, hf_freq, w_sconv, w_lconv, lru_wa, lru_ba, lru_wx, lru_bx,
           lru_lambda, w_rg, b_rg, w_re, b_re, w_e_gate, w_e_up, w_e_down):
    nb, seq, d = x.shape
    cl = ctx.shape[1]
    depth = w_in.shape[0]
    db = w_branch.shape[2]
    gdim = db // FNET_GROUPS
    ngroups = w_rg.shape[2]
    epg = w_re.shape[3]
    n_experts = ngroups * epg
    t_lat, t_ctx = nb * seq, nb * cl
    t_all = t_lat + t_ctx
    nlt, tpb = t_lat // TM, seq // TM
    nat = t_all // TM
    tl_lat, tl_ctx = min(TL, seq), min(TL, cl)

    x_all = jnp.concatenate([x.reshape(t_lat, d), ctx.reshape(t_ctx, d)], axis=0)
    nrows = -(-(nb + 1) // 8) * 8
    cvec = jnp.zeros((nrows, d), F32).at[:nb].set(c).at[nb].set(c_ctx)
    mods = _adaln(cvec, w_mod, b_mod)

    use_fft = nb % 2 == 0 and seq % (8 * FFT_N2) == 0
    if use_fft:
        hyfft = _fft_consts(seq)
    else:
        c_lat, sf_lat, si_lat = _dft_mats(seq)
        fwd_lat, inv_lat = (c_lat, sf_lat), (c_lat, si_lat)
    c_ctx, sf_ctx, si_ctx = _dft_mats(cl)
    fwd_ctx, inv_ctx = (c_ctx, sf_ctx), (c_ctx, si_ctx)
    fc_lat, fs_lat, w_pq = _fnet_mats(seq, gdim, FNET_GROUPS)
    fc_ctx, fs_ctx, _ = _fnet_mats(cl, gdim, FNET_GROUPS)

    out = None
    for l in range(depth):
        last = l == depth - 1
        mod3 = mods[l].reshape(nrows, 1, 6 * d)
        u_all, h_bf = _inproj(x_all, mod3, g_norm1[l], w_in[l].astype(BF16), nlt, tpb, nb)

        wa = [_block_diag(lru_wa[l, dd]) for dd in range(2)]
        wx = [_block_diag(lru_wx[l, dd]) for dd in range(2)]
        ba = [lru_ba[l, dd].reshape(1, db) for dd in range(2)]
        bx = [lru_bx[l, dd].reshape(1, db) for dd in range(2)]
        lam = [lru_lambda[l, dd].reshape(1, db) for dd in range(2)]
        zeros_st = jnp.zeros((nb, 1, db), F32)
        rows_out = t_lat if last else t_all

        def lru(row0, length, tl, rw, h0f, h0b, combine, out_init):
            hf, stf = _lru_pass(u_all, w_lconv[l], wa[0], wx[0], ba[0], bx[0], lam[0], h0f,
                                row0, nb, length, tl, rw, db, False)
            if combine:
                y, stb = _lru_pass(u_all, w_lconv[l], wa[1], wx[1], ba[1], bx[1], lam[1], h0b,
                                   row0, nb, length, tl, rw, db, True, hf=hf, out_init=out_init, t_all=rows_out)
            else:
                y, stb = _lru_pass(u_all, w_lconv[l], wa[1], wx[1], ba[1], bx[1], lam[1], h0b,
                                   row0, nb, length, tl, rw, db, True)
            return y, stf, stb

        bias0, bias1 = hy_bias[l, 0].reshape(1, db), hy_bias[l, 1].reshape(1, db)

        def hyena_dense(row0, length, fwd, inv, v, x1, x2, v_bf, out_init):
            filt = _hyena_filters(length, hf_w1[l], hf_b1[l], hf_w2[l], hf_b2[l], hf_w3[l], hf_freq[l], db)
            pqr = _filter_spectra(fwd, filt, length, db)
            yhat = _hy_fwd(fwd, v_bf, nb, length, db, pqr[0])
            z2 = _hy_inv(inv, yhat, v, 0, x1, 0, bias0, nb, length, db, 0, nb * length, None)
            yhat = _hy_fwd(fwd, z2, nb, length, db, pqr[1])
            return _hy_inv(inv, yhat, z2, 0, x2, 0, bias1, nb, length, db, row0, rows_out, out_init)

        def hyena_fft(length, v, x1, x2, v_bf, out_init):
            filt = _hyena_filters(length, hf_w1[l], hf_b1[l], hf_w2[l], hf_b2[l], hf_w3[l], hf_freq[l], db,
                                  circular=True)
            kspec = _fft_filter_spectra(hyfft, filt, length, db)
            z2 = _fft_conv(hyfft, v_bf, kspec[0], v, x1, bias0, nb, db, None, nb * length)
            return _fft_conv(hyfft, z2, kspec[1], z2, x2, bias1, nb, db, out_init, rows_out)

        zbuf = lambda: None if last else jnp.zeros((rows_out, db), BF16)
        if last:
            _, st_f, st_b = lru(t_lat, cl, tl_ctx, cl, zeros_st, zeros_st, False, None)
        else:
            y_lru_c, st_f, st_b = lru(t_lat, cl, tl_ctx, cl, zeros_st, zeros_st, True, zbuf())
        y_lru, _, _ = lru(0, seq, tl_lat, GRID_W, st_f, st_b, True, None if last else y_lru_c)

        pq = _fnet_pq(u_all, w_pq, db)
        v, x1, x2, v_bf, y_sc = _conv_stage(u_all, w_hconv[l], w_sconv[l], 0, t_lat, tl_lat, GRID_W, db,
                                            jnp.zeros((rows_out, db), BF16))
        y_fn = _fnet_dft(fc_lat, fs_lat, pq, 0, nb, seq, db, gdim, zbuf(), rows_out)
        if not last:
            vc, x1c, x2c, vc_bf, y_sc = _conv_stage(u_all, w_hconv[l], w_sconv[l], t_lat, t_ctx, tl_ctx, cl, db,
                                                    y_sc)
            y_fn = _fnet_dft(fc_ctx, fs_ctx, pq, t_lat, nb, cl, db, gdim, y_fn, rows_out)
            y_hy = hyena_dense(t_lat, cl, fwd_ctx, inv_ctx, vc, x1c, x2c, vc_bf, zbuf())
        else:
            y_hy = None
        if use_fft:
            y_hy = hyena_fft(seq, v, x1, x2, v_bf, y_hy)
        else:
            y_hy = hyena_dense(0, seq, fwd_lat, inv_lat, v, x1, x2, v_bf, y_hy)

        ntiles = nlt if last else nat
        x_mid = _merge(h_bf, (y_fn, y_hy, y_sc, y_lru), w_merge[l].astype(BF16), w_branch[l].astype(BF16),
                       w_out[l].astype(BF16), x_all, mod3, ntiles, nlt, tpb, nb)

        w_router = jnp.zeros((d, 128), F32).at[:, :ngroups].set(w_rg[l])
        w_router = w_router.at[:, ngroups:ngroups + n_experts].set(
            jnp.transpose(w_re[l], (1, 0, 2)).reshape(d, n_experts))
        b_router = jnp.zeros((1, 128), F32).at[0, :ngroups].set(b_rg[l])
        b_router = b_router.at[0, ngroups:ngroups + n_experts].set(b_re[l].reshape(-1))
        h2, slab, tile_counts = _router(x_mid, mod3, g_norm2[l], w_router, b_router, ntiles, nlt, tpb, nb,
                                        ngroups, epg)
        slot, tile_expert, n_valid, n_slot_tiles = _route_tables(slab, tile_counts, ngroups, n_experts, TM_MOE)
        xs = _dispatch(h2, slot, n_slot_tiles * TM_MOE)
        ys = _moe_experts(xs, tile_expert, n_valid, _layer_bf16(w_e_gate, l), _layer_bf16(w_e_up, l),
                          _layer_bf16(w_e_down, l), TM_MOE)
        x_all = _combine(ys, slot, slab, x_mid, mod3, g_final, ntiles, nlt, tpb, nb, last)
        if last:
            out = x_all.reshape(nb, seq, d)
    return out
```

```python
import functools
import math

import jax
import jax.numpy as jnp
from jax import lax
from jax.experimental import pallas as pl
from jax.experimental.pallas import tpu as pltpu

F32 = jnp.float32
BF16 = jnp.bfloat16
HIGHEST = lax.Precision.HIGHEST

NORM_EPS = 1e-6
GRID_W = 64
FNET_GROUPS = 4
LRU_C = 8.0
HY_DECAY_TARGET = 1e-2
HY_FAST_PCT = 0.3
HY_SLOW_PCT = 1.5
TOP_K = 2

TM = 512
TL = 1024
TR = 512
TM_MOE = 512
TM_COMB = 512
TN_IN = 1536
TN_MERGE = 512
VMEM_LIMIT = 56 * 1024 * 1024


def _cp(*sem):
    return pltpu.CompilerParams(dimension_semantics=sem, vmem_limit_bytes=VMEM_LIMIT)


def _sigmoid(x):
    return 0.5 * jnp.tanh(0.5 * x) + 0.5


def _adaln_kernel(c_ref, w_ref, b_ref, o_ref):
    c = c_ref[...]
    s = c * _sigmoid(c)
    o_ref[...] = jnp.dot(s, w_ref[...], preferred_element_type=F32, precision=HIGHEST) + b_ref[...]


def _adaln(cvec, w_mod, b_mod):
    nl, d, n6 = w_mod.shape
    r = cvec.shape[0]
    tn = min(1024, n6)
    return pl.pallas_call(
        _adaln_kernel,
        grid=(nl, n6 // tn),
        in_specs=[pl.BlockSpec((r, d), lambda l, j: (0, 0)),
                  pl.BlockSpec((None, d, tn), lambda l, j: (l, 0, j)),
                  pl.BlockSpec((None, 1, tn), lambda l, j: (l, 0, j))],
        out_specs=pl.BlockSpec((None, r, tn), lambda l, j: (l, 0, j)),
        out_shape=jax.ShapeDtypeStruct((nl, r, n6), F32),
        compiler_params=_cp("parallel", "parallel"),
        name="adaln",
    )(cvec, w_mod, b_mod.reshape(nl, 1, n6))


def _norm_mod(x, g, shift, scale):
    y = x * lax.rsqrt(jnp.mean(x * x, axis=-1, keepdims=True) + NORM_EPS) * g
    return y * (1.0 + scale) + shift


def _norm_mod_store(dst, x_ref, g_ref, sh_ref, sc_ref, rows=128):
    rows = min(rows, dst.shape[0])

    def body(c, carry):
        sl = pl.ds(pl.multiple_of(c * rows, rows), rows)
        dst[sl, :] = _norm_mod(x_ref[sl, :], g_ref[...], sh_ref[...], sc_ref[...]).astype(dst.dtype)
        return carry

    lax.fori_loop(0, dst.shape[0] // rows, body, 0)


U32 = jnp.uint32


def _pack_rows(v):
    half = v.shape[1] // 2
    lo = pltpu.bitcast(v[:, :half].astype(BF16).astype(F32), U32) >> 16
    hi = pltpu.bitcast(v[:, half:].astype(BF16).astype(F32), U32) & jnp.uint32(0xFFFF0000)
    return hi | lo


def _unpack_rows(w):
    lo = pltpu.bitcast(w << 16, F32)
    hi = pltpu.bitcast(w & jnp.uint32(0xFFFF0000), F32)
    return lo, hi


def _mod_spec(d, chunk, nlt, tpb, nb):
    return pl.BlockSpec((None, 1, d), lambda i, *_: (jnp.where(i < nlt, i // tpb, nb), 0, chunk))


def _inproj_kernel(x_ref, sh_ref, sc_ref, g_ref, w_ref, u_ref, h_ref, *, tn):
    j = pl.program_id(1)

    @pl.when(j == 0)
    def _():
        _norm_mod_store(h_ref, x_ref, g_ref, sh_ref, sc_ref)

    for jc in range(w_ref.shape[1] // tn):
        @pl.when(j == jc)
        def _(jc=jc):
            u_ref[...] = jnp.dot(h_ref[...], w_ref[:, jc * tn:(jc + 1) * tn], preferred_element_type=F32)


def _inproj(x_all, mod3, g, w_bf, nlt, tpb, nb):
    t, d = x_all.shape
    n = w_bf.shape[1]
    tn = TN_IN if n % TN_IN == 0 else n
    return pl.pallas_call(
        functools.partial(_inproj_kernel, tn=tn),
        grid=(t // TM, n // tn),
        in_specs=[pl.BlockSpec((TM, d), lambda i, j: (i, 0)),
                  _mod_spec(d, 0, nlt, tpb, nb), _mod_spec(d, 1, nlt, tpb, nb),
                  pl.BlockSpec((1, d), lambda i, j: (0, 0)),
                  pl.BlockSpec((d, n), lambda i, j: (0, 0))],
        out_specs=[pl.BlockSpec((TM, tn), lambda i, j: (i, j)), pl.BlockSpec((TM, d), lambda i, j: (i, 0))],
        out_shape=[jax.ShapeDtypeStruct((t, n), F32), jax.ShapeDtypeStruct((t, d), BF16)],
        compiler_params=_cp("parallel", "arbitrary"),
        name="inproj",
    )(x_all, mod3, mod3, g.reshape(1, d), w_bf)


def _dwconv(x, w_ref, left, rw):
    tl = x.shape[0]
    pos = lax.broadcasted_iota(jnp.int32, x.shape, 0) % rw
    acc = None
    for k in range(w_ref.shape[0]):
        off = k - left
        if off == 0:
            term = x
        else:
            shifted = pltpu.roll(x, (-off) % tl, axis=0)
            ok = (pos + off >= 0) & (pos + off < rw)
            term = jnp.where(ok, shifted, 0.0)
        term = term * w_ref[k:k + 1, :]
        acc = term if acc is None else acc + term
    return acc


def _conv_kernel(hv_ref, h1_ref, h2_ref, sb_ref, sc_ref, sh_ref, wh_ref, ws_ref, ys_in,
                 v_ref, x1_ref, x2_ref, vbf_ref, ys_ref, *, rw, db):
    del ys_in
    for k, (ref, dst) in enumerate(((hv_ref, v_ref), (h1_ref, x1_ref), (h2_ref, x2_ref))):
        y = _dwconv(ref[...], wh_ref.at[:, k * db:(k + 1) * db], 1, rw)
        dst[...] = y
        if k == 0:
            vbf_ref[...] = y.astype(BF16)
    ys_ref[...] = (sb_ref[...] * _dwconv(sc_ref[...] * sh_ref[...], ws_ref, 1, rw)).astype(BF16)


def _conv_stage(u_all, w_hconv, w_sconv, row0, nrows, tl, rw, db, ys_init):
    ob = row0 // tl
    col = lambda c: pl.BlockSpec((tl, db), lambda i: (ob + i, c))
    own = pl.BlockSpec((tl, db), lambda i: (i, 0))
    return pl.pallas_call(
        functools.partial(_conv_kernel, rw=rw, db=db),
        grid=(nrows // tl,),
        in_specs=[col(1), col(2), col(3), col(4), col(5), col(6),
                  pl.BlockSpec(w_hconv.shape, lambda i: (0, 0)),
                  pl.BlockSpec(w_sconv.shape, lambda i: (0, 0)),
                  pl.BlockSpec(memory_space=pl.ANY)],
        out_specs=[own, own, own, own, pl.BlockSpec((tl, db), lambda i: (ob + i, 0))],
        out_shape=[jax.ShapeDtypeStruct((nrows, db), F32)] * 3
                  + [jax.ShapeDtypeStruct((nrows, db), BF16), jax.ShapeDtypeStruct(ys_init.shape, BF16)],
        input_output_aliases={8: 4},
        compiler_params=_cp("parallel"),
        name="conv",
    )(*([u_all] * 6), w_hconv, w_sconv, ys_init)


def _trig_outer(l, period):
    q = 1 << ((l.bit_length() - 1) // 2)
    n = lax.broadcasted_iota(jnp.int32, (1, l), 1)
    scale = 2.0 * math.pi / period

    def table(rows, step):
        r = lax.broadcasted_iota(jnp.int32, (rows, 1), 0) * step
        ang = ((r * n) % period).astype(F32) * scale
        return jnp.cos(ang), jnp.sin(ang)

    ac, as_ = table(l // q, q)
    bc, bs = table(q, 1)
    c = ac[:, None, :] * bc[None] - as_[:, None, :] * bs[None]
    s = as_[:, None, :] * bc[None] + ac[:, None, :] * bs[None]
    return c.reshape(l, l), s.reshape(l, l)


def _dft_mats(l):
    c, s = _trig_outer(l, 2 * l)
    k = lax.broadcasted_iota(jnp.int32, (l, l), 0)
    n = lax.broadcasted_iota(jnp.int32, (l, l), 1)
    alt_n = jnp.where(n % 2 == 0, 1.0, -1.0).astype(F32)
    alt_k = jnp.where(k % 2 == 0, 1.0, -1.0).astype(F32)
    return c.astype(BF16), jnp.where(k == 0, alt_n, -s).astype(BF16), jnp.where(n == 0, alt_k, -s).astype(BF16)


def _fnet_mats(l, gdim, groups):
    c, s = _trig_outer(l, l)
    cg, sg = _trig_outer(gdim, gdim)
    eye = jnp.eye(groups, dtype=F32)
    w_pq = jnp.concatenate([jnp.kron(eye, cg), jnp.kron(eye, sg)], axis=1).astype(BF16)
    return c.astype(BF16), (-s).astype(BF16), w_pq


def _fnet_pq_kernel(u_ref, w_ref, o_ref):
    o_ref[...] = jnp.dot(u_ref[...].astype(BF16), w_ref[...], preferred_element_type=F32).astype(BF16)


def _fnet_pq(u_all, w_pq, db):
    t = u_all.shape[0]
    return pl.pallas_call(
        _fnet_pq_kernel,
        grid=(t // TM,),
        in_specs=[pl.BlockSpec((TM, db), lambda i: (i, 0)),
                  pl.BlockSpec(w_pq.shape, lambda i: (0, 0))],
        out_specs=pl.BlockSpec((TM, 2 * db), lambda i: (i, 0)),
        out_shape=jax.ShapeDtypeStruct((t, 2 * db), BF16),
        compiler_params=_cp("parallel"),
        name="fnet_pq",
    )(u_all, w_pq)


def _fnet_dft_kernel(d0_ref, d1_ref, x_ref, *rest, db, scale):
    o_ref = rest[-1]
    acc = jnp.dot(d0_ref[...], x_ref[:, :db], preferred_element_type=F32)
    acc = acc + jnp.dot(d1_ref[...], x_ref[:, db:], preferred_element_type=F32)
    o_ref[...] = (acc * scale).astype(BF16)


def _mat_spec(tr, l):
    return pl.BlockSpec((tr, l), lambda i, b: (i, 0))


def _fnet_dft(d0, d1, pq, row0, nb, l, db, gdim, out_init, t_all):
    tr = min(TR, l)
    nrt = l // tr
    args = [d0, d1, pq]
    in_specs = [_mat_spec(tr, l), _mat_spec(tr, l),
                pl.BlockSpec((l, 2 * db), lambda i, b: (row0 // l + b, 0))]
    aliases = {}
    if out_init is not None:
        args.append(out_init)
        in_specs.append(pl.BlockSpec(memory_space=pl.ANY))
        aliases = {3: 0}
    return pl.pallas_call(
        functools.partial(_fnet_dft_kernel, db=db, scale=1.0 / math.sqrt(l * gdim)),
        grid=(nrt, nb),
        in_specs=in_specs,
        out_specs=pl.BlockSpec((tr, db), lambda i, b: (row0 // tr + b * nrt + i, 0)),
        out_shape=jax.ShapeDtypeStruct((t_all, db), BF16),
        input_output_aliases=aliases,
        compiler_params=_cp("parallel", "parallel"),
        name="fnet_dft",
    )(*args)


def _fnet_fused_kernel(d0_ref, d1_ref, u_ref, w_ref, *rest, db, scale):
    o_ref, pq = rest[-2], rest[-1]
    l = u_ref.shape[0]
    rows = min(512, l)

    @pl.when(pl.program_id(1) == 0)
    def _():
        def body(c, carry):
            sl = pl.ds(pl.multiple_of(c * rows, rows), rows)
            pq[sl, :] = jnp.dot(u_ref[sl, :].astype(BF16), w_ref[...], preferred_element_type=F32).astype(BF16)
            return carry

        lax.fori_loop(0, l // rows, body, 0)

    acc = jnp.dot(d0_ref[...], pq[:, :db], preferred_element_type=F32)
    acc = acc + jnp.dot(d1_ref[...], pq[:, db:], preferred_element_type=F32)
    o_ref[...] = (acc * scale).astype(BF16)


def _fnet_fused(d0, d1, u_all, w_pq, row0, nb, l, db, gdim, out_init, t_all):
    tr = min(TR, l)
    nrt = l // tr
    mat = pl.BlockSpec((tr, l), lambda b, i: (i, 0))
    args = [d0, d1, u_all, w_pq]
    in_specs = [mat, mat, pl.BlockSpec((l, db), lambda b, i: (row0 // l + b, 0)),
                pl.BlockSpec(w_pq.shape, lambda b, i: (0, 0))]
    aliases = {}
    if out_init is not None:
        args.append(out_init)
        in_specs.append(pl.BlockSpec(memory_space=pl.ANY))
        aliases = {4: 0}
    return pl.pallas_call(
        functools.partial(_fnet_fused_kernel, db=db, scale=1.0 / math.sqrt(l * gdim)),
        grid=(nb, nrt),
        in_specs=in_specs,
        out_specs=pl.BlockSpec((tr, db), lambda b, i: (row0 // tr + b * nrt + i, 0)),
        out_shape=jax.ShapeDtypeStruct((t_all, db), BF16),
        scratch_shapes=[pltpu.VMEM((l, 2 * db), BF16)],
        input_output_aliases=aliases,
        compiler_params=_cp("parallel", "arbitrary"),
        name="fnet_fused",
    )(*args)


def _hy_fwd_kernel(f0_ref, f1_ref, x_ref, *rest, mult):
    o_ref = rest[-1]
    zre = jnp.dot(f0_ref[...], x_ref[...], preferred_element_type=F32)
    zim = jnp.dot(f1_ref[...], x_ref[...], preferred_element_type=F32)
    if mult:
        k_ref = rest[0]
        p, q, r = k_ref[0], k_ref[1], k_ref[2]
        o_ref[0] = (zre * p - zim * q).astype(o_ref.dtype)
        o_ref[1] = (zre * q + zim * r).astype(o_ref.dtype)
    else:
        o_ref[0] = zre
        o_ref[1] = zim


def _hy_fwd(fwd, x, nb, l, db, pqr):
    tr = min(TR, l)
    args = [fwd[0], fwd[1], x]
    in_specs = [_mat_spec(tr, l), _mat_spec(tr, l),
                pl.BlockSpec((l, db), lambda i, b: (b, 0))]
    if pqr is not None:
        args.append(pqr)
        in_specs.append(pl.BlockSpec((3, tr, db), lambda i, b: (0, i, 0)))
    return pl.pallas_call(
        functools.partial(_hy_fwd_kernel, mult=pqr is not None),
        grid=(l // tr, nb),
        in_specs=in_specs,
        out_specs=pl.BlockSpec((None, 2, tr, db), lambda i, b: (b, 0, i, 0)),
        out_shape=jax.ShapeDtypeStruct((nb, 2, l, db), F32 if pqr is None else BF16),
        compiler_params=_cp("parallel", "parallel"),
        name="hy_fwd",
    )(*args)


def _hy_inv_kernel(g0_ref, g1_ref, y_ref, zp_ref, gate_ref, bias_ref, *rest):
    o_ref = rest[-1]
    y = jnp.dot(g0_ref[...], y_ref[0], preferred_element_type=F32)
    y = y + jnp.dot(g1_ref[...], y_ref[1], preferred_element_type=F32)
    o_ref[...] = (gate_ref[...] * (y + bias_ref[...] * zp_ref[...].astype(F32))).astype(BF16)


def _hy_inv(inv, yhat, zprev, zcol, gate, gcol, bias, nb, l, db, out_row0, out_rows, out_init):
    tr = min(TR, l)
    nrt = l // tr
    args = [inv[0], inv[1], yhat, zprev, gate, bias]
    in_specs = [_mat_spec(tr, l), _mat_spec(tr, l),
                pl.BlockSpec((None, 2, l, db), lambda i, b: (b, 0, 0, 0)),
                pl.BlockSpec((tr, db), lambda i, b: (b * nrt + i, zcol)),
                pl.BlockSpec((tr, db), lambda i, b: (b * nrt + i, gcol)),
                pl.BlockSpec((1, db), lambda i, b: (0, 0))]
    aliases = {}
    if out_init is not None:
        args.append(out_init)
        in_specs.append(pl.BlockSpec(memory_space=pl.ANY))
        aliases = {6: 0}
    return pl.pallas_call(
        _hy_inv_kernel,
        grid=(nrt, nb),
        in_specs=in_specs,
        out_specs=pl.BlockSpec((tr, db), lambda i, b: (out_row0 // tr + b * nrt + i, 0)),
        out_shape=jax.ShapeDtypeStruct((out_rows, db), BF16),
        input_output_aliases=aliases,
        compiler_params=_cp("parallel", "parallel"),
        name="hy_inv",
    )(*args)


def _hyena_filters(length, hf_w1, hf_b1, hf_w2, hf_b2, hf_w3, hf_freq, db, circular=False):
    emb = hf_w1.shape[0]
    nbands = (emb - 1) // 2
    bands = jnp.linspace(1e-4, nbands - 1, nbands, dtype=F32)[None, :]
    min_decay = math.log(HY_DECAY_TARGET) / HY_SLOW_PCT
    max_decay = math.log(HY_DECAY_TARGET) / HY_FAST_PCT
    deltas = jnp.abs(jnp.linspace(min_decay, max_decay, db, dtype=F32))
    w3 = hf_w3.reshape(hf_w3.shape[0], -1, 2, db)

    def direction(pos, d, order_major):
        t = (pos * (1.0 / (length - 1)))[:, None]
        w = (2.0 * math.pi / length) * pos[:, None]
        feats = jnp.concatenate([t, jnp.cos(bands * w), -jnp.sin(bands * w)], axis=-1)
        z = jnp.sin(hf_freq[0] * (jnp.dot(feats, hf_w1, precision=HIGHEST) + hf_b1))
        z = jnp.sin(hf_freq[1] * (jnp.dot(z, hf_w2, precision=HIGHEST) + hf_b2))
        decay = jnp.exp(-t * deltas)
        if order_major:
            return jnp.einsum("lf,fod->old", z, w3[:, :, d, :], precision=HIGHEST) * decay[None]
        return jnp.einsum("lf,fod->lod", z, w3[:, :, d, :], precision=HIGHEST) * decay[:, None, :]

    pos = jnp.arange(length, dtype=F32)
    if circular:
        first = (pos == 0)[None, :, None]
        f0 = direction(pos, 0, True)
        f1 = direction(jnp.where(pos == 0, 0.0, length - pos), 1, True)
        scale = lax.rsqrt(jnp.sum(f0 * f0, axis=1, keepdims=True) + jnp.sum(f1 * f1, axis=1, keepdims=True)
                          + NORM_EPS)
        return jnp.stack([f0 * scale, jnp.where(first, 0.0, f1) * scale], axis=1)
    filt = jnp.stack([direction(pos, 0, False), direction(pos, 1, False)], axis=2)
    return filt * lax.rsqrt(jnp.sum(filt * filt, axis=(0, 2), keepdims=True) + NORM_EPS)


def _filter_spectra(fwd, filt, l, db):
    n_ord = filt.shape[1]
    cols = jnp.transpose(filt, (1, 2, 0, 3))
    cols = cols.at[:, 1, 0, :].set(0.0)
    x = cols.reshape(n_ord * 2 * l, db).astype(BF16)
    spec = _hy_fwd(fwd, x, n_ord * 2, l, db, None).reshape(n_ord, 2, 2, l, db)
    hf, hb = spec[:, 0], spec[:, 1]
    k_re = hf[:, 0] + hb[:, 0]
    k_im = hf[:, 1] - hb[:, 1]
    k_nyq = hf[:, 1, 0] + hb[:, 1, 0]
    first = (jnp.arange(l) == 0)[None, :, None]
    scale = jnp.where(first, 1.0 / (2 * l), 2.0 / (2 * l)).astype(F32)
    p = k_re * scale
    q = jnp.where(first, 0.0, k_im) * scale
    r = jnp.where(first, k_nyq[:, None, :], k_re) * scale
    return jnp.stack([p, q, r], axis=1)


FFT_N2 = 128
FFT_NT = 16
FFT_KC = 8


def _fft_consts(l):
    n, n2 = 2 * l, FFT_N2
    n1 = n // n2
    h = n1 // 2
    cat = jnp.concatenate
    ia = jnp.arange(n1, dtype=jnp.int32)
    ang = ((ia[:, None] * ia[None, :]) % n1).astype(F32) * (2.0 * math.pi / n1)
    fr, fi = jnp.cos(ang), -jnp.sin(ang)
    mat_data = cat([cat([fr[:, :h], -fi[:, :h]], 1), cat([fi[:, :h], fr[:, :h]], 1)], 0)
    mat_filt = cat([fr, fi], 0)
    gr, gi = fr[:, :h].T, -fi[:, :h].T
    mat_inv = jnp.stack([cat([gr, -gi], 1), cat([gi, gr], 1)])
    k = ia[:, None, None] + n1 * jnp.arange(n2, dtype=jnp.int32)[None, :, None]
    nn = jnp.arange(n2, dtype=jnp.int32)[None, None, :]
    angb = ((k * nn) % n).astype(F32) * (2.0 * math.pi / n)
    er, ei = jnp.cos(angb), -jnp.sin(angb)
    mb = cat([cat([er, -ei], 2), cat([ei, er], 2)], 1)
    eye = jnp.eye(FFT_NT, dtype=F32)
    kron = lambda a: jnp.kron(a, eye).astype(BF16)
    return dict(n1=n1, h=h, mat_data=kron(mat_data), mat_filt=kron(mat_filt * (1.0 / n)),
                mat_inv=jnp.stack([kron(mat_inv[0]), kron(mat_inv[1])]),
                mb=mb.astype(BF16), mib=jnp.transpose(mb, (0, 2, 1)).astype(BF16))


def _fft_a_kernel(m_ref, a_ref, b_ref, o_ref):
    h, nt, db = a_ref.shape
    x = jnp.concatenate([a_ref[...].reshape(h * nt, db), b_ref[...].reshape(h * nt, db)], axis=0)
    res = jnp.dot(m_ref[...], x, preferred_element_type=F32)
    o_ref[...] = res.astype(o_ref.dtype).reshape(o_ref.shape)


def _fft_stage_a(mat, x4, npairs, stride, offset):
    _, h, n2, db = x4.shape
    nt = FFT_NT
    rows = mat.shape[0] // nt
    return pl.pallas_call(
        _fft_a_kernel,
        grid=(npairs, n2 // nt),
        in_specs=[pl.BlockSpec(mat.shape, lambda p, j: (0, 0)),
                  pl.BlockSpec((None, h, nt, db), lambda p, j: (p * stride, 0, j, 0)),
                  pl.BlockSpec((None, h, nt, db), lambda p, j: (p * stride + offset, 0, j, 0))],
        out_specs=pl.BlockSpec((None, rows, nt, db), lambda p, j: (p, 0, j, 0)),
        out_shape=jax.ShapeDtypeStruct((npairs, rows, n2, db), BF16),
        compiler_params=_cp("parallel", "parallel"),
        name="fft_a",
    )(mat, x4, x4)


def _fft_b_kernel(m_ref, ar_ref, ai_ref, o_ref):
    for kk in range(m_ref.shape[0]):
        a = jnp.concatenate([ar_ref[kk], ai_ref[kk]], axis=0)
        o_ref[kk] = jnp.dot(m_ref[kk], a, preferred_element_type=F32)


def _fft_bb_kernel(m_ref, mi_ref, ar_ref, ai_ref, k_ref, o_ref):
    n2 = ar_ref.shape[1]
    for kk in range(m_ref.shape[0]):
        a = jnp.concatenate([ar_ref[kk], ai_ref[kk]], axis=0)
        x = jnp.dot(m_ref[kk], a, preferred_element_type=F32)
        xr, xi = x[:n2], x[n2:]
        kr, ki = k_ref[kk, :n2], k_ref[kk, n2:]
        y = jnp.concatenate([(xr * kr - xi * ki).astype(BF16), (xr * ki + xi * kr).astype(BF16)], axis=0)
        b = jnp.dot(mi_ref[kk], y, preferred_element_type=F32)
        o_ref[0, kk] = b[:n2].astype(o_ref.dtype)
        o_ref[1, kk] = b[n2:].astype(o_ref.dtype)


def _fft_stage_b(fc, a, kspec, db):
    mb, mib = fc["mb"], fc["mib"]
    npairs = a.shape[0]
    n1, r2, _ = mb.shape
    n2 = r2 // 2
    kc = min(FFT_KC, n1)
    a5 = a.reshape(npairs, 2, n1, n2, db)
    mspec = pl.BlockSpec((kc, r2, r2), lambda c, p: (c, 0, 0))
    aspec = lambda part: pl.BlockSpec((None, None, kc, n2, db), lambda c, p: (p, part, c, 0, 0))
    if kspec is None:
        kern, args, in_specs = _fft_b_kernel, [mb, a5, a5], [mspec, aspec(0), aspec(1)]
        out_spec = pl.BlockSpec((None, kc, r2, db), lambda c, p: (p, c, 0, 0))
        out_shape = jax.ShapeDtypeStruct((npairs, n1, r2, db), F32)
    else:
        kern, args = _fft_bb_kernel, [mb, mib, a5, a5, kspec]
        in_specs = [mspec, mspec, aspec(0), aspec(1), pl.BlockSpec((kc, r2, db), lambda c, p: (c, 0, 0))]
        out_spec = pl.BlockSpec((None, 2, kc, n2, db), lambda c, p: (p, 0, c, 0, 0))
        out_shape = jax.ShapeDtypeStruct((npairs, 2, n1, n2, db), BF16)
    return pl.pallas_call(
        kern,
        grid=(n1 // kc, npairs),
        in_specs=in_specs,
        out_specs=out_spec,
        out_shape=out_shape,
        compiler_params=_cp("parallel", "parallel"),
        name="fft_b",
    )(*args)


def _ifft_a_kernel(m_ref, b_ref, zp_ref, gate_ref, bias_ref, *rest):
    o_ref = rest[-1]
    h, nt, db = o_ref.shape
    bm = b_ref[...].reshape(b_ref.shape[0] * nt, db)
    y = jnp.dot(m_ref[...], bm, preferred_element_type=F32)
    zp = zp_ref[...].reshape(h * nt, db).astype(F32)
    out = gate_ref[...].reshape(h * nt, db) * (y + bias_ref[...] * zp)
    o_ref[...] = out.astype(o_ref.dtype).reshape(o_ref.shape)


def _ifft_stage_a(mat_inv, b5, zprev, gate, bias, nb, db, out_init, out_rows):
    npairs, _, n1, n2, _ = b5.shape
    nt = FFT_NT
    h = mat_inv.shape[1] // nt
    b4 = b5.reshape(npairs, 2 * n1, n2, db)
    view = lambda a: a.reshape(a.shape[0] // n2, n2, db)
    blk = pl.BlockSpec((h, nt, db), lambda b, j: (b, j, 0))
    args = [mat_inv, b4, view(zprev), view(gate), bias]
    in_specs = [pl.BlockSpec((None,) + mat_inv.shape[1:], lambda b, j: (b // npairs, 0, 0)),
                pl.BlockSpec((None, 2 * n1, nt, db), lambda b, j: (b % npairs, 0, j, 0)),
                blk, blk, pl.BlockSpec((1, db), lambda b, j: (0, 0))]
    aliases = {}
    if out_init is not None:
        args.append(view(out_init))
        in_specs.append(pl.BlockSpec(memory_space=pl.ANY))
        aliases = {5: 0}
    out = pl.pallas_call(
        _ifft_a_kernel,
        grid=(nb, n2 // nt),
        in_specs=in_specs,
        out_specs=blk,
        out_shape=jax.ShapeDtypeStruct((out_rows // n2, n2, db), BF16),
        input_output_aliases=aliases,
        compiler_params=_cp("parallel", "parallel"),
        name="ifft_a",
    )(*args)
    return out.reshape(out_rows, db)


def _fft_filter_spectra(fc, filt, l, db):
    n_ord = filt.shape[0]
    x4 = filt.astype(BF16).reshape(n_ord * 2, fc["h"], FFT_N2, db)
    a = _fft_stage_a(fc["mat_filt"], x4, n_ord, 2, 1)
    return _fft_stage_b(fc, a, None, db)


def _fft_conv(fc, z_bf, kspec, zprev, gate, bias, nb, db, out_init, out_rows):
    x4 = z_bf.reshape(nb, fc["h"], FFT_N2, db)
    a = _fft_stage_a(fc["mat_data"], x4, nb // 2, 1, nb // 2)
    b5 = _fft_stage_b(fc, a, kspec, db)
    return _ifft_stage_a(fc["mat_inv"], b5, zprev, gate, bias, nb, db, out_init, out_rows)


def _lru_kernel(*refs, rw, reverse, final):
    if final:
        (x_ref, w_ref, wa_ref, wx_ref, ba_ref, bx_ref, lam_ref, h0_ref, hf_ref, ug_ref,
         _, o_ref, st_ref, a_s, b_s, carry) = refs
    else:
        (x_ref, w_ref, wa_ref, wx_ref, ba_ref, bx_ref, lam_ref, h0_ref,
         o_ref, st_ref, a_s, b_s, carry) = refs
    tl = x_ref.shape[0]

    @pl.when(pl.program_id(1) == 0)
    def _():
        carry[...] = h0_ref[...]

    xc = _dwconv(x_ref[...], w_ref, w_ref.shape[0] // 2, rw)
    xb = xc.astype(BF16)
    r = _sigmoid(jnp.dot(xb, wa_ref[...], preferred_element_type=F32) + ba_ref[...])
    ig = _sigmoid(jnp.dot(xb, wx_ref[...], preferred_element_type=F32) + bx_ref[...])
    lam = lam_ref[...]
    softplus = jnp.maximum(-lam, 0.0) + jnp.log(1.0 + jnp.exp(-jnp.abs(lam)))
    log_a = -LRU_C * r * softplus
    a = jnp.exp(log_a)
    bv = jnp.sqrt(1.0 - jnp.exp(2.0 * log_a)) * (ig * xc)

    pos8 = lax.broadcasted_iota(jnp.int32, a.shape, 0) % 8
    for s in (1, 2, 4):
        if reverse:
            a_sh, b_sh, ok = pltpu.roll(a, tl - s, axis=0), pltpu.roll(bv, tl - s, axis=0), pos8 < 8 - s
        else:
            a_sh, b_sh, ok = pltpu.roll(a, s, axis=0), pltpu.roll(bv, s, axis=0), pos8 >= s
        bv = jnp.where(ok, a * b_sh + bv, bv)
        a = jnp.where(ok, a * a_sh, a)
    a_s[...] = a
    b_s[...] = bv

    ng = tl // 8

    def body(gi, c):
        g = ng - 1 - gi if reverse else gi
        sl = pl.ds(pl.multiple_of(g * 8, 8), 8)
        h8 = b_s[sl, :] + a_s[sl, :] * c
        b_s[sl, :] = h8
        return h8[0:1, :] if reverse else h8[7:8, :]

    c_out = lax.fori_loop(0, ng, body, carry[...])
    carry[...] = c_out
    st_ref[...] = c_out
    if final:
        ug = ug_ref[...]
        gelu = 0.5 * ug * (1.0 + jnp.tanh(math.sqrt(2.0 / math.pi) * (ug + 0.044715 * ug * ug * ug)))
        o_ref[...] = ((hf_ref[...] + b_s[...]) * gelu).astype(BF16)
    else:
        o_ref[...] = b_s[...]


def _lru_pass(u_all, w_lconv, wa, wx, ba, bx, lam, h0, row0, nb, l, tl, rw, db, reverse, hf=None, out_init=None,
              t_all=None):
    final = hf is not None
    nc = l // tl
    ob = row0 // tl
    cidx = (lambda c: nc - 1 - c) if reverse else (lambda c: c)
    full = lambda a: pl.BlockSpec(a.shape, lambda b, c: (0,) * a.ndim)
    ncolx, ncolg = 7, 8
    args = [u_all, w_lconv, wa, wx, ba, bx, lam, h0]
    in_specs = [pl.BlockSpec((tl, db), lambda b, c: (ob + b * nc + cidx(c), ncolx)),
                full(w_lconv), full(wa), full(wx), full(ba), full(bx), full(lam),
                pl.BlockSpec((None, 1, db), lambda b, c: (b, 0, 0))]
    aliases = {}
    if final:
        args += [hf, u_all]
        in_specs += [pl.BlockSpec((tl, db), lambda b, c: (b * nc + cidx(c), 0)),
                     pl.BlockSpec((tl, db), lambda b, c: (ob + b * nc + cidx(c), ncolg))]
        if out_init is None:
            out_init = jnp.zeros((8, 128), BF16)
        else:
            aliases = {10: 0}
        args.append(out_init)
        in_specs.append(pl.BlockSpec(memory_space=pl.ANY))
        out_spec = pl.BlockSpec((tl, db), lambda b, c: (ob + b * nc + cidx(c), 0))
        out_shape = jax.ShapeDtypeStruct((t_all, db), BF16)
    else:
        out_spec = pl.BlockSpec((tl, db), lambda b, c: (b * nc + cidx(c), 0))
        out_shape = jax.ShapeDtypeStruct((nb * l, db), F32)
    return pl.pallas_call(
        functools.partial(_lru_kernel, rw=rw, reverse=reverse, final=final),
        grid=(nb, nc),
        in_specs=in_specs,
        out_specs=[out_spec, pl.BlockSpec((None, 1, db), lambda b, c: (b, 0, 0))],
        out_shape=[out_shape, jax.ShapeDtypeStruct((nb, 1, db), F32)],
        scratch_shapes=[pltpu.VMEM((tl, db), F32), pltpu.VMEM((tl, db), F32), pltpu.VMEM((1, db), F32)],
        input_output_aliases=aliases,
        compiler_params=_cp("arbitrary", "arbitrary"),
        name="lru_bwd" if reverse else "lru_fwd",
    )(*args)


def _block_diag(w):
    h, hd, _ = w.shape
    eye = jnp.eye(h, dtype=w.dtype)
    return (eye[:, None, :, None] * w[:, :, None, :]).reshape(h * hd, h * hd).astype(BF16)


def _merge_kernel(x_ref, h_ref, b0_ref, b1_ref, b2_ref, b3_ref, wm_ref, wb_ref, wo_ref, ga_ref, o_ref):
    j = pl.program_id(1)
    h = h_ref[...]
    m = None
    for k, br in enumerate((b0_ref, b1_ref, b2_ref, b3_ref)):
        g = jnp.dot(h, wm_ref[k], preferred_element_type=F32)
        p = jnp.dot(br[...], wb_ref[k], preferred_element_type=F32)
        term = _sigmoid(g) * p
        m = term if m is None else m + term
    part = jnp.dot(m.astype(BF16), wo_ref[...], preferred_element_type=F32)

    @pl.when(j == 0)
    def _():
        o_ref[...] = part

    @pl.when(j > 0)
    def _():
        o_ref[...] += part

    @pl.when(j == pl.num_programs(1) - 1)
    def _():
        o_ref[...] = x_ref[...] + ga_ref[...] * o_ref[...]


def _merge(h_bf, branches, wm_bf, wb_bf, wo_bf, x_all, mod3, ntiles, nlt, tpb, nb):
    t, d = x_all.shape
    db = wb_bf.shape[1]
    tn = min(TN_MERGE, d)
    row = lambda w: pl.BlockSpec((TM, w), lambda i, j: (i, 0))
    return pl.pallas_call(
        _merge_kernel,
        grid=(ntiles, d // tn),
        in_specs=[row(d), row(d), row(db), row(db), row(db), row(db),
                  pl.BlockSpec((4, d, tn), lambda i, j: (0, 0, j)),
                  pl.BlockSpec((4, db, tn), lambda i, j: (0, 0, j)),
                  pl.BlockSpec((tn, d), lambda i, j: (j, 0)),
                  _mod_spec(d, 2, nlt, tpb, nb)],
        out_specs=row(d),
        out_shape=jax.ShapeDtypeStruct((ntiles * TM, d), F32),
        compiler_params=_cp("parallel", "arbitrary"),
        name="merge",
    )(x_all, h_bf, *branches, wm_bf, wb_bf, wo_bf, mod3)


def _router_kernel(x_ref, sh_ref, sc_ref, g_ref, whi_ref, wlo_ref, br_ref, tri_ref, h_ref, r_ref, c_ref, *,
                   ngroups, epg):
    h = _norm_mod(x_ref[...], g_ref[...], sh_ref[...], sc_ref[...])
    h_ref[...] = _pack_rows(h)
    h_hi = h.astype(BF16)
    h_lo = (h - h_hi.astype(F32)).astype(BF16)
    logits = jnp.dot(h_hi, wlo_ref[...], preferred_element_type=F32)
    logits = logits + jnp.dot(h_lo, whi_ref[...], preferred_element_type=F32)
    logits = logits + jnp.dot(h_hi, whi_ref[...], preferred_element_type=F32) + br_ref[...]
    lane = lax.broadcasted_iota(jnp.int32, logits.shape, 1)
    lane_f = lane.astype(F32)
    neg = jnp.float32(-1e30)
    big = jnp.float32(1e6)
    is_g = lane < ngroups
    lg = jnp.where(is_g, logits, neg)
    mx = jnp.max(lg, axis=-1, keepdims=True)
    g_top = jnp.min(jnp.where(lg == mx, lane_f, big), axis=-1, keepdims=True)
    den = jnp.sum(jnp.where(is_g, jnp.exp(lg - mx), 0.0), axis=-1, keepdims=True)
    p_top = 1.0 / den
    el = lane_f - ngroups
    in_grp = (el >= g_top * epg) & (el < (g_top + 1.0) * epg)
    le = jnp.where(in_grp, logits, neg)
    v1 = jnp.max(le, axis=-1, keepdims=True)
    e1 = jnp.min(jnp.where(le == v1, el, big), axis=-1, keepdims=True)
    le2 = jnp.where(el == e1, neg, le)
    v2 = jnp.max(le2, axis=-1, keepdims=True)
    e2 = jnp.min(jnp.where(le2 == v2, el, big), axis=-1, keepdims=True)
    dlt = jnp.exp(v2 - v1)
    w1 = p_top / (1.0 + dlt)
    w2 = p_top * dlt / (1.0 + dlt)
    pick1 = el == e1
    pick2 = el == e2
    onehot = jnp.where(pick1 | pick2, 1.0, 0.0)
    before = jnp.dot(tri_ref[...], onehot.astype(BF16), preferred_element_type=F32)
    rank1 = jnp.sum(jnp.where(pick1, before, 0.0), axis=-1, keepdims=True)
    rank2 = jnp.sum(jnp.where(pick2, before, 0.0), axis=-1, keepdims=True)
    out = jnp.where(lane == 0, e1, 0.0)
    out = jnp.where(lane == 1, e2, out)
    out = jnp.where(lane == 2, w1, out)
    out = jnp.where(lane == 3, w2, out)
    out = jnp.where(lane == 4, rank1, out)
    out = jnp.where(lane == 5, rank2, out)
    r_ref[...] = out
    c_ref[...] = jnp.broadcast_to(jnp.sum(onehot, axis=0, keepdims=True), c_ref.shape)


def _router(x_all, mod3, g, w_router, b_router, ntiles, nlt, tpb, nb, ngroups, epg):
    t, d = x_all.shape
    w_hi = w_router.astype(BF16)
    w_hi_lo = (w_hi, (w_router - w_hi.astype(F32)).astype(BF16))
    earlier = jnp.tril(jnp.ones((TM, TM), F32), -1).astype(BF16)
    return pl.pallas_call(
        functools.partial(_router_kernel, ngroups=ngroups, epg=epg),
        grid=(ntiles,),
        in_specs=[pl.BlockSpec((TM, d), lambda i: (i, 0)),
                  _mod_spec(d, 3, nlt, tpb, nb), _mod_spec(d, 4, nlt, tpb, nb),
                  pl.BlockSpec((1, d), lambda i: (0, 0)),
                  pl.BlockSpec((d, 128), lambda i: (0, 0)),
                  pl.BlockSpec((d, 128), lambda i: (0, 0)),
                  pl.BlockSpec((1, 128), lambda i: (0, 0)),
                  pl.BlockSpec((TM, TM), lambda i: (0, 0))],
        out_specs=[pl.BlockSpec((TM, d // 2), lambda i: (i, 0)), pl.BlockSpec((TM, 128), lambda i: (i, 0)),
                   pl.BlockSpec((None, 8, 128), lambda i: (i, 0, 0))],
        out_shape=[jax.ShapeDtypeStruct((ntiles * TM, d // 2), U32),
                   jax.ShapeDtypeStruct((ntiles * TM, 128), F32),
                   jax.ShapeDtypeStruct((ntiles, 8, 128), F32)],
        compiler_params=_cp("parallel"),
        name="router",
    )(x_all, mod3, mod3, g.reshape(1, d), w_hi, w_hi_lo[1], b_router, earlier)


def _route_tables(slab, tile_counts, ngroups, n_experts, tm):
    ntiles = tile_counts.shape[0]
    e = slab[:, 0:TOP_K].astype(jnp.int32).reshape(ntiles, TM, TOP_K)
    rank = slab[:, 4:4 + TOP_K].astype(jnp.int32).reshape(ntiles, TM, TOP_K)
    counts_t = tile_counts[:, 0, ngroups:ngroups + n_experts].astype(jnp.int32)
    base = jnp.cumsum(counts_t, axis=0) - counts_t
    counts = jnp.sum(counts_t, axis=0)
    tiles_per_e = (counts + tm - 1) // tm
    tile_end = jnp.cumsum(tiles_per_e)
    starts = (tile_end - tiles_per_e) * tm
    off = starts[None, :] + base
    sel = e[..., None] == jnp.arange(n_experts, dtype=jnp.int32)
    slot = jnp.sum(jnp.where(sel, off[:, None, None, :], 0), axis=-1) + rank
    n_slot_tiles = (ntiles * TM * TOP_K) // tm + n_experts
    n_valid = tile_end[-1]
    tile_ids = jnp.minimum(jnp.arange(n_slot_tiles, dtype=jnp.int32), n_valid - 1)
    tile_expert = jnp.sum((tile_end[None, :] <= tile_ids[:, None]).astype(jnp.int32), axis=1)
    return slot, tile_expert, n_valid.reshape(1).astype(jnp.int32), n_slot_tiles


ROW_DMA_UNROLL = 8


def _stage_indices(idx_hbm_row, idx_smem, sem_i):
    cp = pltpu.make_async_copy(idx_hbm_row, idx_smem, sem_i)
    cp.start()
    cp.wait()


def _dispatch_kernel(p_hbm, h_ref, xs_in, xs_hbm, idx, sem_i, sem_x):
    del xs_in
    i = pl.program_id(0)
    tm = h_ref.shape[0]
    _stage_indices(p_hbm.at[i], idx, sem_i)

    def issue(r, c):
        src = h_ref.at[pl.ds(r, 1), :]
        for k in range(TOP_K):
            pltpu.make_async_copy(src, xs_hbm.at[pl.ds(idx[k * tm + r], 1), :], sem_x).start(priority=k % 2)
        return c

    lax.fori_loop(0, tm, issue, 0, unroll=ROW_DMA_UNROLL)
    for k in range(TOP_K):
        pltpu.make_async_copy(h_ref, xs_hbm.at[pl.ds(0, tm), :], sem_x).wait()


def _dispatch(h_all, slot, n_slot_rows):
    ntiles = slot.shape[0]
    d = h_all.shape[1]
    p = jnp.transpose(slot, (0, 2, 1)).reshape(ntiles, TOP_K * TM)
    xs0 = jnp.zeros((n_slot_rows, d), h_all.dtype)
    return pl.pallas_call(
        _dispatch_kernel,
        grid=(ntiles,),
        in_specs=[pl.BlockSpec(memory_space=pl.ANY), pl.BlockSpec((TM, d), lambda i: (i, 0)),
                  pl.BlockSpec(memory_space=pl.ANY)],
        out_specs=pl.BlockSpec(memory_space=pl.ANY),
        out_shape=jax.ShapeDtypeStruct((n_slot_rows, d), h_all.dtype),
        scratch_shapes=[pltpu.SMEM((TOP_K * TM,), jnp.int32),
                        pltpu.SemaphoreType.DMA(()), pltpu.SemaphoreType.DMA(())],
        input_output_aliases={2: 0},
        compiler_params=_cp("arbitrary"),
        name="dispatch",
    )(p, h_all, xs0)


def _cast_kernel(w_ref, o_ref):
    o_ref[...] = w_ref[...].astype(o_ref.dtype)


def _layer_bf16(w_stack, l):
    _, e, a, b = w_stack.shape
    return pl.pallas_call(
        _cast_kernel,
        grid=(e,),
        in_specs=[pl.BlockSpec((None, None, a, b), lambda i: (l, i, 0, 0))],
        out_specs=pl.BlockSpec((None, a, b), lambda i: (i, 0, 0)),
        out_shape=jax.ShapeDtypeStruct((e, a, b), BF16),
        compiler_params=_cp("parallel"),
        name="cast_bf16",
    )(w_stack)


def _moe_kernel(te_ref, nv_ref, x_ref, wg_ref, wu_ref, wd_ref, o_ref):
    del te_ref
    valid = pl.program_id(0) < nv_ref[0]

    @pl.when(valid)
    def _():
        half = x_ref.shape[1]
        x_lo, x_hi = _unpack_rows(x_ref[...])
        x_lo, x_hi = x_lo.astype(BF16), x_hi.astype(BF16)

        def proj(w_ref):
            return (jnp.dot(x_lo, w_ref[:half, :], preferred_element_type=F32)
                    + jnp.dot(x_hi, w_ref[half:, :], preferred_element_type=F32))

        g = proj(wg_ref)
        u = proj(wu_ref)
        hid = (g * _sigmoid(g) * u).astype(BF16)
        o_ref[...] = _pack_rows(jnp.dot(hid, wd_ref[...], preferred_element_type=F32))

    @pl.when(jnp.logical_not(valid))
    def _():
        o_ref[...] = jnp.zeros_like(o_ref)


def _moe_experts(xs, tile_expert, n_valid, w_gate, w_up, w_down, tm):
    n_rows, half = xs.shape
    d, f = w_gate.shape[1:]
    xi = lambda i, te, nv: (jnp.maximum(jnp.minimum(i, nv[0] - 1), 0), 0)
    grid_spec = pltpu.PrefetchScalarGridSpec(
        num_scalar_prefetch=2,
        grid=(n_rows // tm,),
        in_specs=[pl.BlockSpec((tm, half), xi),
                  pl.BlockSpec((None, d, f), lambda i, te, nv: (te[i], 0, 0)),
                  pl.BlockSpec((None, d, f), lambda i, te, nv: (te[i], 0, 0)),
                  pl.BlockSpec((None, f, d), lambda i, te, nv: (te[i], 0, 0))],
        out_specs=pl.BlockSpec((tm, half), lambda i, te, nv: (i, 0)),
    )
    return pl.pallas_call(
        _moe_kernel,
        grid_spec=grid_spec,
        out_shape=jax.ShapeDtypeStruct((n_rows, half), U32),
        compiler_params=_cp("arbitrary"),
        name="moe_experts",
    )(tile_expert, n_valid, xs, w_gate, w_up, w_down)


def _combine_kernel(p_hbm, ys_hbm, x_ref, slab_ref, ga_ref, g_ref, o_ref, idx, buf, sem_i, sem_x, *, final):
    i = pl.program_id(0)
    n = pl.num_programs(0)
    rows = idx.shape[0]
    tm = rows // TOP_K

    def fetch(tile, s):
        _stage_indices(p_hbm.at[tile], idx, sem_i)

        def issue(t, c):
            for k in range(TOP_K):
                r = k * tm + t
                pltpu.make_async_copy(ys_hbm.at[pl.ds(idx[r], 1), :], buf.at[s, pl.ds(r, 1), :],
                                      sem_x.at[s]).start(priority=k % 2)
            return c

        lax.fori_loop(0, tm, issue, 0, unroll=ROW_DMA_UNROLL)

    @pl.when(i == 0)
    def _():
        fetch(0, 0)

    @pl.when(i + 1 < n)
    def _():
        fetch(i + 1, (i + 1) % 2)

    s = i % 2
    pltpu.make_async_copy(ys_hbm.at[pl.ds(0, rows), :], buf.at[s], sem_x.at[s]).wait()
    slab = slab_ref[...]
    y_lo = y_hi = None
    for k in range(TOP_K):
        w_k = slab[:, TOP_K + k:TOP_K + k + 1]
        lo, hi = _unpack_rows(buf[s, k * tm:(k + 1) * tm, :])
        y_lo = w_k * lo if y_lo is None else y_lo + w_k * lo
        y_hi = w_k * hi if y_hi is None else y_hi + w_k * hi
    x = x_ref[...] + ga_ref[...] * jnp.concatenate([y_lo, y_hi], axis=1)
    if final:
        x = x * lax.rsqrt(jnp.mean(x * x, axis=-1, keepdims=True) + NORM_EPS) * g_ref[...]
    o_ref[...] = x


def _combine(ys, slot, slab, x_all, mod3, g_final, ntiles, nlt, tpb, nb, final):
    d = x_all.shape[1]
    tm = TM_COMB
    f = TM // tm
    n = ntiles * f
    p = jnp.transpose(slot.reshape(n, tm, TOP_K), (0, 2, 1)).reshape(n, TOP_K * tm)
    return pl.pallas_call(
        functools.partial(_combine_kernel, final=final),
        grid=(n,),
        in_specs=[pl.BlockSpec(memory_space=pl.ANY), pl.BlockSpec(memory_space=pl.ANY),
                  pl.BlockSpec((tm, d), lambda i: (i, 0)),
                  pl.BlockSpec((tm, 128), lambda i: (i, 0)),
                  pl.BlockSpec((None, 1, d), lambda i: (jnp.where(i < nlt * f, i // (tpb * f), nb), 0, 5)),
                  pl.BlockSpec((1, d), lambda i: (0, 0))],
        out_specs=pl.BlockSpec((tm, d), lambda i: (i, 0)),
        out_shape=jax.ShapeDtypeStruct((n * tm, d), F32),
        scratch_shapes=[pltpu.SMEM((TOP_K * tm,), jnp.int32),
                        pltpu.VMEM((2, TOP_K * tm, ys.shape[1]), ys.dtype),
                        pltpu.SemaphoreType.DMA(()), pltpu.SemaphoreType.DMA((2,))],
        compiler_params=_cp("arbitrary"),
        name="combine",
    )(p, ys, x_all, slab, mod3, g_final.reshape(1, d))


def kernel(x, c, ctx, c_ctx, w_mod, b_mod, g_norm1, g_norm2, g_final, w_in, w_merge, w_branch, w_out, w_hconv,
           hy_bias, hf_w1, hf_b1, hf_w2, hf_b2, hf_w3, hf_freq, w_sconv, w_lconv, lru_wa, lru_ba, lru_wx, lru_bx,
           lru_lambda, w_rg, b_rg, w_re, b_re, w_e_gate, w_e_up, w_e_down):
    nb, seq, d = x.shape
    cl = ctx.shape[1]
    depth = w_in.shape[0]
    db = w_branch.shape[2]
    gdim = db // FNET_GROUPS
    ngroups = w_rg.shape[2]
    epg = w_re.shape[3]
    n_experts = ngroups * epg
    t_lat, t_ctx = nb * seq, nb * cl
    t_all = t_lat + t_ctx
    nlt, tpb = t_lat // TM, seq // TM
    nat = t_all // TM
    tl_lat, tl_ctx = min(TL, seq), min(TL, cl)

    x_all = jnp.concatenate([x.reshape(t_lat, d), ctx.reshape(t_ctx, d)], axis=0)
    nrows = -(-(nb + 1) // 8) * 8
    cvec = jnp.zeros((nrows, d), F32).at[:nb].set(c).at[nb].set(c_ctx)
    mods = _adaln(cvec, w_mod, b_mod)

    use_fft = nb % 2 == 0 and seq % (8 * FFT_N2) == 0
    if use_fft:
        hyfft = _fft_consts(seq)
    else:
        c_lat, sf_lat, si_lat = _dft_mats(seq)
        fwd_lat, inv_lat = (c_lat, sf_lat), (c_lat, si_lat)
    c_ctx, sf_ctx, si_ctx = _dft_mats(cl)
    fwd_ctx, inv_ctx = (c_ctx, sf_ctx), (c_ctx, si_ctx)
    fc_lat, fs_lat, w_pq = _fnet_mats(seq, gdim, FNET_GROUPS)
    fc_ctx, fs_ctx, _ = _fnet_mats(cl, gdim, FNET_GROUPS)

    out = None
    for l in range(depth):
        last = l == depth - 1
        mod3 = mods[l].reshape(nrows, 1, 6 * d)
        u_all, h_bf = _inproj(x_all, mod3, g_norm1[l], w_in[l].astype(BF16), nlt, tpb, nb)

        wa = [_block_diag(lru_wa[l, dd]) for dd in range(2)]
        wx = [_block_diag(lru_wx[l, dd]) for dd in range(2)]
        ba = [lru_ba[l, dd].reshape(1, db) for dd in range(2)]
        bx = [lru_bx[l, dd].reshape(1, db) for dd in range(2)]
        lam = [lru_lambda[l, dd].reshape(1, db) for dd in range(2)]
        zeros_st = jnp.zeros((nb, 1, db), F32)
        rows_out = t_lat if last else t_all

        def lru(row0, length, tl, rw, h0f, h0b, combine, out_init):
            hf, stf = _lru_pass(u_all, w_lconv[l], wa[0], wx[0], ba[0], bx[0], lam[0], h0f,
                                row0, nb, length, tl, rw, db, False)
            if combine:
                y, stb = _lru_pass(u_all, w_lconv[l], wa[1], wx[1], ba[1], bx[1], lam[1], h0b,
                                   row0, nb, length, tl, rw, db, True, hf=hf, out_init=out_init, t_all=rows_out)
            else:
                y, stb = _lru_pass(u_all, w_lconv[l], wa[1], wx[1], ba[1], bx[1], lam[1], h0b,
                                   row0, nb, length, tl, rw, db, True)
            return y, stf, stb

        bias0, bias1 = hy_bias[l, 0].reshape(1, db), hy_bias[l, 1].reshape(1, db)

        def hyena_dense(row0, length, fwd, inv, v, x1, x2, v_bf, out_init):
            filt = _hyena_filters(length, hf_w1[l], hf_b1[l], hf_w2[l], hf_b2[l], hf_w3[l], hf_freq[l], db)
            pqr = _filter_spectra(fwd, filt, length, db)
            yhat = _hy_fwd(fwd, v_bf, nb, length, db, pqr[0])
            z2 = _hy_inv(inv, yhat, v, 0, x1, 0, bias0, nb, length, db, 0, nb * length, None)
            yhat = _hy_fwd(fwd, z2, nb, length, db, pqr[1])
            return _hy_inv(inv, yhat, z2, 0, x2, 0, bias1, nb, length, db, row0, rows_out, out_init)

        def hyena_fft(length, v, x1, x2, v_bf, out_init):
            filt = _hyena_filters(length, hf_w1[l], hf_b1[l], hf_w2[l], hf_b2[l], hf_w3[l], hf_freq[l], db,
                                  circular=True)
            kspec = _fft_filter_spectra(hyfft, filt, length, db)
            z2 = _fft_conv(hyfft, v_bf, kspec[0], v, x1, bias0, nb, db, None, nb * length)
            return _fft_conv(hyfft, z2, kspec[1], z2, x2, bias1, nb, db, out_init, rows_out)

        zbuf = lambda: None if last else jnp.zeros((rows_out, db), BF16)
        if last:
            _, st_f, st_b = lru(t_lat, cl, tl_ctx, cl, zeros_st, zeros_st, False, None)
        else:
            y_lru_c, st_f, st_b = lru(t_lat, cl, tl_ctx, cl, zeros_st, zeros_st, True, zbuf())
        y_lru, _, _ = lru(0, seq, tl_lat, GRID_W, st_f, st_b, True, None if last else y_lru_c)

        v, x1, x2, v_bf, y_sc = _conv_stage(u_all, w_hconv[l], w_sconv[l], 0, t_lat, tl_lat, GRID_W, db,
                                            jnp.zeros((rows_out, db), BF16))
        y_fn = _fnet_fused(fc_lat, fs_lat, u_all, w_pq, 0, nb, seq, db, gdim, zbuf(), rows_out)
        if not last:
            vc, x1c, x2c, vc_bf, y_sc = _conv_stage(u_all, w_hconv[l], w_sconv[l], t_lat, t_ctx, tl_ctx, cl, db,
                                                    y_sc)
            y_fn = _fnet_fused(fc_ctx, fs_ctx, u_all, w_pq, t_lat, nb, cl, db, gdim, y_fn, rows_out)
            y_hy = hyena_dense(t_lat, cl, fwd_ctx, inv_ctx, vc, x1c, x2c, vc_bf, zbuf())
        else:
            y_hy = None
        if use_fft:
            y_hy = hyena_fft(seq, v, x1, x2, v_bf, y_hy)
        else:
            y_hy = hyena_dense(0, seq, fwd_lat, inv_lat, v, x1, x2, v_bf, y_hy)

        ntiles = nlt if last else nat
        x_mid = _merge(h_bf, (y_fn, y_hy, y_sc, y_lru), w_merge[l].astype(BF16), w_branch[l].astype(BF16),
                       w_out[l].astype(BF16), x_all, mod3, ntiles, nlt, tpb, nb)

        w_router = jnp.zeros((d, 128), F32).at[:, :ngroups].set(w_rg[l])
        w_router = w_router.at[:, ngroups:ngroups + n_experts].set(
            jnp.transpose(w_re[l], (1, 0, 2)).reshape(d, n_experts))
        b_router = jnp.zeros((1, 128), F32).at[0, :ngroups].set(b_rg[l])
        b_router = b_router.at[0, ngroups:ngroups + n_experts].set(b_re[l].reshape(-1))
        h2, slab, tile_counts = _router(x_mid, mod3, g_norm2[l], w_router, b_router, ntiles, nlt, tpb, nb,
                                        ngroups, epg)
        slot, tile_expert, n_valid, n_slot_tiles = _route_tables(slab, tile_counts, ngroups, n_experts, TM_MOE)
        xs = _dispatch(h2, slot, n_slot_tiles * TM_MOE)
        ys = _moe_experts(xs, tile_expert, n_valid, _layer_bf16(w_e_gate, l), _layer_bf16(w_e_up, l),
                          _layer_bf16(w_e_down, l), TM_MOE)
        x_all = _combine(ys, slot, slab, x_mid, mod3, g_final, ntiles, nlt, tpb, nb, last)
        if last:
            out = x_all.reshape(nb, seq, d)
    return out
```
